```python
import jax, jax.numpy as jnp
from jax import lax
import numpy as np

D_MODEL = 1024
BATCH = 8
SEQ = 2048
DEPTH = 1
DEC_BATCH = 128
DEC_SEQ = 4
PAST_LEN = 16384
PAGE_SIZE = 128

CHUNK = 128
H_A = 8
W_A = D_MODEL
DH_A = W_A // H_A
H_B = 8
W_B = D_MODEL
CONV_W = 3
P_DIM = 256
EPS = 1e-6
LN_EPS = 1e-5
SPLIT_WIDTHS = [W_A, W_A, W_A, W_B, W_B, W_B, W_B, D_MODEL, D_MODEL]
SPLIT_IDX = [int(i) for i in np.cumsum(SPLIT_WIDTHS)[:-1]]
IN_COLS = int(sum(SPLIT_WIDTHS))

kernel_name = "gated_parallel_gmlp_shortconv_decoder_step"


def rms_norm(x, g):
    xf = x.astype(jnp.float32)
    y = xf * lax.rsqrt(jnp.mean(xf * xf, axis=-1, keepdims=True) + EPS)
    return (y * g.astype(jnp.float32)).astype(x.dtype)


def layer_norm(x, g, b):
    xf = x.astype(jnp.float32)
    mu = jnp.mean(xf, axis=-1, keepdims=True)
    var = jnp.mean(jnp.square(xf - mu), axis=-1, keepdims=True)
    y = (xf - mu) * lax.rsqrt(var + LN_EPS)
    return (y * g.astype(jnp.float32) + b.astype(jnp.float32)).astype(x.dtype)


def chunk_spatial_mix(v, w_s, b_s):
    bsz, L, _ = v.shape
    n = -(-L // CHUNK)
    pad = n * CHUNK - L
    vp = jnp.pad(v, ((0, 0), (0, pad), (0, 0))).reshape(bsz, n, CHUNK, H_A, DH_A)
    mask = jnp.tril(jnp.ones((CHUNK, CHUNK), dtype=bool))
    w = jnp.where(mask[None], w_s, jnp.zeros_like(w_s))
    s = jnp.einsum('hts,bnshd->bnthd', w, vp) + jnp.transpose(b_s)[None, None, :, :, None]
    return s.reshape(bsz, n * CHUNK, W_A)[:, :L]


def causal_conv(u, state, conv_w):
    L = u.shape[1]
    full = jnp.concatenate([state.astype(u.dtype), u], axis=1)
    y = conv_w[0] * full[:, 0:L]
    for k in range(1, CONV_W):
        y = y + conv_w[k] * full[:, k:k + L]
    return y, full[:, -(CONV_W - 1):]


def mixer_layer(x, p, conv_state, norm_g, w_in, ln_v_g, ln_v_b, w_s, b_s, conv_w,
                w_a_out, w_b_out, w_o, pe_norm_g, w_pe_gate, w_pe_proj):
    h = rms_norm(x, norm_g)
    z = jnp.einsum('bld,de->ble', h, w_in)
    u_a, v_a, gate_a, c_b, b_b, h_b, gate_b, m_a, m_b = jnp.split(z, SPLIT_IDX, axis=-1)
    u_a = jax.nn.gelu(u_a)
    v_a = layer_norm(jax.nn.gelu(v_a), ln_v_g, ln_v_b)
    s = chunk_spatial_mix(v_a, w_s, b_s)
    y_a = u_a * s * jax.nn.silu(gate_a)
    conv_out, new_conv = causal_conv(c_b * h_b, conv_state, conv_w)
    y_b = b_b * conv_out * jax.nn.silu(gate_b)
    merged = (jax.nn.sigmoid(m_a) * jnp.einsum('blw,wd->bld', y_a, w_a_out)
              + jax.nn.sigmoid(m_b) * jnp.einsum('blw,wd->bld', y_b, w_b_out))
    x = x + jnp.einsum('bld,de->ble', merged, w_o)
    pe_gate = jax.nn.sigmoid(jnp.einsum('bld,de->ble', rms_norm(x, pe_norm_g), w_pe_gate))
    x = x + pe_gate * jnp.einsum('blp,pd->bld', p, w_pe_proj)
    L = v_a.shape[1]
    start = ((L - 1) // CHUNK) * CHUNK
    return x, new_conv, v_a[:, start:]


def setup_inputs(seed: int = 0) -> dict:
    key = jax.random.key(seed)
    ks = jax.random.split(key, 20)
    f32 = jnp.float32
    nrm = lambda k, shape, scale: jax.random.normal(k, shape, f32) * scale
    return {
        "x_prompt": nrm(ks[0], (BATCH, SEQ, D_MODEL), 1.0),
        "x_sample": nrm(ks[1], (DEC_BATCH, DEC_SEQ, D_MODEL), 1.0),
        "state_conv": nrm(ks[2], (DEPTH, DEC_BATCH, CONV_W - 1, W_B), 0.5),
        "p_prompt": nrm(ks[3], (DEPTH, BATCH, SEQ, P_DIM), 1.0),
        "p_sample": nrm(ks[4], (DEPTH, DEC_BATCH, DEC_SEQ, P_DIM), 1.0),
        "norm_g": 1.0 + nrm(ks[5], (DEPTH, D_MODEL), 0.1),
        "w_in": nrm(ks[6], (DEPTH, D_MODEL, IN_COLS), D_MODEL ** -0.5),
        "ln_v_g": 1.0 + nrm(ks[7], (DEPTH, W_A), 0.1),
        "ln_v_b": nrm(ks[8], (DEPTH, W_A), 0.02),
        "w_s": nrm(ks[9], (DEPTH, H_A, CHUNK, CHUNK), 0.5 * CHUNK ** -0.5),
        "b_s": 1.0 + nrm(ks[10], (DEPTH, H_A, CHUNK), 0.1),
        "conv_w": nrm(ks[11], (DEPTH, CONV_W, W_B), CONV_W ** -0.5),
        "w_a_out": nrm(ks[12], (DEPTH, W_A, D_MODEL), W_A ** -0.5),
        "w_b_out": nrm(ks[13], (DEPTH, W_B, D_MODEL), W_B ** -0.5),
        "w_o": nrm(ks[14], (DEPTH, D_MODEL, D_MODEL), D_MODEL ** -0.5),
        "pe_norm_g": 1.0 + nrm(ks[15], (DEPTH, D_MODEL), 0.1),
        "w_pe_gate": nrm(ks[16], (DEPTH, D_MODEL, D_MODEL), D_MODEL ** -0.5),
        "w_pe_proj": nrm(ks[17], (DEPTH, P_DIM, D_MODEL), P_DIM ** -0.5),
        "final_norm_g": 1.0 + nrm(ks[18], (D_MODEL,), 0.1),
    }


def reference(x_prompt, x_sample, state_conv, p_prompt, p_sample, norm_g, w_in, ln_v_g,
              ln_v_b, w_s, b_s, conv_w, w_a_out, w_b_out, w_o, pe_norm_g, w_pe_gate,
              w_pe_proj, final_norm_g):
    xp, xs = x_prompt, x_sample
    conv_p_list, conv_s_list, v_p_list, v_s_list = [], [], [], []
    zero_state = jnp.zeros((x_prompt.shape[0], CONV_W - 1, W_B), x_prompt.dtype)
    for i in range(DEPTH):
        params = (norm_g[i], w_in[i], ln_v_g[i], ln_v_b[i], w_s[i], b_s[i], conv_w[i],
                  w_a_out[i], w_b_out[i], w_o[i], pe_norm_g[i], w_pe_gate[i], w_pe_proj[i])
        xp, cp, vp = mixer_layer(xp, p_prompt[i], zero_state, *params)
        xs, cs, vs = mixer_layer(xs, p_sample[i], state_conv[i], *params)
        conv_p_list.append(cp)
        conv_s_list.append(cs)
        v_p_list.append(vp)
        v_s_list.append(vs)
    y_prompt = rms_norm(xp, final_norm_g)
    y_sample = rms_norm(xs, final_norm_g)
    conv_state_prompt = jnp.stack(conv_p_list)
    conv_state_sample = jnp.stack(conv_s_list)
    v_rows_prompt = jnp.stack(v_p_list)
    v_rows_sample = jnp.stack(v_s_list)
    return (y_prompt, y_sample, conv_state_prompt, conv_state_sample, v_rows_prompt, v_rows_sample)
```

```python
import functools
import math

import jax
import jax.numpy as jnp
from jax import lax
from jax.experimental import pallas as pl
from jax.experimental.pallas import tpu as pltpu

CHUNK = 128
N_HEADS = 8
HEAD_W = 128
CONV_W = 3
EPS = 1e-6
LN_EPS = 1e-5
COL_U, COL_V, COL_GA, COL_C, COL_B, COL_H, COL_GB, COL_MA, COL_MB = range(9)

V7X_VMEM_LIMIT_BYTES = 60000 * 1024
PROMPT_TILE = 256
CARRY_ROWS = 8


def _gelu(x):
    c = math.sqrt(2.0 / math.pi)
    return x * (0.5 * (1.0 + jnp.tanh(c * (x + 0.044715 * (x * x * x)))))


def _sigmoid(x):
    return 0.5 * jnp.tanh(0.5 * x) + 0.5


def _silu(x):
    return x * _sigmoid(x)


def _rms_norm(x, g):
    return x * lax.rsqrt(jnp.mean(x * x, axis=-1, keepdims=True) + EPS) * g


def _layer_norm(x, g, b):
    mu = jnp.mean(x, axis=-1, keepdims=True)
    xc = x - mu
    var = jnp.mean(xc * xc, axis=-1, keepdims=True)
    return xc * lax.rsqrt(var + LN_EPS) * g + b


def _dot(a, b):
    return jnp.dot(a, b, preferred_element_type=jnp.float32)


def _proj(h, w_in_ref, col, d):
    return _dot(h, w_in_ref[:, col * d:(col + 1) * d])


def _merge_and_finish(x, p, y_a, y_b, h, w_in_ref, w_a_ref, w_b_ref, w_o_ref,
                      pe_g_ref, w_pg_ref, w_pp_ref, fin_g_ref):
    d = x.shape[-1]
    bf = jnp.bfloat16
    m_a = _sigmoid(_proj(h, w_in_ref, COL_MA, d))
    m_b = _sigmoid(_proj(h, w_in_ref, COL_MB, d))
    merged = (m_a * _dot(y_a.astype(bf), w_a_ref[...])
              + m_b * _dot(y_b.astype(bf), w_b_ref[...]))
    x = x + _dot(merged.astype(bf), w_o_ref[...])
    gate = _sigmoid(_dot(_rms_norm(x, pe_g_ref[...]).astype(bf), w_pg_ref[...]))
    x = x + gate * _dot(p.astype(bf), w_pp_ref[...])
    return _rms_norm(x, fin_g_ref[...])


def _prompt_kernel(x_ref, p_ref, ng_ref, w_in_ref, lng_ref, lnb_ref, ws_ref, bs_ref,
                   cw_ref, w_a_ref, w_b_ref, w_o_ref, pe_g_ref, w_pg_ref, w_pp_ref,
                   fin_g_ref, y_ref, conv_ref, vrow_ref, pbuf_ref):
    i = pl.program_id(1)
    last = pl.num_programs(1) - 1
    tm, d = x_ref.shape[1], x_ref.shape[2]
    bf = jnp.bfloat16

    x = x_ref[0]
    h = _rms_norm(x, ng_ref[...]).astype(bf)

    v = _layer_norm(_gelu(_proj(h, w_in_ref, COL_V, d)), lng_ref[...], lnb_ref[...])

    @pl.when(i == last)
    def _():
        vrow_ref[0, 0] = v[tm - CHUNK:, :]

    v_b = v.astype(bf)
    row = lax.broadcasted_iota(jnp.int32, (CHUNK, CHUNK), 0)
    col = lax.broadcasted_iota(jnp.int32, (CHUNK, CHUNK), 1)
    s_rows = []
    for c in range(tm // CHUNK):
        s_heads = []
        for hd in range(N_HEADS):
            w = jnp.where(col <= row, ws_ref[hd], 0.0).astype(bf)
            s_heads.append(_dot(w, v_b[c * CHUNK:(c + 1) * CHUNK,
                                       hd * HEAD_W:(hd + 1) * HEAD_W]))
        s_rows.append(jnp.concatenate(s_heads, axis=1) + bs_ref[...])
    s = jnp.concatenate(s_rows, axis=0)
    y_a = (_gelu(_proj(h, w_in_ref, COL_U, d)) * s
           * _silu(_proj(h, w_in_ref, COL_GA, d)))

    @pl.when(i == 0)
    def _():
        pbuf_ref[0:CARRY_ROWS, :] = jnp.zeros((CARRY_ROWS, d), jnp.float32)

    pbuf_ref[CARRY_ROWS:CARRY_ROWS + tm, :] = (
        _proj(h, w_in_ref, COL_C, d) * _proj(h, w_in_ref, COL_H, d))
    conv = (cw_ref[0:1, :] * pbuf_ref[CARRY_ROWS - 2:CARRY_ROWS - 2 + tm, :]
            + cw_ref[1:2, :] * pbuf_ref[CARRY_ROWS - 1:CARRY_ROWS - 1 + tm, :]
            + cw_ref[2:3, :] * pbuf_ref[CARRY_ROWS:CARRY_ROWS + tm, :])
    tail = pbuf_ref[tm:tm + CARRY_ROWS, :]
    pbuf_ref[0:CARRY_ROWS, :] = tail

    @pl.when(i == last)
    def _():
        conv_ref[0, 0] = tail[CARRY_ROWS - (CONV_W - 1):, :]

    y_b = (_proj(h, w_in_ref, COL_B, d) * conv
           * _silu(_proj(h, w_in_ref, COL_GB, d)))

    y_ref[0] = _merge_and_finish(x, p_ref[0], y_a, y_b, h, w_in_ref, w_a_ref, w_b_ref,
                                 w_o_ref, pe_g_ref, w_pg_ref, w_pp_ref, fin_g_ref)


def _sample_kernel(x_ref, p_ref, st_ref, ng_ref, w_in_ref, lng_ref, lnb_ref, wmix_ref,
                   bs_ref, cw_ref, w_a_ref, w_b_ref, w_o_ref, pe_g_ref, w_pg_ref,
                   w_pp_ref, fin_g_ref, y_ref, conv_ref, vrow_ref, *, n_seq, n_t):
    d = x_ref.shape[1]
    bf = jnp.bfloat16

    def slab(a, t):
        return a[t * n_seq:(t + 1) * n_seq, :]

    x = x_ref[...]
    h = _rms_norm(x, ng_ref[...]).astype(bf)

    v = _layer_norm(_gelu(_proj(h, w_in_ref, COL_V, d)), lng_ref[...], lnb_ref[...])
    vrow_ref[...] = v
    s_slabs = []
    for t in range(n_t):
        acc = bs_ref[t:t + 1, :] + wmix_ref[t * n_t:t * n_t + 1, :] * slab(v, 0)
        for j in range(1, t + 1):
            acc = acc + wmix_ref[t * n_t + j:t * n_t + j + 1, :] * slab(v, j)
        s_slabs.append(acc)
    s = jnp.concatenate(s_slabs, axis=0)
    y_a = (_gelu(_proj(h, w_in_ref, COL_U, d)) * s
           * _silu(_proj(h, w_in_ref, COL_GA, d)))

    prod = _proj(h, w_in_ref, COL_C, d) * _proj(h, w_in_ref, COL_H, d)
    full = [st_ref[0:n_seq, :], st_ref[n_seq:2 * n_seq, :]] + [
        slab(prod, t) for t in range(n_t)]
    conv = jnp.concatenate(
        [cw_ref[0:1, :] * full[t] + cw_ref[1:2, :] * full[t + 1]
         + cw_ref[2:3, :] * full[t + 2] for t in range(n_t)], axis=0)
    conv_ref[...] = jnp.concatenate(full[-(CONV_W - 1):], axis=0)
    y_b = (_proj(h, w_in_ref, COL_B, d) * conv
           * _silu(_proj(h, w_in_ref, COL_GB, d)))

    y_ref[...] = _merge_and_finish(x, p_ref[...], y_a, y_b, h, w_in_ref, w_a_ref,
                                   w_b_ref, w_o_ref, pe_g_ref, w_pg_ref, w_pp_ref,
                                   fin_g_ref)


def _resident(shape):
    return pl.BlockSpec(shape, lambda *_: (0,) * len(shape),
                        pipeline_mode=pl.Buffered(1))


def kernel(x_prompt, x_sample, state_conv, p_prompt, p_sample, norm_g, w_in, ln_v_g,
           ln_v_b, w_s, b_s, conv_w, w_a_out, w_b_out, w_o, pe_norm_g, w_pe_gate,
           w_pe_proj, final_norm_g):
    depth = w_in.shape[0]
    assert depth == 1, "single-layer step only"
    batch, seq, d = x_prompt.shape
    n_seq, n_t, _ = x_sample.shape
    p_dim = p_prompt.shape[-1]
    tm = PROMPT_TILE
    assert seq % tm == 0 and tm % CHUNK == 0 and d == N_HEADS * HEAD_W
    assert n_t <= CHUNK and n_t >= CONV_W - 1
    f32, bf = jnp.float32, jnp.bfloat16

    row2 = lambda a: a.reshape(1, -1)
    w_in_b = w_in[0].astype(bf)
    weights_tail = (conv_w[0], w_a_out[0].astype(bf), w_b_out[0].astype(bf),
                    w_o[0].astype(bf), row2(pe_norm_g[0]), w_pe_gate[0].astype(bf),
                    w_pe_proj[0].astype(bf), row2(final_norm_g))
    weights_head = (row2(norm_g[0]), w_in_b, row2(ln_v_g[0]), row2(ln_v_b[0]))
    bs_full = jnp.repeat(b_s[0].T, HEAD_W, axis=1)

    def specs(arrs):
        return [_resident(a.shape) for a in arrs]

    params = pltpu.CompilerParams(
        dimension_semantics=("arbitrary", "arbitrary"),
        vmem_limit_bytes=V7X_VMEM_LIMIT_BYTES)

    prompt_w = weights_head + (w_s[0], bs_full) + weights_tail
    y_p, conv_p, v_p = pl.pallas_call(
        _prompt_kernel,
        grid=(batch, seq // tm),
        in_specs=[pl.BlockSpec((1, tm, d), lambda b, i: (b, i, 0)),
                  pl.BlockSpec((1, tm, p_dim), lambda b, i: (b, i, 0))] + specs(prompt_w),
        out_specs=[pl.BlockSpec((1, tm, d), lambda b, i: (b, i, 0)),
                   pl.BlockSpec((1, 1, CONV_W - 1, d), lambda b, i: (0, b, 0, 0)),
                   pl.BlockSpec((1, 1, CHUNK, d), lambda b, i: (0, b, 0, 0))],
        out_shape=[jax.ShapeDtypeStruct((batch, seq, d), f32),
                   jax.ShapeDtypeStruct((1, batch, CONV_W - 1, d), f32),
                   jax.ShapeDtypeStruct((1, batch, CHUNK, d), f32)],
        scratch_shapes=[pltpu.VMEM((tm + CARRY_ROWS, d), f32)],
        compiler_params=params,
        name="prompt_layer",
    )(x_prompt, p_prompt[0], *prompt_w)

    rows = n_seq * n_t
    tmaj = lambda a: jnp.transpose(a, (1, 0, 2)).reshape(-1, a.shape[-1])
    from_tmaj = lambda a, t: jnp.transpose(a.reshape(t, n_seq, d), (1, 0, 2))
    wmix = jnp.repeat(jnp.transpose(w_s[0][:, :n_t, :n_t], (1, 2, 0)), HEAD_W,
                      axis=2).reshape(n_t * n_t, d)
    sample_w = weights_head + (wmix, bs_full[:n_t]) + weights_tail
    sample_in = (tmaj(x_sample), tmaj(p_sample[0]), tmaj(state_conv[0]))
    y_s, conv_s, v_s = pl.pallas_call(
        functools.partial(_sample_kernel, n_seq=n_seq, n_t=n_t),
        grid=(1,),
        in_specs=specs(sample_in + sample_w),
        out_specs=[_resident((rows, d)), _resident(((CONV_W - 1) * n_seq, d)),
                   _resident((rows, d))],
        out_shape=[jax.ShapeDtypeStruct((rows, d), f32),
                   jax.ShapeDtypeStruct(((CONV_W - 1) * n_seq, d), f32),
                   jax.ShapeDtypeStruct((rows, d), f32)],
        compiler_params=pltpu.CompilerParams(
            dimension_semantics=("arbitrary",),
            vmem_limit_bytes=V7X_VMEM_LIMIT_BYTES),
        name="sample_layer",
    )(*sample_in, *sample_w)

    return (y_p, from_tmaj(y_s, n_t), conv_p, from_tmaj(conv_s, CONV_W - 1)[None],
            v_p, from_tmaj(v_s, n_t)[None])
```

```python
import functools
import math

import jax
import jax.numpy as jnp
from jax import lax
from jax.experimental import pallas as pl
from jax.experimental.pallas import tpu as pltpu

CHUNK = 128
N_HEADS = 8
HEAD_W = 128
CONV_W = 3
EPS = 1e-6
LN_EPS = 1e-5
COL_U, COL_V, COL_GA, COL_C, COL_B, COL_H, COL_GB, COL_MA, COL_MB = range(9)

V7X_VMEM_LIMIT_BYTES = 60000 * 1024
PROMPT_TILE = 512
CARRY_ROWS = 8


def _gelu(x):
    c = math.sqrt(2.0 / math.pi)
    return x * (0.5 * (1.0 + jnp.tanh(c * (x + 0.044715 * (x * x * x)))))


def _sigmoid(x):
    return 0.5 * jnp.tanh(0.5 * x) + 0.5


def _silu(x):
    return x * _sigmoid(x)


def _rms_norm(x, g):
    return x * lax.rsqrt(jnp.mean(x * x, axis=-1, keepdims=True) + EPS) * g


def _layer_norm(x, g, b):
    mu = jnp.mean(x, axis=-1, keepdims=True)
    xc = x - mu
    var = jnp.mean(xc * xc, axis=-1, keepdims=True)
    return xc * lax.rsqrt(var + LN_EPS) * g + b


def _dot(a, b):
    return jnp.dot(a, b, preferred_element_type=jnp.float32)


def _proj(h, w_in_ref, col, d):
    return _dot(h, w_in_ref[:, col * d:(col + 1) * d])


def _merge_and_finish(x, p, y_a, y_b, h, w_in_ref, w_a_ref, w_b_ref, w_o_ref,
                      pe_g_ref, w_pg_ref, w_pp_ref, fin_g_ref):
    d = x.shape[-1]
    bf = jnp.bfloat16
    m_a = _sigmoid(_proj(h, w_in_ref, COL_MA, d))
    m_b = _sigmoid(_proj(h, w_in_ref, COL_MB, d))
    merged = (m_a * _dot(y_a.astype(bf), w_a_ref[...])
              + m_b * _dot(y_b.astype(bf), w_b_ref[...]))
    x = x + _dot(merged.astype(bf), w_o_ref[...])
    gate = _sigmoid(_dot(_rms_norm(x, pe_g_ref[...]).astype(bf), w_pg_ref[...]))
    x = x + gate * _dot(p.astype(bf), w_pp_ref[...])
    return _rms_norm(x, fin_g_ref[...])


def _prompt_kernel(x_ref, p_ref, ng_ref, w_in_ref, lng_ref, lnb_ref, ws_ref, bs_ref,
                   cw_ref, w_a_ref, w_b_ref, w_o_ref, pe_g_ref, w_pg_ref, w_pp_ref,
                   fin_g_ref, y_ref, conv_ref, vrow_ref, pbuf_ref):
    tm, d = x_ref.shape[1], x_ref.shape[2]
    bf = jnp.bfloat16

    @pl.when(pl.program_id(1) == 0)
    def _():
        pbuf_ref[0:CARRY_ROWS, :] = jnp.zeros((CARRY_ROWS, d), jnp.float32)

    x = x_ref[0]
    h = _rms_norm(x, ng_ref[...]).astype(bf)

    v = _layer_norm(_gelu(_proj(h, w_in_ref, COL_V, d)), lng_ref[...], lnb_ref[...])
    vrow_ref[0, 0] = v[tm - CHUNK:, :]

    v_b = v.astype(bf)
    row = lax.broadcasted_iota(jnp.int32, (CHUNK, CHUNK), 0)
    col = lax.broadcasted_iota(jnp.int32, (CHUNK, CHUNK), 1)
    s_rows = []
    for c in range(tm // CHUNK):
        s_heads = []
        for hd in range(N_HEADS):
            w = jnp.where(col <= row, ws_ref[hd], 0.0).astype(bf)
            s_heads.append(_dot(w, v_b[c * CHUNK:(c + 1) * CHUNK,
                                       hd * HEAD_W:(hd + 1) * HEAD_W]))
        s_rows.append(jnp.concatenate(s_heads, axis=1) + bs_ref[...])
    s = jnp.concatenate(s_rows, axis=0)
    y_a = (_gelu(_proj(h, w_in_ref, COL_U, d)) * s
           * _silu(_proj(h, w_in_ref, COL_GA, d)))

    pbuf_ref[CARRY_ROWS:CARRY_ROWS + tm, :] = (
        _proj(h, w_in_ref, COL_C, d) * _proj(h, w_in_ref, COL_H, d))
    conv = (cw_ref[0:1, :] * pbuf_ref[CARRY_ROWS - 2:CARRY_ROWS - 2 + tm, :]
            + cw_ref[1:2, :] * pbuf_ref[CARRY_ROWS - 1:CARRY_ROWS - 1 + tm, :]
            + cw_ref[2:3, :] * pbuf_ref[CARRY_ROWS:CARRY_ROWS + tm, :])
    tail = pbuf_ref[tm:tm + CARRY_ROWS, :]
    pbuf_ref[0:CARRY_ROWS, :] = tail
    conv_ref[0, 0] = tail[CARRY_ROWS - (CONV_W - 1):, :]

    y_b = (_proj(h, w_in_ref, COL_B, d) * conv
           * _silu(_proj(h, w_in_ref, COL_GB, d)))

    y_ref[0] = _merge_and_finish(x, p_ref[0], y_a, y_b, h, w_in_ref, w_a_ref, w_b_ref,
                                 w_o_ref, pe_g_ref, w_pg_ref, w_pp_ref, fin_g_ref)


def _sample_kernel(x_ref, p_ref, st_ref, ng_ref, w_in_ref, lng_ref, lnb_ref, wmix_ref,
                   bs_ref, cw_ref, w_a_ref, w_b_ref, w_o_ref, pe_g_ref, w_pg_ref,
                   w_pp_ref, fin_g_ref, y_ref, conv_ref, vrow_ref, *, n_seq, n_t):
    d = x_ref.shape[1]
    bf = jnp.bfloat16

    def slab(a, t):
        return a[t * n_seq:(t + 1) * n_seq, :]

    x = x_ref[...]
    h = _rms_norm(x, ng_ref[...]).astype(bf)

    v = _layer_norm(_gelu(_proj(h, w_in_ref, COL_V, d)), lng_ref[...], lnb_ref[...])
    vrow_ref[...] = v
    s_slabs = []
    for t in range(n_t):
        acc = bs_ref[t:t + 1, :] + wmix_ref[t * n_t:t * n_t + 1, :] * slab(v, 0)
        for j in range(1, t + 1):
            acc = acc + wmix_ref[t * n_t + j:t * n_t + j + 1, :] * slab(v, j)
        s_slabs.append(acc)
    s = jnp.concatenate(s_slabs, axis=0)
    y_a = (_gelu(_proj(h, w_in_ref, COL_U, d)) * s
           * _silu(_proj(h, w_in_ref, COL_GA, d)))

    prod = _proj(h, w_in_ref, COL_C, d) * _proj(h, w_in_ref, COL_H, d)
    full = [st_ref[0:n_seq, :], st_ref[n_seq:2 * n_seq, :]] + [
        slab(prod, t) for t in range(n_t)]
    conv = jnp.concatenate(
        [cw_ref[0:1, :] * full[t] + cw_ref[1:2, :] * full[t + 1]
         + cw_ref[2:3, :] * full[t + 2] for t in range(n_t)], axis=0)
    conv_ref[...] = jnp.concatenate(full[-(CONV_W - 1):], axis=0)
    y_b = (_proj(h, w_in_ref, COL_B, d) * conv
           * _silu(_proj(h, w_in_ref, COL_GB, d)))

    y_ref[...] = _merge_and_finish(x, p_ref[...], y_a, y_b, h, w_in_ref, w_a_ref,
                                   w_b_ref, w_o_ref, pe_g_ref, w_pg_ref, w_pp_ref,
                                   fin_g_ref)


def _resident(shape):
    return pl.BlockSpec(shape, lambda *_: (0,) * len(shape),
                        pipeline_mode=pl.Buffered(1))


def kernel(x_prompt, x_sample, state_conv, p_prompt, p_sample, norm_g, w_in, ln_v_g,
           ln_v_b, w_s, b_s, conv_w, w_a_out, w_b_out, w_o, pe_norm_g, w_pe_gate,
           w_pe_proj, final_norm_g):
    depth = w_in.shape[0]
    assert depth == 1, "single-layer step only"
    batch, seq, d = x_prompt.shape
    n_seq, n_t, _ = x_sample.shape
    p_dim = p_prompt.shape[-1]
    tm = PROMPT_TILE
    assert seq % tm == 0 and tm % CHUNK == 0 and d == N_HEADS * HEAD_W
    assert n_t <= CHUNK and n_t >= CONV_W - 1
    f32, bf = jnp.float32, jnp.bfloat16

    row2 = lambda a: a.reshape(1, -1)
    w_in_b = w_in[0].astype(bf)
    weights_tail = (conv_w[0], w_a_out[0].astype(bf), w_b_out[0].astype(bf),
                    w_o[0].astype(bf), row2(pe_norm_g[0]), w_pe_gate[0].astype(bf),
                    w_pe_proj[0].astype(bf), row2(final_norm_g))
    weights_head = (row2(norm_g[0]), w_in_b, row2(ln_v_g[0]), row2(ln_v_b[0]))
    bs_full = jnp.repeat(b_s[0].T, HEAD_W, axis=1)

    def specs(arrs):
        return [_resident(a.shape) for a in arrs]

    params = pltpu.CompilerParams(
        dimension_semantics=("arbitrary", "arbitrary"),
        vmem_limit_bytes=V7X_VMEM_LIMIT_BYTES)

    prompt_w = weights_head + (w_s[0], bs_full) + weights_tail
    y_p, conv_p, v_p = pl.pallas_call(
        _prompt_kernel,
        grid=(batch, seq // tm),
        in_specs=[pl.BlockSpec((1, tm, d), lambda b, i: (b, i, 0)),
                  pl.BlockSpec((1, tm, p_dim), lambda b, i: (b, i, 0))] + specs(prompt_w),
        out_specs=[pl.BlockSpec((1, tm, d), lambda b, i: (b, i, 0)),
                   pl.BlockSpec((1, 1, CONV_W - 1, d), lambda b, i: (0, b, 0, 0)),
                   pl.BlockSpec((1, 1, CHUNK, d), lambda b, i: (0, b, 0, 0))],
        out_shape=[jax.ShapeDtypeStruct((batch, seq, d), f32),
                   jax.ShapeDtypeStruct((1, batch, CONV_W - 1, d), f32),
                   jax.ShapeDtypeStruct((1, batch, CHUNK, d), f32)],
        scratch_shapes=[pltpu.VMEM((tm + CARRY_ROWS, d), f32)],
        compiler_params=params,
        name="prompt_layer",
    )(x_prompt, p_prompt[0], *prompt_w)

    rows = n_seq * n_t
    tmaj = lambda a: jnp.transpose(a, (1, 0, 2)).reshape(-1, a.shape[-1])
    from_tmaj = lambda a, t: jnp.transpose(a.reshape(t, n_seq, d), (1, 0, 2))
    wmix = jnp.repeat(jnp.transpose(w_s[0][:, :n_t, :n_t], (1, 2, 0)), HEAD_W,
                      axis=2).reshape(n_t * n_t, d)
    sample_w = weights_head + (wmix, bs_full[:n_t]) + weights_tail
    sample_in = (tmaj(x_sample), tmaj(p_sample[0]), tmaj(state_conv[0]))
    y_s, conv_s, v_s = pl.pallas_call(
        functools.partial(_sample_kernel, n_seq=n_seq, n_t=n_t),
        grid=(1,),
        in_specs=specs(sample_in + sample_w),
        out_specs=[_resident((rows, d)), _resident(((CONV_W - 1) * n_seq, d)),
                   _resident((rows, d))],
        out_shape=[jax.ShapeDtypeStruct((rows, d), f32),
                   jax.ShapeDtypeStruct(((CONV_W - 1) * n_seq, d), f32),
                   jax.ShapeDtypeStruct((rows, d), f32)],
        compiler_params=pltpu.CompilerParams(
            dimension_semantics=("arbitrary",),
            vmem_limit_bytes=V7X_VMEM_LIMIT_BYTES),
        name="sample_layer",
    )(*sample_in, *sample_w)

    return (y_p, from_tmaj(y_s, n_t), conv_p, from_tmaj(conv_s, CONV_W - 1)[None],
            v_p, from_tmaj(v_s, n_t)[None])
```

```python
import functools
import math

import jax
import jax.numpy as jnp
from jax import lax
from jax.experimental import pallas as pl
from jax.experimental.pallas import tpu as pltpu

CHUNK = 128
N_HEADS = 8
HEAD_W = 128
CONV_W = 3
EPS = 1e-6
LN_EPS = 1e-5
COL_U, COL_V, COL_GA, COL_C, COL_B, COL_H, COL_GB, COL_MA, COL_MB = range(9)

V7X_VMEM_LIMIT_BYTES = 60000 * 1024
PROMPT_TILE = 512
CARRY_ROWS = 8


def _gelu(x):
    c = math.sqrt(2.0 / math.pi)
    return x * (0.5 * (1.0 + jnp.tanh(c * (x + 0.044715 * (x * x * x)))))


def _sigmoid(x):
    return 0.5 * jnp.tanh(0.5 * x) + 0.5


def _silu(x):
    return x * _sigmoid(x)


def _rms_norm(x, g):
    return x * lax.rsqrt(jnp.mean(x * x, axis=-1, keepdims=True) + EPS) * g


def _layer_norm(x, g, b):
    mu = jnp.mean(x, axis=-1, keepdims=True)
    xc = x - mu
    var = jnp.mean(xc * xc, axis=-1, keepdims=True)
    return xc * lax.rsqrt(var + LN_EPS) * g + b


def _dot(a, b):
    return jnp.dot(a, b, preferred_element_type=jnp.float32)


def _layer_tile(x, p, w, mix_mask, conv_taps):
    rows, d = x.shape
    bf = jnp.bfloat16
    h = _rms_norm(x, w["norm_g"][...]).astype(bf)

    def proj(col):
        return _dot(h, w["w_in"][:, col * d:(col + 1) * d])

    v = _layer_norm(_gelu(proj(COL_V)), w["ln_g"][...], w["ln_b"][...])
    v_b = v.astype(bf)
    mix = [jnp.where(mix_mask, w["mix"][hd], 0.0).astype(bf) for hd in range(N_HEADS)]
    s_rows = []
    for c in range(rows // CHUNK):
        s_heads = [_dot(mix[hd], v_b[c * CHUNK:(c + 1) * CHUNK,
                                     hd * HEAD_W:(hd + 1) * HEAD_W])
                   for hd in range(N_HEADS)]
        s_rows.append(jnp.concatenate(s_heads, axis=1) + w["mix_b"][...])
    s = jnp.concatenate(s_rows, axis=0)
    y_a = _gelu(proj(COL_U)) * s * _silu(proj(COL_GA))

    u = proj(COL_C) * proj(COL_H)
    u1, u2 = conv_taps(u)
    cw = w["conv_w"]
    conv = cw[0:1, :] * u2 + cw[1:2, :] * u1 + cw[2:3, :] * u
    y_b = proj(COL_B) * conv * _silu(proj(COL_GB))

    merged = (_sigmoid(proj(COL_MA)) * _dot(y_a.astype(bf), w["w_a"][...])
              + _sigmoid(proj(COL_MB)) * _dot(y_b.astype(bf), w["w_b"][...]))
    x = x + _dot(merged.astype(bf), w["w_o"][...])
    gate = _sigmoid(_dot(_rms_norm(x, w["pe_g"][...]).astype(bf), w["w_pg"][...]))
    x = x + gate * _dot(p.astype(bf), w["w_pp"][...])
    return _rms_norm(x, w["fin_g"][...]), v, u


WEIGHT_NAMES = ("norm_g", "w_in", "ln_g", "ln_b", "mix", "mix_b", "conv_w", "w_a",
                "w_b", "w_o", "pe_g", "w_pg", "w_pp", "fin_g")
N_W = len(WEIGHT_NAMES)


def _chunk_iota():
    return (lax.broadcasted_iota(jnp.int32, (CHUNK, CHUNK), 0),
            lax.broadcasted_iota(jnp.int32, (CHUNK, CHUNK), 1))


def _prompt_kernel(x_ref, p_ref, *rest):
    w = dict(zip(WEIGHT_NAMES, rest[:N_W]))
    y_ref, conv_ref, vrow_ref, pbuf_ref = rest[N_W:]
    tm, d = x_ref.shape[1], x_ref.shape[2]

    @pl.when(pl.program_id(1) == 0)
    def _():
        pbuf_ref[0:CARRY_ROWS, :] = jnp.zeros((CARRY_ROWS, d), jnp.float32)

    def conv_taps(u):
        pbuf_ref[CARRY_ROWS:CARRY_ROWS + tm, :] = u
        return (pbuf_ref[CARRY_ROWS - 1:CARRY_ROWS - 1 + tm, :],
                pbuf_ref[CARRY_ROWS - 2:CARRY_ROWS - 2 + tm, :])

    row, col = _chunk_iota()
    y, v, _ = _layer_tile(x_ref[0], p_ref[0], w, col <= row, conv_taps)
    y_ref[0] = y
    vrow_ref[0, 0] = v[tm - CHUNK:, :]
    tail = pbuf_ref[tm:tm + CARRY_ROWS, :]
    pbuf_ref[0:CARRY_ROWS, :] = tail
    conv_ref[0, 0] = tail[CARRY_ROWS - (CONV_W - 1):, :]


def _sample_kernel(x_ref, p_ref, e_ref, *rest, n_t):
    w = dict(zip(WEIGHT_NAMES, rest[:N_W]))
    y_ref, u_ref, vrow_ref, pbuf_ref = rest[N_W:]
    rows, d = x_ref.shape
    shift = n_t.bit_length() - 1

    pbuf_ref[0:CARRY_ROWS, :] = jnp.zeros((CARRY_ROWS, d), jnp.float32)

    def conv_taps(u):
        pbuf_ref[CARRY_ROWS:CARRY_ROWS + rows, :] = u
        t = lax.broadcasted_iota(jnp.int32, (rows, d), 0) & (n_t - 1)
        u1 = jnp.where(t >= 1, pbuf_ref[CARRY_ROWS - 1:CARRY_ROWS - 1 + rows, :],
                       e_ref[1:1 + rows, :])
        u2 = jnp.where(t >= 2, pbuf_ref[CARRY_ROWS - 2:CARRY_ROWS - 2 + rows, :],
                       e_ref[0:rows, :])
        return u1, u2

    row, col = _chunk_iota()
    mix_mask = (col <= row) & ((row >> shift) == (col >> shift))
    y, v, u = _layer_tile(x_ref[...], p_ref[...], w, mix_mask, conv_taps)
    y_ref[...] = y
    vrow_ref[...] = v
    u_ref[...] = u


def _resident(shape):
    return pl.BlockSpec(shape, lambda *_: (0,) * len(shape),
                        pipeline_mode=pl.Buffered(1))


def kernel(x_prompt, x_sample, state_conv, p_prompt, p_sample, norm_g, w_in, ln_v_g,
           ln_v_b, w_s, b_s, conv_w, w_a_out, w_b_out, w_o, pe_norm_g, w_pe_gate,
           w_pe_proj, final_norm_g):
    assert w_in.shape[0] == 1, "single-layer step only"
    batch, seq, d = x_prompt.shape
    n_seq, n_t, _ = x_sample.shape
    p_dim = p_prompt.shape[-1]
    tm = PROMPT_TILE
    rows = n_seq * n_t
    assert seq % tm == 0 and tm % CHUNK == 0 and d == N_HEADS * HEAD_W
    assert n_t & (n_t - 1) == 0 and CONV_W - 1 <= n_t <= CHUNK and rows % CHUNK == 0
    f32, bf = jnp.float32, jnp.bfloat16

    row2 = lambda a: a.reshape(1, -1)
    bs_full = jnp.repeat(b_s[0].T, HEAD_W, axis=1)
    weights = dict(
        norm_g=row2(norm_g[0]), w_in=w_in[0].astype(bf), ln_g=row2(ln_v_g[0]),
        ln_b=row2(ln_v_b[0]), mix=w_s[0], mix_b=bs_full, conv_w=conv_w[0],
        w_a=w_a_out[0].astype(bf), w_b=w_b_out[0].astype(bf), w_o=w_o[0].astype(bf),
        pe_g=row2(pe_norm_g[0]), w_pg=w_pe_gate[0].astype(bf),
        w_pp=w_pe_proj[0].astype(bf), fin_g=row2(final_norm_g))

    def specs(arrs):
        return [_resident(a.shape) for a in arrs]

    prompt_w = tuple(weights[k] for k in WEIGHT_NAMES)
    y_p, conv_p, v_p = pl.pallas_call(
        _prompt_kernel,
        grid=(batch, seq // tm),
        in_specs=[pl.BlockSpec((1, tm, d), lambda b, i: (b, i, 0)),
                  pl.BlockSpec((1, tm, p_dim), lambda b, i: (b, i, 0))] + specs(prompt_w),
        out_specs=[pl.BlockSpec((1, tm, d), lambda b, i: (b, i, 0)),
                   pl.BlockSpec((1, 1, CONV_W - 1, d), lambda b, i: (0, b, 0, 0)),
                   pl.BlockSpec((1, 1, CHUNK, d), lambda b, i: (0, b, 0, 0))],
        out_shape=[jax.ShapeDtypeStruct((batch, seq, d), f32),
                   jax.ShapeDtypeStruct((1, batch, CONV_W - 1, d), f32),
                   jax.ShapeDtypeStruct((1, batch, CHUNK, d), f32)],
        scratch_shapes=[pltpu.VMEM((tm + CARRY_ROWS, d), f32)],
        compiler_params=pltpu.CompilerParams(
            dimension_semantics=("arbitrary", "arbitrary"),
            vmem_limit_bytes=V7X_VMEM_LIMIT_BYTES),
        name="prompt_layer",
    )(x_prompt, p_prompt[0], *prompt_w)

    reps = CHUNK // n_t
    sample_weights = dict(weights,
                          mix=jnp.tile(w_s[0][:, :n_t, :n_t], (1, reps, reps)),
                          mix_b=jnp.tile(bs_full[:n_t], (reps, 1)))
    sample_w = tuple(sample_weights[k] for k in WEIGHT_NAMES)
    e_rows = jnp.pad(state_conv[0], ((0, 0), (0, n_t - (CONV_W - 1)), (0, 0)))
    e_rows = jnp.pad(e_rows.reshape(rows, d), ((0, CARRY_ROWS), (0, 0)))
    sample_in = (x_sample.reshape(rows, d), p_sample[0].reshape(rows, p_dim), e_rows)
    y_s, u_s, v_s = pl.pallas_call(
        functools.partial(_sample_kernel, n_t=n_t),
        grid=(1,),
        in_specs=specs(sample_in + sample_w),
        out_specs=[_resident((rows, d))] * 3,
        out_shape=[jax.ShapeDtypeStruct((rows, d), f32)] * 3,
        scratch_shapes=[pltpu.VMEM((rows + CARRY_ROWS, d), f32)],
        compiler_params=pltpu.CompilerParams(
            dimension_semantics=("arbitrary",),
            vmem_limit_bytes=V7X_VMEM_LIMIT_BYTES),
        name="sample_layer",
    )(*sample_in, *sample_w)

    conv_s = u_s.reshape(n_seq, n_t, d)[:, n_t - (CONV_W - 1):, :]
    return (y_p, y_s.reshape(n_seq, n_t, d), conv_p, conv_s[None],
            v_p, v_s.reshape(1, n_seq, n_t, d))
```

```python
import functools
import math

import jax
import jax.numpy as jnp
from jax import lax
from jax.experimental import pallas as pl
from jax.experimental.pallas import tpu as pltpu

CHUNK = 128
N_HEADS = 8
HEAD_W = 128
CONV_W = 3
EPS = 1e-6
LN_EPS = 1e-5
COL_U, COL_V, COL_GA, COL_C, COL_B, COL_H, COL_GB, COL_MA, COL_MB = range(9)

V7X_VMEM_LIMIT_BYTES = 60000 * 1024
PROMPT_TILE = 512
CARRY_ROWS = 8


def _gelu(x):
    c = math.sqrt(2.0 / math.pi)
    return x * (0.5 * (1.0 + jnp.tanh(c * (x + 0.044715 * (x * x * x)))))


def _sigmoid(x):
    return 0.5 * jnp.tanh(0.5 * x) + 0.5


def _silu(x):
    return x * _sigmoid(x)


def _rms_norm(x, g):
    return x * lax.rsqrt(jnp.mean(x * x, axis=-1, keepdims=True) + EPS) * g


def _layer_norm(x, g, b):
    mu = jnp.mean(x, axis=-1, keepdims=True)
    xc = x - mu
    var = jnp.mean(xc * xc, axis=-1, keepdims=True)
    return xc * lax.rsqrt(var + LN_EPS) * g + b


def _dot(a, b):
    return jnp.dot(a, b, preferred_element_type=jnp.float32)


def _layer_tile(x, p, w, spatial_mix, conv_taps):
    d = x.shape[-1]
    bf = jnp.bfloat16
    h = _rms_norm(x, w["norm_g"][...]).astype(bf)

    def proj(col):
        return _dot(h, w["w_in"][:, col * d:(col + 1) * d])

    v = _layer_norm(_gelu(proj(COL_V)), w["ln_g"][...], w["ln_b"][...])
    y_a = _gelu(proj(COL_U)) * spatial_mix(v) * _silu(proj(COL_GA))

    u = proj(COL_C) * proj(COL_H)
    u1, u2 = conv_taps(u)
    cw = w["conv_w"]
    conv = cw[0:1, :] * u2 + cw[1:2, :] * u1 + cw[2:3, :] * u
    y_b = proj(COL_B) * conv * _silu(proj(COL_GB))

    merged = (_sigmoid(proj(COL_MA)) * _dot(y_a.astype(bf), w["w_a"][...])
              + _sigmoid(proj(COL_MB)) * _dot(y_b.astype(bf), w["w_b"][...]))
    x = x + _dot(merged.astype(bf), w["w_o"][...])
    gate = _sigmoid(_dot(_rms_norm(x, w["pe_g"][...]).astype(bf), w["w_pg"][...]))
    x = x + gate * _dot(p.astype(bf), w["w_pp"][...])
    return _rms_norm(x, w["fin_g"][...]), v, u


WEIGHT_NAMES = ("norm_g", "w_in", "ln_g", "ln_b", "mix", "mix_b", "conv_w", "w_a",
                "w_b", "w_o", "pe_g", "w_pg", "w_pp", "fin_g")
N_W = len(WEIGHT_NAMES)


def _prompt_kernel(x_ref, p_ref, *rest):
    w = dict(zip(WEIGHT_NAMES, rest[:N_W]))
    y_ref, conv_ref, vrow_ref, pbuf_ref = rest[N_W:]
    tm, d = x_ref.shape[1], x_ref.shape[2]
    bf = jnp.bfloat16

    @pl.when(pl.program_id(1) == 0)
    def _():
        pbuf_ref[0:CARRY_ROWS, :] = jnp.zeros((CARRY_ROWS, d), jnp.float32)

    def spatial_mix(v):
        v_b = v.astype(bf)
        row = lax.broadcasted_iota(jnp.int32, (CHUNK, CHUNK), 0)
        col = lax.broadcasted_iota(jnp.int32, (CHUNK, CHUNK), 1)
        mix = [jnp.where(col <= row, w["mix"][hd], 0.0).astype(bf)
               for hd in range(N_HEADS)]
        s_rows = []
        for c in range(tm // CHUNK):
            s_heads = [_dot(mix[hd], v_b[c * CHUNK:(c + 1) * CHUNK,
                                         hd * HEAD_W:(hd + 1) * HEAD_W])
                       for hd in range(N_HEADS)]
            s_rows.append(jnp.concatenate(s_heads, axis=1) + w["mix_b"][...])
        return jnp.concatenate(s_rows, axis=0)

    def conv_taps(u):
        pbuf_ref[CARRY_ROWS:CARRY_ROWS + tm, :] = u
        return (pbuf_ref[CARRY_ROWS - 1:CARRY_ROWS - 1 + tm, :],
                pbuf_ref[CARRY_ROWS - 2:CARRY_ROWS - 2 + tm, :])

    y, v, _ = _layer_tile(x_ref[0], p_ref[0], w, spatial_mix, conv_taps)
    y_ref[0] = y
    vrow_ref[0, 0] = v[tm - CHUNK:, :]
    tail = pbuf_ref[tm:tm + CARRY_ROWS, :]
    pbuf_ref[0:CARRY_ROWS, :] = tail
    conv_ref[0, 0] = tail[CARRY_ROWS - (CONV_W - 1):, :]


def _sample_kernel(x_ref, p_ref, st_ref, *rest):
    w = dict(zip(WEIGHT_NAMES, rest[:N_W]))
    y_ref, conv_ref, vrow_ref = rest[N_W:]
    n_seq, n_t, _ = x_ref.shape

    def slab(a, t):
        return a[t * n_seq:(t + 1) * n_seq, :]

    def gather(ref):
        return jnp.concatenate([ref[:, t, :] for t in range(ref.shape[1])], axis=0)

    def scatter(ref, a):
        for t in range(ref.shape[1]):
            ref[:, t, :] = slab(a, t)

    def spatial_mix(v):
        out = []
        for t in range(n_t):
            acc = w["mix_b"][t:t + 1, :] + w["mix"][t * n_t:t * n_t + 1, :] * slab(v, 0)
            for j in range(1, t + 1):
                acc = acc + w["mix"][t * n_t + j:t * n_t + j + 1, :] * slab(v, j)
            out.append(acc)
        return jnp.concatenate(out, axis=0)

    def conv_taps(u):
        full = [st_ref[:, j, :] for j in range(CONV_W - 1)] + [
            slab(u, t) for t in range(n_t)]
        return (jnp.concatenate(full[1:1 + n_t], axis=0),
                jnp.concatenate(full[0:n_t], axis=0))

    y, v, u = _layer_tile(gather(x_ref), gather(p_ref), w, spatial_mix, conv_taps)
    scatter(y_ref, y)
    scatter(vrow_ref, v)
    scatter(conv_ref, u[(n_t - (CONV_W - 1)) * n_seq:, :])


def _resident(shape):
    return pl.BlockSpec(shape, lambda *_: (0,) * len(shape),
                        pipeline_mode=pl.Buffered(1))


def kernel(x_prompt, x_sample, state_conv, p_prompt, p_sample, norm_g, w_in, ln_v_g,
           ln_v_b, w_s, b_s, conv_w, w_a_out, w_b_out, w_o, pe_norm_g, w_pe_gate,
           w_pe_proj, final_norm_g):
    assert w_in.shape[0] == 1, "single-layer step only"
    batch, seq, d = x_prompt.shape
    n_seq, n_t, _ = x_sample.shape
    p_dim = p_prompt.shape[-1]
    tm = PROMPT_TILE
    assert seq % tm == 0 and tm % CHUNK == 0 and d == N_HEADS * HEAD_W
    assert CONV_W - 1 <= n_t <= CHUNK
    f32, bf = jnp.float32, jnp.bfloat16

    row2 = lambda a: a.reshape(1, -1)
    bs_full = jnp.repeat(b_s[0].T, HEAD_W, axis=1)
    weights = dict(
        norm_g=row2(norm_g[0]), w_in=w_in[0].astype(bf), ln_g=row2(ln_v_g[0]),
        ln_b=row2(ln_v_b[0]), mix=w_s[0], mix_b=bs_full, conv_w=conv_w[0],
        w_a=w_a_out[0].astype(bf), w_b=w_b_out[0].astype(bf), w_o=w_o[0].astype(bf),
        pe_g=row2(pe_norm_g[0]), w_pg=w_pe_gate[0].astype(bf),
        w_pp=w_pe_proj[0].astype(bf), fin_g=row2(final_norm_g))

    def specs(arrs):
        return [_resident(a.shape) for a in arrs]

    prompt_w = tuple(weights[k] for k in WEIGHT_NAMES)
    y_p, conv_p, v_p = pl.pallas_call(
        _prompt_kernel,
        grid=(batch, seq // tm),
        in_specs=[pl.BlockSpec((1, tm, d), lambda b, i: (b, i, 0)),
                  pl.BlockSpec((1, tm, p_dim), lambda b, i: (b, i, 0))] + specs(prompt_w),
        out_specs=[pl.BlockSpec((1, tm, d), lambda b, i: (b, i, 0)),
                   pl.BlockSpec((1, 1, CONV_W - 1, d), lambda b, i: (0, b, 0, 0)),
                   pl.BlockSpec((1, 1, CHUNK, d), lambda b, i: (0, b, 0, 0))],
        out_shape=[jax.ShapeDtypeStruct((batch, seq, d), f32),
                   jax.ShapeDtypeStruct((1, batch, CONV_W - 1, d), f32),
                   jax.ShapeDtypeStruct((1, batch, CHUNK, d), f32)],
        scratch_shapes=[pltpu.VMEM((tm + CARRY_ROWS, d), f32)],
        compiler_params=pltpu.CompilerParams(
            dimension_semantics=("arbitrary", "arbitrary"),
            vmem_limit_bytes=V7X_VMEM_LIMIT_BYTES),
        name="prompt_layer",
    )(x_prompt, p_prompt[0], *prompt_w)

    wmix = jnp.repeat(jnp.transpose(w_s[0][:, :n_t, :n_t], (1, 2, 0)), HEAD_W,
                      axis=2).reshape(n_t * n_t, d)
    sample_weights = dict(weights, mix=wmix, mix_b=bs_full[:n_t])
    sample_w = tuple(sample_weights[k] for k in WEIGHT_NAMES)
    sample_in = (x_sample, p_sample[0], state_conv[0])
    out_shapes = ((n_seq, n_t, d), (n_seq, CONV_W - 1, d), (n_seq, n_t, d))
    y_s, conv_s, v_s = pl.pallas_call(
        _sample_kernel,
        grid=(1,),
        in_specs=specs(sample_in + sample_w),
        out_specs=[_resident(s) for s in out_shapes],
        out_shape=[jax.ShapeDtypeStruct(s, f32) for s in out_shapes],
        compiler_params=pltpu.CompilerParams(
            dimension_semantics=("arbitrary",),
            vmem_limit_bytes=V7X_VMEM_LIMIT_BYTES),
        name="sample_layer",
    )(*sample_in, *sample_w)

    return (y_p, y_s, conv_p, conv_s[None], v_p, v_s[None])
```

```python
import functools
import math

import jax
import jax.numpy as jnp
from jax import lax
from jax.experimental import pallas as pl
from jax.experimental.pallas import tpu as pltpu

CHUNK = 128
N_HEADS = 8
HEAD_W = 128
CONV_W = 3
EPS = 1e-6
LN_EPS = 1e-5
COL_U, COL_V, COL_GA, COL_C, COL_B, COL_H, COL_GB, COL_MA, COL_MB = range(9)

V7X_VMEM_LIMIT_BYTES = 60000 * 1024
PROMPT_TILE = 512
CARRY_ROWS = 8
STAGE_ROWS, STAGE_COLS = 512, 1024


def _gelu(x):
    c = math.sqrt(2.0 / math.pi)
    return x * (0.5 * (1.0 + jnp.tanh(c * (x + 0.044715 * (x * x * x)))))


def _sigmoid(x):
    return 0.5 * jnp.tanh(0.5 * x) + 0.5


def _silu(x):
    return x * _sigmoid(x)


def _rms_norm(x, g):
    return x * lax.rsqrt(jnp.mean(x * x, axis=-1, keepdims=True) + EPS) * g


def _layer_norm(x, g, b):
    mu = jnp.mean(x, axis=-1, keepdims=True)
    xc = x - mu
    var = jnp.mean(xc * xc, axis=-1, keepdims=True)
    return xc * lax.rsqrt(var + LN_EPS) * g + b


def _dot(a, b):
    return jnp.dot(a, b, preferred_element_type=jnp.float32)


def _layer_tile(x, p, w, spatial_mix, conv_taps):
    d = x.shape[-1]
    bf = jnp.bfloat16
    h = _rms_norm(x, w["norm_g"][...]).astype(bf)

    def proj(col):
        return _dot(h, w["w_in"][:, col * d:(col + 1) * d])

    v = _layer_norm(_gelu(proj(COL_V)), w["ln_g"][...], w["ln_b"][...])
    y_a = _gelu(proj(COL_U)) * spatial_mix(v) * _silu(proj(COL_GA))

    u = proj(COL_C) * proj(COL_H)
    u1, u2 = conv_taps(u)
    cw = w["conv_w"]
    conv = cw[0:1, :] * u2 + cw[1:2, :] * u1 + cw[2:3, :] * u
    y_b = proj(COL_B) * conv * _silu(proj(COL_GB))

    merged = (_sigmoid(proj(COL_MA)) * _dot(y_a.astype(bf), w["w_a"][...])
              + _sigmoid(proj(COL_MB)) * _dot(y_b.astype(bf), w["w_b"][...]))
    x = x + _dot(merged.astype(bf), w["w_o"][...])
    gate = _sigmoid(_dot(_rms_norm(x, w["pe_g"][...]).astype(bf), w["w_pg"][...]))
    x = x + gate * _dot(p.astype(bf), w["w_pp"][...])
    return _rms_norm(x, w["fin_g"][...]), v, u


SMALL_NAMES = ("norm_g", "ln_g", "ln_b", "mix", "mix_b", "conv_w", "pe_g", "fin_g")
MATMUL_NAMES = ("w_in", "w_a", "w_b", "w_o", "w_pg", "w_pp")
N_SMALL, N_MM = len(SMALL_NAMES), len(MATMUL_NAMES)


def _weight_chunks(shapes):
    chunks = []
    for wi, (rows, cols) in enumerate(shapes):
        rb = min(rows, STAGE_ROWS)
        assert rows % rb == 0 and cols % STAGE_COLS == 0
        for c0 in range(0, cols, STAGE_COLS):
            for r0 in range(0, rows, rb):
                chunks.append((wi, r0, rb, c0))
    return chunks


def _export_copy(wi, vmem_refs, hbm_refs, sem):
    return pltpu.make_async_copy(vmem_refs[wi], hbm_refs[wi], sem.at[wi])


def _load_weights_as_bf16(f32_hbm, bf16_vmem, bf16_hbm, stage_ref, in_sem, out_sem):
    chunks = _weight_chunks([r.shape for r in f32_hbm])

    def fetch(k):
        wi, r0, rb, c0 = chunks[k]
        return pltpu.make_async_copy(
            f32_hbm[wi].at[pl.ds(r0, rb), pl.ds(c0, STAGE_COLS)],
            stage_ref.at[k % 2, pl.ds(0, rb), :], in_sem.at[k % 2])

    fetch(0).start()
    for k, (wi, r0, rb, c0) in enumerate(chunks):
        if k + 1 < len(chunks):
            fetch(k + 1).start()
        fetch(k).wait()
        bf16_vmem[wi][pl.ds(r0, rb), pl.ds(c0, STAGE_COLS)] = (
            stage_ref[k % 2, 0:rb, :].astype(jnp.bfloat16))
        if k + 1 == len(chunks) or chunks[k + 1][0] != wi:
            _export_copy(wi, bf16_vmem, bf16_hbm, out_sem).start()


def _prompt_kernel(x_ref, p_ref, *rest):
    small = rest[:N_SMALL]
    mm_f32_hbm = rest[N_SMALL:N_SMALL + N_MM]
    outs = rest[N_SMALL + N_MM:]
    y_ref, conv_ref, vrow_ref = outs[:3]
    mm_bf16_hbm = outs[3:3 + N_MM]
    scratch = outs[3 + N_MM:]
    pbuf_ref = scratch[0]
    mm_bf16 = scratch[1:1 + N_MM]
    stage_ref, in_sem, out_sem = scratch[1 + N_MM:]
    w = dict(zip(SMALL_NAMES + MATMUL_NAMES, small + mm_bf16))
    tm, d = x_ref.shape[1], x_ref.shape[2]
    bf = jnp.bfloat16
    step = pl.program_id(0) * pl.num_programs(1) + pl.program_id(1)

    @pl.when(step == 0)
    def _():
        _load_weights_as_bf16(mm_f32_hbm, mm_bf16, mm_bf16_hbm, stage_ref, in_sem, out_sem)

    @pl.when(step == 1)
    def _():
        for wi in range(N_MM):
            _export_copy(wi, mm_bf16, mm_bf16_hbm, out_sem).wait()

    @pl.when(pl.program_id(1) == 0)
    def _():
        pbuf_ref[0:CARRY_ROWS, :] = jnp.zeros((CARRY_ROWS, d), jnp.float32)

    def spatial_mix(v):
        v_b = v.astype(bf)
        row = lax.broadcasted_iota(jnp.int32, (CHUNK, CHUNK), 0)
        col = lax.broadcasted_iota(jnp.int32, (CHUNK, CHUNK), 1)
        mix = [jnp.where(col <= row, w["mix"][hd], 0.0).astype(bf)
               for hd in range(N_HEADS)]
        s_rows = []
        for c in range(tm // CHUNK):
            s_heads = [_dot(mix[hd], v_b[c * CHUNK:(c + 1) * CHUNK,
                                         hd * HEAD_W:(hd + 1) * HEAD_W])
                       for hd in range(N_HEADS)]
            s_rows.append(jnp.concatenate(s_heads, axis=1) + w["mix_b"][...])
        return jnp.concatenate(s_rows, axis=0)

    def conv_taps(u):
        pbuf_ref[CARRY_ROWS:CARRY_ROWS + tm, :] = u
        return (pbuf_ref[CARRY_ROWS - 1:CARRY_ROWS - 1 + tm, :],
                pbuf_ref[CARRY_ROWS - 2:CARRY_ROWS - 2 + tm, :])

    y, v, _ = _layer_tile(x_ref[0], p_ref[0], w, spatial_mix, conv_taps)
    y_ref[0] = y
    vrow_ref[0, 0] = v[tm - CHUNK:, :]
    tail = pbuf_ref[tm:tm + CARRY_ROWS, :]
    pbuf_ref[0:CARRY_ROWS, :] = tail
    conv_ref[0, 0] = tail[CARRY_ROWS - (CONV_W - 1):, :]


def _sample_kernel(x_ref, p_ref, st_ref, *rest):
    w = dict(zip(SMALL_NAMES + MATMUL_NAMES, rest[:N_SMALL + N_MM]))
    y_ref, conv_ref, vrow_ref = rest[N_SMALL + N_MM:]
    n_seq, n_t, _ = x_ref.shape

    def slab(a, t):
        return a[t * n_seq:(t + 1) * n_seq, :]

    def gather(ref):
        return jnp.concatenate([ref[:, t, :] for t in range(ref.shape[1])], axis=0)

    def scatter(ref, a):
        for t in range(ref.shape[1]):
            ref[:, t, :] = slab(a, t)

    def spatial_mix(v):
        out = []
        for t in range(n_t):
            acc = w["mix_b"][t:t + 1, :] + w["mix"][t * n_t:t * n_t + 1, :] * slab(v, 0)
            for j in range(1, t + 1):
                acc = acc + w["mix"][t * n_t + j:t * n_t + j + 1, :] * slab(v, j)
            out.append(acc)
        return jnp.concatenate(out, axis=0)

    def conv_taps(u):
        full = [st_ref[:, j, :] for j in range(CONV_W - 1)] + [
            slab(u, t) for t in range(n_t)]
        return (jnp.concatenate(full[1:1 + n_t], axis=0),
                jnp.concatenate(full[0:n_t], axis=0))

    y, v, u = _layer_tile(gather(x_ref), gather(p_ref), w, spatial_mix, conv_taps)
    scatter(y_ref, y)
    scatter(vrow_ref, v)
    scatter(conv_ref, u[(n_t - (CONV_W - 1)) * n_seq:, :])


def _resident(shape):
    return pl.BlockSpec(shape, lambda *_: (0,) * len(shape),
                        pipeline_mode=pl.Buffered(1))


def kernel(x_prompt, x_sample, state_conv, p_prompt, p_sample, norm_g, w_in, ln_v_g,
           ln_v_b, w_s, b_s, conv_w, w_a_out, w_b_out, w_o, pe_norm_g, w_pe_gate,
           w_pe_proj, final_norm_g):
    assert w_in.shape[0] == 1, "single-layer step only"
    batch, seq, d = x_prompt.shape
    n_seq, n_t, _ = x_sample.shape
    p_dim = p_prompt.shape[-1]
    tm = PROMPT_TILE
    assert seq % tm == 0 and tm % CHUNK == 0 and d == N_HEADS * HEAD_W
    assert batch * (seq // tm) >= 2, "weight copies are drained in the second step"
    assert CONV_W - 1 <= n_t <= CHUNK
    f32, bf = jnp.float32, jnp.bfloat16

    row2 = lambda a: a.reshape(1, -1)
    bs_full = jnp.repeat(b_s[0].T, HEAD_W, axis=1)
    small = dict(norm_g=row2(norm_g[0]), ln_g=row2(ln_v_g[0]), ln_b=row2(ln_v_b[0]),
                 mix=w_s[0], mix_b=bs_full, conv_w=conv_w[0], pe_g=row2(pe_norm_g[0]),
                 fin_g=row2(final_norm_g))
    mm_f32 = dict(w_in=w_in[0], w_a=w_a_out[0], w_b=w_b_out[0], w_o=w_o[0],
                  w_pg=w_pe_gate[0], w_pp=w_pe_proj[0])

    def specs(arrs):
        return [_resident(a.shape) for a in arrs]

    prompt_small = tuple(small[k] for k in SMALL_NAMES)
    mm_in = tuple(mm_f32[k] for k in MATMUL_NAMES)
    hbm = pl.BlockSpec(memory_space=pltpu.HBM)
    outs = pl.pallas_call(
        _prompt_kernel,
        grid=(batch, seq // tm),
        in_specs=[pl.BlockSpec((1, tm, d), lambda b, i: (b, i, 0)),
                  pl.BlockSpec((1, tm, p_dim), lambda b, i: (b, i, 0))]
        + specs(prompt_small) + [hbm] * N_MM,
        out_specs=[pl.BlockSpec((1, tm, d), lambda b, i: (b, i, 0)),
                   pl.BlockSpec((1, 1, CONV_W - 1, d), lambda b, i: (0, b, 0, 0)),
                   pl.BlockSpec((1, 1, CHUNK, d), lambda b, i: (0, b, 0, 0))]
        + [hbm] * N_MM,
        out_shape=[jax.ShapeDtypeStruct((batch, seq, d), f32),
                   jax.ShapeDtypeStruct((1, batch, CONV_W - 1, d), f32),
                   jax.ShapeDtypeStruct((1, batch, CHUNK, d), f32)]
        + [jax.ShapeDtypeStruct(a.shape, bf) for a in mm_in],
        scratch_shapes=[pltpu.VMEM((tm + CARRY_ROWS, d), f32)]
        + [pltpu.VMEM(a.shape, bf) for a in mm_in]
        + [pltpu.VMEM((2, STAGE_ROWS, STAGE_COLS), f32),
           pltpu.SemaphoreType.DMA((2,)), pltpu.SemaphoreType.DMA((N_MM,))],
        compiler_params=pltpu.CompilerParams(
            dimension_semantics=("arbitrary", "arbitrary"),
            vmem_limit_bytes=V7X_VMEM_LIMIT_BYTES),
        name="prompt_layer",
    )(x_prompt, p_prompt[0], *prompt_small, *mm_in)
    y_p, conv_p, v_p = outs[:3]
    mm_bf16 = tuple(outs[3:])

    wmix = jnp.repeat(jnp.transpose(w_s[0][:, :n_t, :n_t], (1, 2, 0)), HEAD_W,
                      axis=2).reshape(n_t * n_t, d)
    sample_small = tuple(dict(small, mix=wmix, mix_b=bs_full[:n_t])[k]
                         for k in SMALL_NAMES)
    sample_in = (x_sample, p_sample[0], state_conv[0]) + sample_small + mm_bf16
    out_shapes = ((n_seq, n_t, d), (n_seq, CONV_W - 1, d), (n_seq, n_t, d))
    y_s, conv_s, v_s = pl.pallas_call(
        _sample_kernel,
        grid=(1,),
        in_specs=specs(sample_in),
        out_specs=[_resident(s) for s in out_shapes],
        out_shape=[jax.ShapeDtypeStruct(s, f32) for s in out_shapes],
        compiler_params=pltpu.CompilerParams(
            dimension_semantics=("arbitrary",),
            vmem_limit_bytes=V7X_VMEM_LIMIT_BYTES),
        name="sample_layer",
    )(*sample_in)

    return (y_p, y_s, conv_p, conv_s[None], v_p, v_s[None])
```

```python
import functools
import math

import jax
import jax.numpy as jnp
from jax import lax
from jax.experimental import pallas as pl
from jax.experimental.pallas import tpu as pltpu

CHUNK = 128
N_HEADS = 8
HEAD_W = 128
CONV_W = 3
EPS = 1e-6
LN_EPS = 1e-5
COL_U, COL_V, COL_GA, COL_C, COL_B, COL_H, COL_GB, COL_MA, COL_MB = range(9)

V7X_VMEM_LIMIT_BYTES = 60000 * 1024
PROMPT_TILE = 512
CARRY_ROWS = 8
STAGE_ROWS, STAGE_COLS = 256, 1024
STAGE_SLOTS = 4


def _gelu(x):
    c = math.sqrt(2.0 / math.pi)
    return x * (0.5 * (1.0 + jnp.tanh(c * (x + 0.044715 * (x * x * x)))))


def _sigmoid(x):
    return 0.5 * jnp.tanh(0.5 * x) + 0.5


def _silu(x):
    return x * _sigmoid(x)


def _rms_norm(x, g):
    return x * lax.rsqrt(jnp.mean(x * x, axis=-1, keepdims=True) + EPS) * g


def _layer_norm(x, g, b):
    mu = jnp.mean(x, axis=-1, keepdims=True)
    xc = x - mu
    var = jnp.mean(xc * xc, axis=-1, keepdims=True)
    return xc * lax.rsqrt(var + LN_EPS) * g + b


def _dot(a, b):
    return jnp.dot(a, b, preferred_element_type=jnp.float32)


def _layer_tile(x, p, w, spatial_mix, conv_taps):
    d = x.shape[-1]
    bf = jnp.bfloat16
    h = _rms_norm(x, w["norm_g"][...]).astype(bf)

    def proj(col):
        return _dot(h, w["w_in"][:, col * d:(col + 1) * d])

    v = _layer_norm(_gelu(proj(COL_V)), w["ln_g"][...], w["ln_b"][...])
    y_a = _gelu(proj(COL_U)) * spatial_mix(v) * _silu(proj(COL_GA))

    u = proj(COL_C) * proj(COL_H)
    u1, u2 = conv_taps(u)
    cw = w["conv_w"]
    conv = cw[0:1, :] * u2 + cw[1:2, :] * u1 + cw[2:3, :] * u
    y_b = proj(COL_B) * conv * _silu(proj(COL_GB))

    merged = (_sigmoid(proj(COL_MA)) * _dot(y_a.astype(bf), w["w_a"][...])
              + _sigmoid(proj(COL_MB)) * _dot(y_b.astype(bf), w["w_b"][...]))
    x = x + _dot(merged.astype(bf), w["w_o"][...])
    gate = _sigmoid(_dot(_rms_norm(x, w["pe_g"][...]).astype(bf), w["w_pg"][...]))
    x = x + gate * _dot(p.astype(bf), w["w_pp"][...])
    return _rms_norm(x, w["fin_g"][...]), v, u


SMALL_NAMES = ("norm_g", "ln_g", "ln_b", "mix", "mix_b", "conv_w", "pe_g", "fin_g")
MATMUL_NAMES = ("w_in", "w_a", "w_b", "w_o", "w_pg", "w_pp")
N_SMALL, N_MM = len(SMALL_NAMES), len(MATMUL_NAMES)


def _weight_chunks(shapes):
    chunks = []
    for wi, (rows, cols) in enumerate(shapes):
        rb = min(rows, STAGE_ROWS)
        assert rows % rb == 0 and cols % STAGE_COLS == 0
        for c0 in range(0, cols, STAGE_COLS):
            for r0 in range(0, rows, rb):
                chunks.append((wi, r0, rb, c0))
    return chunks


def _export_copy(wi, vmem_refs, hbm_refs, sem):
    return pltpu.make_async_copy(vmem_refs[wi], hbm_refs[wi], sem.at[wi])


def _load_weights_as_bf16(f32_hbm, bf16_vmem, bf16_hbm, stage_ref, in_sem, out_sem):
    chunks = _weight_chunks([r.shape for r in f32_hbm])
    n_slots = stage_ref.shape[0]

    def fetch(k):
        wi, r0, rb, c0 = chunks[k]
        return pltpu.make_async_copy(
            f32_hbm[wi].at[pl.ds(r0, rb), pl.ds(c0, STAGE_COLS)],
            stage_ref.at[k % n_slots, pl.ds(0, rb), :], in_sem.at[k % n_slots])

    for k in range(min(n_slots, len(chunks))):
        fetch(k).start()
    for k, (wi, r0, rb, c0) in enumerate(chunks):
        fetch(k).wait()
        bf16_vmem[wi][pl.ds(r0, rb), pl.ds(c0, STAGE_COLS)] = (
            stage_ref[k % n_slots, 0:rb, :].astype(jnp.bfloat16))
        if k + n_slots < len(chunks):
            fetch(k + n_slots).start()
    for wi in range(len(f32_hbm)):
        _export_copy(wi, bf16_vmem, bf16_hbm, out_sem).start()


def _prompt_kernel(x_ref, p_ref, *rest):
    small = rest[:N_SMALL]
    mm_f32_hbm = rest[N_SMALL:N_SMALL + N_MM]
    outs = rest[N_SMALL + N_MM:]
    y_ref, conv_ref, vrow_ref = outs[:3]
    mm_bf16_hbm = outs[3:3 + N_MM]
    scratch = outs[3 + N_MM:]
    pbuf_ref = scratch[0]
    mm_bf16 = scratch[1:1 + N_MM]
    stage_ref, in_sem, out_sem = scratch[1 + N_MM:]
    w = dict(zip(SMALL_NAMES + MATMUL_NAMES, small + mm_bf16))
    tm, d = x_ref.shape[1], x_ref.shape[2]
    bf = jnp.bfloat16
    step = pl.program_id(0) * pl.num_programs(1) + pl.program_id(1)

    @pl.when(step == 0)
    def _():
        _load_weights_as_bf16(mm_f32_hbm, mm_bf16, mm_bf16_hbm, stage_ref, in_sem, out_sem)

    @pl.when(step == 1)
    def _():
        for wi in range(N_MM):
            _export_copy(wi, mm_bf16, mm_bf16_hbm, out_sem).wait()

    @pl.when(pl.program_id(1) == 0)
    def _():
        pbuf_ref[0:CARRY_ROWS, :] = jnp.zeros((CARRY_ROWS, d), jnp.float32)

    def spatial_mix(v):
        v_b = v.astype(bf)
        row = lax.broadcasted_iota(jnp.int32, (CHUNK, CHUNK), 0)
        col = lax.broadcasted_iota(jnp.int32, (CHUNK, CHUNK), 1)
        mix = [jnp.where(col <= row, w["mix"][hd], 0.0).astype(bf)
               for hd in range(N_HEADS)]
        s_rows = []
        for c in range(tm // CHUNK):
            s_heads = [_dot(mix[hd], v_b[c * CHUNK:(c + 1) * CHUNK,
                                         hd * HEAD_W:(hd + 1) * HEAD_W])
                       for hd in range(N_HEADS)]
            s_rows.append(jnp.concatenate(s_heads, axis=1) + w["mix_b"][...])
        return jnp.concatenate(s_rows, axis=0)

    def conv_taps(u):
        pbuf_ref[CARRY_ROWS:CARRY_ROWS + tm, :] = u
        return (pbuf_ref[CARRY_ROWS - 1:CARRY_ROWS - 1 + tm, :],
                pbuf_ref[CARRY_ROWS - 2:CARRY_ROWS - 2 + tm, :])

    y, v, _ = _layer_tile(x_ref[0], p_ref[0], w, spatial_mix, conv_taps)
    y_ref[0] = y
    vrow_ref[0, 0] = v[tm - CHUNK:, :]
    tail = pbuf_ref[tm:tm + CARRY_ROWS, :]
    pbuf_ref[0:CARRY_ROWS, :] = tail
    conv_ref[0, 0] = tail[CARRY_ROWS - (CONV_W - 1):, :]


def _sample_kernel(x_ref, p_ref, st_ref, *rest):
    w = dict(zip(SMALL_NAMES + MATMUL_NAMES, rest[:N_SMALL + N_MM]))
    y_ref, conv_ref, vrow_ref = rest[N_SMALL + N_MM:]
    n_seq, n_t, _ = x_ref.shape

    def slab(a, t):
        return a[t * n_seq:(t + 1) * n_seq, :]

    def gather(ref):
        return jnp.concatenate([ref[:, t, :] for t in range(ref.shape[1])], axis=0)

    def scatter(ref, a):
        for t in range(ref.shape[1]):
            ref[:, t, :] = slab(a, t)

    def spatial_mix(v):
        out = []
        for t in range(n_t):
            acc = w["mix_b"][t:t + 1, :] + w["mix"][t * n_t:t * n_t + 1, :] * slab(v, 0)
            for j in range(1, t + 1):
                acc = acc + w["mix"][t * n_t + j:t * n_t + j + 1, :] * slab(v, j)
            out.append(acc)
        return jnp.concatenate(out, axis=0)

    def conv_taps(u):
        full = [st_ref[:, j, :] for j in range(CONV_W - 1)] + [
            slab(u, t) for t in range(n_t)]
        return (jnp.concatenate(full[1:1 + n_t], axis=0),
                jnp.concatenate(full[0:n_t], axis=0))

    y, v, u = _layer_tile(gather(x_ref), gather(p_ref), w, spatial_mix, conv_taps)
    scatter(y_ref, y)
    scatter(vrow_ref, v)
    scatter(conv_ref, u[(n_t - (CONV_W - 1)) * n_seq:, :])


def _resident(shape):
    return pl.BlockSpec(shape, lambda *_: (0,) * len(shape),
                        pipeline_mode=pl.Buffered(1))


def kernel(x_prompt, x_sample, state_conv, p_prompt, p_sample, norm_g, w_in, ln_v_g,
           ln_v_b, w_s, b_s, conv_w, w_a_out, w_b_out, w_o, pe_norm_g, w_pe_gate,
           w_pe_proj, final_norm_g):
    assert w_in.shape[0] == 1, "single-layer step only"
    batch, seq, d = x_prompt.shape
    n_seq, n_t, _ = x_sample.shape
    p_dim = p_prompt.shape[-1]
    tm = PROMPT_TILE
    assert seq % tm == 0 and tm % CHUNK == 0 and d == N_HEADS * HEAD_W
    assert batch * (seq // tm) >= 2, "weight copies are drained in the second step"
    assert CONV_W - 1 <= n_t <= CHUNK
    f32, bf = jnp.float32, jnp.bfloat16

    row2 = lambda a: a.reshape(1, -1)
    bs_full = jnp.repeat(b_s[0].T, HEAD_W, axis=1)
    small = dict(norm_g=row2(norm_g[0]), ln_g=row2(ln_v_g[0]), ln_b=row2(ln_v_b[0]),
                 mix=w_s[0], mix_b=bs_full, conv_w=conv_w[0], pe_g=row2(pe_norm_g[0]),
                 fin_g=row2(final_norm_g))
    mm_f32 = dict(w_in=w_in[0], w_a=w_a_out[0], w_b=w_b_out[0], w_o=w_o[0],
                  w_pg=w_pe_gate[0], w_pp=w_pe_proj[0])

    def specs(arrs):
        return [_resident(a.shape) for a in arrs]

    prompt_small = tuple(small[k] for k in SMALL_NAMES)
    mm_in = tuple(mm_f32[k] for k in MATMUL_NAMES)
    hbm = pl.BlockSpec(memory_space=pltpu.HBM)
    outs = pl.pallas_call(
        _prompt_kernel,
        grid=(batch, seq // tm),
        in_specs=[pl.BlockSpec((1, tm, d), lambda b, i: (b, i, 0)),
                  pl.BlockSpec((1, tm, p_dim), lambda b, i: (b, i, 0))]
        + specs(prompt_small) + [hbm] * N_MM,
        out_specs=[pl.BlockSpec((1, tm, d), lambda b, i: (b, i, 0)),
                   pl.BlockSpec((1, 1, CONV_W - 1, d), lambda b, i: (0, b, 0, 0)),
                   pl.BlockSpec((1, 1, CHUNK, d), lambda b, i: (0, b, 0, 0))]
        + [hbm] * N_MM,
        out_shape=[jax.ShapeDtypeStruct((batch, seq, d), f32),
                   jax.ShapeDtypeStruct((1, batch, CONV_W - 1, d), f32),
                   jax.ShapeDtypeStruct((1, batch, CHUNK, d), f32)]
        + [jax.ShapeDtypeStruct(a.shape, bf) for a in mm_in],
        scratch_shapes=[pltpu.VMEM((tm + CARRY_ROWS, d), f32)]
        + [pltpu.VMEM(a.shape, bf) for a in mm_in]
        + [pltpu.VMEM((STAGE_SLOTS, STAGE_ROWS, STAGE_COLS), f32),
           pltpu.SemaphoreType.DMA((STAGE_SLOTS,)), pltpu.SemaphoreType.DMA((N_MM,))],
        compiler_params=pltpu.CompilerParams(
            dimension_semantics=("arbitrary", "arbitrary"),
            vmem_limit_bytes=V7X_VMEM_LIMIT_BYTES),
        name="prompt_layer",
    )(x_prompt, p_prompt[0], *prompt_small, *mm_in)
    y_p, conv_p, v_p = outs[:3]
    mm_bf16 = tuple(outs[3:])

    wmix = jnp.repeat(jnp.transpose(w_s[0][:, :n_t, :n_t], (1, 2, 0)), HEAD_W,
                      axis=2).reshape(n_t * n_t, d)
    sample_small = tuple(dict(small, mix=wmix, mix_b=bs_full[:n_t])[k]
                         for k in SMALL_NAMES)
    sample_in = (x_sample, p_sample[0], state_conv[0]) + sample_small + mm_bf16
    out_shapes = ((n_seq, n_t, d), (n_seq, CONV_W - 1, d), (n_seq, n_t, d))
    y_s, conv_s, v_s = pl.pallas_call(
        _sample_kernel,
        grid=(1,),
        in_specs=specs(sample_in),
        out_specs=[_resident(s) for s in out_shapes],
        out_shape=[jax.ShapeDtypeStruct(s, f32) for s in out_shapes],
        compiler_params=pltpu.CompilerParams(
            dimension_semantics=("arbitrary",),
            vmem_limit_bytes=V7X_VMEM_LIMIT_BYTES),
        name="sample_layer",
    )(*sample_in)

    return (y_p, y_s, conv_p, conv_s[None], v_p, v_s[None])
```

```python
import functools
import math

import jax
import jax.numpy as jnp
from jax import lax
from jax.experimental import pallas as pl
from jax.experimental.pallas import tpu as pltpu

CHUNK = 128
N_HEADS = 8
HEAD_W = 128
CONV_W = 3
EPS = 1e-6
LN_EPS = 1e-5
COL_U, COL_V, COL_GA, COL_C, COL_B, COL_H, COL_GB, COL_MA, COL_MB = range(9)

V7X_VMEM_LIMIT_BYTES = 65280 * 1024
PROMPT_TILE = 1024
CARRY_ROWS = 8
STAGE_ROWS, STAGE_COLS = 64, 1024
STAGE_SLOTS = 4


def _gelu(x):
    c = math.sqrt(2.0 / math.pi)
    return x * (0.5 * (1.0 + jnp.tanh(c * (x + 0.044715 * (x * x * x)))))


def _sigmoid(x):
    return 0.5 * jnp.tanh(0.5 * x) + 0.5


def _silu(x):
    return x * _sigmoid(x)


def _rms_norm(x, g):
    return x * lax.rsqrt(jnp.mean(x * x, axis=-1, keepdims=True) + EPS) * g


def _layer_norm(x, g, b):
    mu = jnp.mean(x, axis=-1, keepdims=True)
    xc = x - mu
    var = jnp.mean(xc * xc, axis=-1, keepdims=True)
    return xc * lax.rsqrt(var + LN_EPS) * g + b


def _dot(a, b):
    return jnp.dot(a, b, preferred_element_type=jnp.float32)


def _layer_tile(x, p, w, spatial_mix, conv_taps):
    d = x.shape[-1]
    bf = jnp.bfloat16
    h = _rms_norm(x, w["norm_g"][...]).astype(bf)

    def proj(col):
        return _dot(h, w["w_in"][:, col * d:(col + 1) * d])

    v = _layer_norm(_gelu(proj(COL_V)), w["ln_g"][...], w["ln_b"][...])
    y_a = _gelu(proj(COL_U)) * spatial_mix(v) * _silu(proj(COL_GA))

    u = proj(COL_C) * proj(COL_H)
    u1, u2 = conv_taps(u)
    cw = w["conv_w"]
    conv = cw[0:1, :] * u2 + cw[1:2, :] * u1 + cw[2:3, :] * u
    y_b = proj(COL_B) * conv * _silu(proj(COL_GB))

    merged = (_sigmoid(proj(COL_MA)) * _dot(y_a.astype(bf), w["w_a"][...])
              + _sigmoid(proj(COL_MB)) * _dot(y_b.astype(bf), w["w_b"][...]))
    x = x + _dot(merged.astype(bf), w["w_o"][...])
    gate = _sigmoid(_dot(_rms_norm(x, w["pe_g"][...]).astype(bf), w["w_pg"][...]))
    x = x + gate * _dot(p.astype(bf), w["w_pp"][...])
    return _rms_norm(x, w["fin_g"][...]), v, u


SMALL_NAMES = ("norm_g", "ln_g", "ln_b", "mix", "mix_b", "conv_w", "pe_g", "fin_g")
MATMUL_NAMES = ("w_in", "w_a", "w_b", "w_o", "w_pg", "w_pp")
N_SMALL, N_MM = len(SMALL_NAMES), len(MATMUL_NAMES)


def _weight_chunks(shapes):
    chunks = []
    for wi, (rows, cols) in enumerate(shapes):
        rb = min(rows, STAGE_ROWS)
        assert rows % rb == 0 and cols % STAGE_COLS == 0
        for c0 in range(0, cols, STAGE_COLS):
            for r0 in range(0, rows, rb):
                chunks.append((wi, r0, rb, c0))
    return chunks


def _export_copy(wi, vmem_refs, hbm_refs, sem):
    return pltpu.make_async_copy(vmem_refs[wi], hbm_refs[wi], sem.at[wi])


def _load_weights_as_bf16(f32_hbm, bf16_vmem, bf16_hbm, stage_ref, in_sem, out_sem):
    chunks = _weight_chunks([r.shape for r in f32_hbm])
    n_slots = stage_ref.shape[0]

    def fetch(k):
        wi, r0, rb, c0 = chunks[k]
        return pltpu.make_async_copy(
            f32_hbm[wi].at[pl.ds(r0, rb), pl.ds(c0, STAGE_COLS)],
            stage_ref.at[k % n_slots, pl.ds(0, rb), :], in_sem.at[k % n_slots])

    for k in range(min(n_slots, len(chunks))):
        fetch(k).start()
    for k, (wi, r0, rb, c0) in enumerate(chunks):
        fetch(k).wait()
        bf16_vmem[wi][pl.ds(r0, rb), pl.ds(c0, STAGE_COLS)] = (
            stage_ref[k % n_slots, 0:rb, :].astype(jnp.bfloat16))
        if k + n_slots < len(chunks):
            fetch(k + n_slots).start()
    for wi in range(len(f32_hbm)):
        _export_copy(wi, bf16_vmem, bf16_hbm, out_sem).start()


def _prompt_kernel(x_ref, p_ref, *rest):
    small = rest[:N_SMALL]
    mm_f32_hbm = rest[N_SMALL:N_SMALL + N_MM]
    outs = rest[N_SMALL + N_MM:]
    y_ref, conv_ref, vrow_ref = outs[:3]
    mm_bf16_hbm = outs[3:3 + N_MM]
    scratch = outs[3 + N_MM:]
    pbuf_ref = scratch[0]
    mm_bf16 = scratch[1:1 + N_MM]
    stage_ref, in_sem, out_sem = scratch[1 + N_MM:]
    w = dict(zip(SMALL_NAMES + MATMUL_NAMES, small + mm_bf16))
    tm, d = x_ref.shape[1], x_ref.shape[2]
    bf = jnp.bfloat16
    step = pl.program_id(0) * pl.num_programs(1) + pl.program_id(1)

    @pl.when(step == 0)
    def _():
        _load_weights_as_bf16(mm_f32_hbm, mm_bf16, mm_bf16_hbm, stage_ref, in_sem, out_sem)

    @pl.when(step == 1)
    def _():
        for wi in range(N_MM):
            _export_copy(wi, mm_bf16, mm_bf16_hbm, out_sem).wait()

    @pl.when(pl.program_id(1) == 0)
    def _():
        pbuf_ref[0:CARRY_ROWS, :] = jnp.zeros((CARRY_ROWS, d), jnp.float32)

    def spatial_mix(v):
        v_b = v.astype(bf)
        row = lax.broadcasted_iota(jnp.int32, (CHUNK, CHUNK), 0)
        col = lax.broadcasted_iota(jnp.int32, (CHUNK, CHUNK), 1)
        mix = [jnp.where(col <= row, w["mix"][hd], 0.0).astype(bf)
               for hd in range(N_HEADS)]
        s_rows = []
        for c in range(tm // CHUNK):
            s_heads = [_dot(mix[hd], v_b[c * CHUNK:(c + 1) * CHUNK,
                                         hd * HEAD_W:(hd + 1) * HEAD_W])
                       for hd in range(N_HEADS)]
            s_rows.append(jnp.concatenate(s_heads, axis=1) + w["mix_b"][...])
        return jnp.concatenate(s_rows, axis=0)

    def conv_taps(u):
        pbuf_ref[CARRY_ROWS:CARRY_ROWS + tm, :] = u
        return (pbuf_ref[CARRY_ROWS - 1:CARRY_ROWS - 1 + tm, :],
                pbuf_ref[CARRY_ROWS - 2:CARRY_ROWS - 2 + tm, :])

    y, v, _ = _layer_tile(x_ref[0], p_ref[0], w, spatial_mix, conv_taps)
    y_ref[0] = y
    vrow_ref[0, 0] = v[tm - CHUNK:, :]
    tail = pbuf_ref[tm:tm + CARRY_ROWS, :]
    pbuf_ref[0:CARRY_ROWS, :] = tail
    conv_ref[0, 0] = tail[CARRY_ROWS - (CONV_W - 1):, :]


def _sample_kernel(x_ref, p_ref, st_ref, *rest):
    w = dict(zip(SMALL_NAMES + MATMUL_NAMES, rest[:N_SMALL + N_MM]))
    y_ref, conv_ref, vrow_ref = rest[N_SMALL + N_MM:]
    n_seq, n_t, _ = x_ref.shape

    def slab(a, t):
        return a[t * n_seq:(t + 1) * n_seq, :]

    def gather(ref):
        return jnp.concatenate([ref[:, t, :] for t in range(ref.shape[1])], axis=0)

    def scatter(ref, a):
        for t in range(ref.shape[1]):
            ref[:, t, :] = slab(a, t)

    def spatial_mix(v):
        out = []
        for t in range(n_t):
            acc = w["mix_b"][t:t + 1, :] + w["mix"][t * n_t:t * n_t + 1, :] * slab(v, 0)
            for j in range(1, t + 1):
                acc = acc + w["mix"][t * n_t + j:t * n_t + j + 1, :] * slab(v, j)
            out.append(acc)
        return jnp.concatenate(out, axis=0)

    def conv_taps(u):
        full = [st_ref[:, j, :] for j in range(CONV_W - 1)] + [
            slab(u, t) for t in range(n_t)]
        return (jnp.concatenate(full[1:1 + n_t], axis=0),
                jnp.concatenate(full[0:n_t], axis=0))

    y, v, u = _layer_tile(gather(x_ref), gather(p_ref), w, spatial_mix, conv_taps)
    scatter(y_ref, y)
    scatter(vrow_ref, v)
    scatter(conv_ref, u[(n_t - (CONV_W - 1)) * n_seq:, :])


def _resident(shape):
    return pl.BlockSpec(shape, lambda *_: (0,) * len(shape),
                        pipeline_mode=pl.Buffered(1))


def kernel(x_prompt, x_sample, state_conv, p_prompt, p_sample, norm_g, w_in, ln_v_g,
           ln_v_b, w_s, b_s, conv_w, w_a_out, w_b_out, w_o, pe_norm_g, w_pe_gate,
           w_pe_proj, final_norm_g):
    assert w_in.shape[0] == 1, "single-layer step only"
    batch, seq, d = x_prompt.shape
    n_seq, n_t, _ = x_sample.shape
    p_dim = p_prompt.shape[-1]
    tm = PROMPT_TILE
    assert seq % tm == 0 and tm % CHUNK == 0 and d == N_HEADS * HEAD_W
    assert batch * (seq // tm) >= 2, "weight copies are drained in the second step"
    assert CONV_W - 1 <= n_t <= CHUNK
    f32, bf = jnp.float32, jnp.bfloat16

    row2 = lambda a: a.reshape(1, -1)
    bs_full = jnp.repeat(b_s[0].T, HEAD_W, axis=1)
    small = dict(norm_g=row2(norm_g[0]), ln_g=row2(ln_v_g[0]), ln_b=row2(ln_v_b[0]),
                 mix=w_s[0], mix_b=bs_full, conv_w=conv_w[0], pe_g=row2(pe_norm_g[0]),
                 fin_g=row2(final_norm_g))
    mm_f32 = dict(w_in=w_in[0], w_a=w_a_out[0], w_b=w_b_out[0], w_o=w_o[0],
                  w_pg=w_pe_gate[0], w_pp=w_pe_proj[0])

    def specs(arrs):
        return [_resident(a.shape) for a in arrs]

    prompt_small = tuple(small[k] for k in SMALL_NAMES)
    mm_in = tuple(mm_f32[k] for k in MATMUL_NAMES)
    hbm = pl.BlockSpec(memory_space=pltpu.HBM)
    outs = pl.pallas_call(
        _prompt_kernel,
        grid=(batch, seq // tm),
        in_specs=[pl.BlockSpec((1, tm, d), lambda b, i: (b, i, 0)),
                  pl.BlockSpec((1, tm, p_dim), lambda b, i: (b, i, 0))]
        + specs(prompt_small) + [hbm] * N_MM,
        out_specs=[pl.BlockSpec((1, tm, d), lambda b, i: (b, i, 0)),
                   pl.BlockSpec((1, 1, CONV_W - 1, d), lambda b, i: (0, b, 0, 0)),
                   pl.BlockSpec((1, 1, CHUNK, d), lambda b, i: (0, b, 0, 0))]
        + [hbm] * N_MM,
        out_shape=[jax.ShapeDtypeStruct((batch, seq, d), f32),
                   jax.ShapeDtypeStruct((1, batch, CONV_W - 1, d), f32),
                   jax.ShapeDtypeStruct((1, batch, CHUNK, d), f32)]
        + [jax.ShapeDtypeStruct(a.shape, bf) for a in mm_in],
        scratch_shapes=[pltpu.VMEM((tm + CARRY_ROWS, d), f32)]
        + [pltpu.VMEM(a.shape, bf) for a in mm_in]
        + [pltpu.VMEM((STAGE_SLOTS, STAGE_ROWS, STAGE_COLS), f32),
           pltpu.SemaphoreType.DMA((STAGE_SLOTS,)), pltpu.SemaphoreType.DMA((N_MM,))],
        compiler_params=pltpu.CompilerParams(
            dimension_semantics=("arbitrary", "arbitrary"),
            vmem_limit_bytes=V7X_VMEM_LIMIT_BYTES),
        name="prompt_layer",
    )(x_prompt, p_prompt[0], *prompt_small, *mm_in)
    y_p, conv_p, v_p = outs[:3]
    mm_bf16 = tuple(outs[3:])

    wmix = jnp.repeat(jnp.transpose(w_s[0][:, :n_t, :n_t], (1, 2, 0)), HEAD_W,
                      axis=2).reshape(n_t * n_t, d)
    sample_small = tuple(dict(small, mix=wmix, mix_b=bs_full[:n_t])[k]
                         for k in SMALL_NAMES)
    sample_in = (x_sample, p_sample[0], state_conv[0]) + sample_small + mm_bf16
    out_shapes = ((n_seq, n_t, d), (n_seq, CONV_W - 1, d), (n_seq, n_t, d))
    y_s, conv_s, v_s = pl.pallas_call(
        _sample_kernel,
        grid=(1,),
        in_specs=specs(sample_in),
        out_specs=[_resident(s) for s in out_shapes],
        out_shape=[jax.ShapeDtypeStruct(s, f32) for s in out_shapes],
        compiler_params=pltpu.CompilerParams(
            dimension_semantics=("arbitrary",),
            vmem_limit_bytes=V7X_VMEM_LIMIT_BYTES),
        name="sample_layer",
    )(*sample_in)

    return (y_p, y_s, conv_p, conv_s[None], v_p, v_s[None])
```

```python
import functools
import math

import jax
import jax.numpy as jnp
from jax import lax
from jax.experimental import pallas as pl
from jax.experimental.pallas import tpu as pltpu

CHUNK = 128
N_HEADS = 8
HEAD_W = 128
CONV_W = 3
EPS = 1e-6
LN_EPS = 1e-5
COL_U, COL_V, COL_GA, COL_C, COL_B, COL_H, COL_GB, COL_MA, COL_MB = range(9)

V7X_VMEM_LIMIT_BYTES = 60000 * 1024
PROMPT_TILE = 512
CARRY_ROWS = 8
STAGE_ROWS, STAGE_COLS = 256, 1024
STAGE_SLOTS = 8


def _gelu(x):
    c = math.sqrt(2.0 / math.pi)
    return x * (0.5 * (1.0 + jnp.tanh(c * (x + 0.044715 * (x * x * x)))))


def _sigmoid(x):
    return 0.5 * jnp.tanh(0.5 * x) + 0.5


def _silu(x):
    return x * _sigmoid(x)


def _rms_norm(x, g):
    return x * lax.rsqrt(jnp.mean(x * x, axis=-1, keepdims=True) + EPS) * g


def _layer_norm(x, g, b):
    mu = jnp.mean(x, axis=-1, keepdims=True)
    xc = x - mu
    var = jnp.mean(xc * xc, axis=-1, keepdims=True)
    return xc * lax.rsqrt(var + LN_EPS) * g + b


def _dot(a, b):
    return jnp.dot(a, b, preferred_element_type=jnp.float32)


def _layer_tile(x, p, w, spatial_mix, conv_taps):
    d = x.shape[-1]
    bf = jnp.bfloat16
    h = _rms_norm(x, w["norm_g"][...]).astype(bf)

    def proj(col):
        return _dot(h, w["w_in"][:, col * d:(col + 1) * d])

    v = _layer_norm(_gelu(proj(COL_V)), w["ln_g"][...], w["ln_b"][...])
    y_a = _gelu(proj(COL_U)) * spatial_mix(v) * _silu(proj(COL_GA))

    u = proj(COL_C) * proj(COL_H)
    u1, u2 = conv_taps(u)
    cw = w["conv_w"]
    conv = cw[0:1, :] * u2 + cw[1:2, :] * u1 + cw[2:3, :] * u
    y_b = proj(COL_B) * conv * _silu(proj(COL_GB))

    merged = (_sigmoid(proj(COL_MA)) * _dot(y_a.astype(bf), w["w_a"][...])
              + _sigmoid(proj(COL_MB)) * _dot(y_b.astype(bf), w["w_b"][...]))
    x = x + _dot(merged.astype(bf), w["w_o"][...])
    gate = _sigmoid(_dot(_rms_norm(x, w["pe_g"][...]).astype(bf), w["w_pg"][...]))
    x = x + gate * _dot(p.astype(bf), w["w_pp"][...])
    return _rms_norm(x, w["fin_g"][...]), v, u


SMALL_NAMES = ("norm_g", "ln_g", "ln_b", "mix", "mix_b", "conv_w", "pe_g", "fin_g")
MATMUL_NAMES = ("w_in", "w_a", "w_b", "w_o", "w_pg", "w_pp")
N_SMALL, N_MM = len(SMALL_NAMES), len(MATMUL_NAMES)


def _weight_chunks(shapes):
    chunks = []
    for wi, (rows, cols) in enumerate(shapes):
        rb = min(rows, STAGE_ROWS)
        assert rows % rb == 0 and cols % STAGE_COLS == 0
        for c0 in range(0, cols, STAGE_COLS):
            for r0 in range(0, rows, rb):
                chunks.append((wi, r0, rb, c0))
    return chunks


def _export_copy(wi, vmem_refs, hbm_refs, sem):
    return pltpu.make_async_copy(vmem_refs[wi], hbm_refs[wi], sem.at[wi])


def _load_weights_as_bf16(f32_hbm, bf16_vmem, bf16_hbm, stage_ref, in_sem, out_sem):
    chunks = _weight_chunks([r.shape for r in f32_hbm])
    n_slots = stage_ref.shape[0]

    def fetch(k):
        wi, r0, rb, c0 = chunks[k]
        return pltpu.make_async_copy(
            f32_hbm[wi].at[pl.ds(r0, rb), pl.ds(c0, STAGE_COLS)],
            stage_ref.at[k % n_slots, pl.ds(0, rb), :], in_sem.at[k % n_slots])

    for k in range(min(n_slots, len(chunks))):
        fetch(k).start()
    for k, (wi, r0, rb, c0) in enumerate(chunks):
        fetch(k).wait()
        bf16_vmem[wi][pl.ds(r0, rb), pl.ds(c0, STAGE_COLS)] = (
            stage_ref[k % n_slots, 0:rb, :].astype(jnp.bfloat16))
        if k + n_slots < len(chunks):
            fetch(k + n_slots).start()
    for wi in range(len(f32_hbm)):
        _export_copy(wi, bf16_vmem, bf16_hbm, out_sem).start()


def _prompt_kernel(x_ref, p_ref, *rest):
    small = rest[:N_SMALL]
    mm_f32_hbm = rest[N_SMALL:N_SMALL + N_MM]
    outs = rest[N_SMALL + N_MM:]
    y_ref, conv_ref, vrow_ref = outs[:3]
    mm_bf16_hbm = outs[3:3 + N_MM]
    scratch = outs[3 + N_MM:]
    pbuf_ref = scratch[0]
    mm_bf16 = scratch[1:1 + N_MM]
    stage_ref, in_sem, out_sem = scratch[1 + N_MM:]
    w = dict(zip(SMALL_NAMES + MATMUL_NAMES, small + mm_bf16))
    tm, d = x_ref.shape[1], x_ref.shape[2]
    bf = jnp.bfloat16
    step = pl.program_id(0) * pl.num_programs(1) + pl.program_id(1)

    @pl.when(step == 0)
    def _():
        _load_weights_as_bf16(mm_f32_hbm, mm_bf16, mm_bf16_hbm, stage_ref, in_sem, out_sem)

    @pl.when(step == 1)
    def _():
        for wi in range(N_MM):
            _export_copy(wi, mm_bf16, mm_bf16_hbm, out_sem).wait()

    @pl.when(pl.program_id(1) == 0)
    def _():
        pbuf_ref[0:CARRY_ROWS, :] = jnp.zeros((CARRY_ROWS, d), jnp.float32)

    def spatial_mix(v):
        v_b = v.astype(bf)
        row = lax.broadcasted_iota(jnp.int32, (CHUNK, CHUNK), 0)
        col = lax.broadcasted_iota(jnp.int32, (CHUNK, CHUNK), 1)
        mix = [jnp.where(col <= row, w["mix"][hd], 0.0).astype(bf)
               for hd in range(N_HEADS)]
        s_rows = []
        for c in range(tm // CHUNK):
            s_heads = [_dot(mix[hd], v_b[c * CHUNK:(c + 1) * CHUNK,
                                         hd * HEAD_W:(hd + 1) * HEAD_W])
                       for hd in range(N_HEADS)]
            s_rows.append(jnp.concatenate(s_heads, axis=1) + w["mix_b"][...])
        return jnp.concatenate(s_rows, axis=0)

    def conv_taps(u):
        pbuf_ref[CARRY_ROWS:CARRY_ROWS + tm, :] = u
        return (pbuf_ref[CARRY_ROWS - 1:CARRY_ROWS - 1 + tm, :],
                pbuf_ref[CARRY_ROWS - 2:CARRY_ROWS - 2 + tm, :])

    y, v, _ = _layer_tile(x_ref[0], p_ref[0], w, spatial_mix, conv_taps)
    y_ref[0] = y
    vrow_ref[0, 0] = v[tm - CHUNK:, :]
    tail = pbuf_ref[tm:tm + CARRY_ROWS, :]
    pbuf_ref[0:CARRY_ROWS, :] = tail
    conv_ref[0, 0] = tail[CARRY_ROWS - (CONV_W - 1):, :]


def _sample_kernel(x_ref, p_ref, st_ref, *rest):
    w = dict(zip(SMALL_NAMES + MATMUL_NAMES, rest[:N_SMALL + N_MM]))
    y_ref, conv_ref, vrow_ref = rest[N_SMALL + N_MM:]
    n_seq, n_t, _ = x_ref.shape

    def slab(a, t):
        return a[t * n_seq:(t + 1) * n_seq, :]

    def gather(ref):
        return jnp.concatenate([ref[:, t, :] for t in range(ref.shape[1])], axis=0)

    def scatter(ref, a):
        for t in range(ref.shape[1]):
            ref[:, t, :] = slab(a, t)

    def spatial_mix(v):
        out = []
        for t in range(n_t):
            acc = w["mix_b"][t:t + 1, :] + w["mix"][t * n_t:t * n_t + 1, :] * slab(v, 0)
            for j in range(1, t + 1):
                acc = acc + w["mix"][t * n_t + j:t * n_t + j + 1, :] * slab(v, j)
            out.append(acc)
        return jnp.concatenate(out, axis=0)

    def conv_taps(u):
        full = [st_ref[:, j, :] for j in range(CONV_W - 1)] + [
            slab(u, t) for t in range(n_t)]
        return (jnp.concatenate(full[1:1 + n_t], axis=0),
                jnp.concatenate(full[0:n_t], axis=0))

    y, v, u = _layer_tile(gather(x_ref), gather(p_ref), w, spatial_mix, conv_taps)
    scatter(y_ref, y)
    scatter(vrow_ref, v)
    scatter(conv_ref, u[(n_t - (CONV_W - 1)) * n_seq:, :])


def _resident(shape):
    return pl.BlockSpec(shape, lambda *_: (0,) * len(shape),
                        pipeline_mode=pl.Buffered(1))


def kernel(x_prompt, x_sample, state_conv, p_prompt, p_sample, norm_g, w_in, ln_v_g,
           ln_v_b, w_s, b_s, conv_w, w_a_out, w_b_out, w_o, pe_norm_g, w_pe_gate,
           w_pe_proj, final_norm_g):
    assert w_in.shape[0] == 1, "single-layer step only"
    batch, seq, d = x_prompt.shape
    n_seq, n_t, _ = x_sample.shape
    p_dim = p_prompt.shape[-1]
    tm = PROMPT_TILE
    assert seq % tm == 0 and tm % CHUNK == 0 and d == N_HEADS * HEAD_W
    assert batch * (seq // tm) >= 2, "weight copies are drained in the second step"
    assert CONV_W - 1 <= n_t <= CHUNK
    f32, bf = jnp.float32, jnp.bfloat16

    row2 = lambda a: a.reshape(1, -1)
    bs_full = jnp.repeat(b_s[0].T, HEAD_W, axis=1)
    small = dict(norm_g=row2(norm_g[0]), ln_g=row2(ln_v_g[0]), ln_b=row2(ln_v_b[0]),
                 mix=w_s[0], mix_b=bs_full, conv_w=conv_w[0], pe_g=row2(pe_norm_g[0]),
                 fin_g=row2(final_norm_g))
    mm_f32 = dict(w_in=w_in[0], w_a=w_a_out[0], w_b=w_b_out[0], w_o=w_o[0],
                  w_pg=w_pe_gate[0], w_pp=w_pe_proj[0])

    def specs(arrs):
        return [_resident(a.shape) for a in arrs]

    prompt_small = tuple(small[k] for k in SMALL_NAMES)
    mm_in = tuple(mm_f32[k] for k in MATMUL_NAMES)
    hbm = pl.BlockSpec(memory_space=pltpu.HBM)
    outs = pl.pallas_call(
        _prompt_kernel,
        grid=(batch, seq // tm),
        in_specs=[pl.BlockSpec((1, tm, d), lambda b, i: (b, i, 0)),
                  pl.BlockSpec((1, tm, p_dim), lambda b, i: (b, i, 0))]
        + specs(prompt_small) + [hbm] * N_MM,
        out_specs=[pl.BlockSpec((1, tm, d), lambda b, i: (b, i, 0)),
                   pl.BlockSpec((1, 1, CONV_W - 1, d), lambda b, i: (0, b, 0, 0)),
                   pl.BlockSpec((1, 1, CHUNK, d), lambda b, i: (0, b, 0, 0))]
        + [hbm] * N_MM,
        out_shape=[jax.ShapeDtypeStruct((batch, seq, d), f32),
                   jax.ShapeDtypeStruct((1, batch, CONV_W - 1, d), f32),
                   jax.ShapeDtypeStruct((1, batch, CHUNK, d), f32)]
        + [jax.ShapeDtypeStruct(a.shape, bf) for a in mm_in],
        scratch_shapes=[pltpu.VMEM((tm + CARRY_ROWS, d), f32)]
        + [pltpu.VMEM(a.shape, bf) for a in mm_in]
        + [pltpu.VMEM((STAGE_SLOTS, STAGE_ROWS, STAGE_COLS), f32),
           pltpu.SemaphoreType.DMA((STAGE_SLOTS,)), pltpu.SemaphoreType.DMA((N_MM,))],
        compiler_params=pltpu.CompilerParams(
            dimension_semantics=("arbitrary", "arbitrary"),
            vmem_limit_bytes=V7X_VMEM_LIMIT_BYTES),
        name="prompt_layer",
    )(x_prompt, p_prompt[0], *prompt_small, *mm_in)
    y_p, conv_p, v_p = outs[:3]
    mm_bf16 = tuple(outs[3:])

    wmix = jnp.repeat(jnp.transpose(w_s[0][:, :n_t, :n_t], (1, 2, 0)), HEAD_W,
                      axis=2).reshape(n_t * n_t, d)
    sample_small = tuple(dict(small, mix=wmix, mix_b=bs_full[:n_t])[k]
                         for k in SMALL_NAMES)
    sample_in = (x_sample, p_sample[0], state_conv[0]) + sample_small + mm_bf16
    out_shapes = ((n_seq, n_t, d), (n_seq, CONV_W - 1, d), (n_seq, n_t, d))
    y_s, conv_s, v_s = pl.pallas_call(
        _sample_kernel,
        grid=(1,),
        in_specs=specs(sample_in),
        out_specs=[_resident(s) for s in out_shapes],
        out_shape=[jax.ShapeDtypeStruct(s, f32) for s in out_shapes],
        compiler_params=pltpu.CompilerParams(
            dimension_semantics=("arbitrary",),
            vmem_limit_bytes=V7X_VMEM_LIMIT_BYTES),
        name="sample_layer",
    )(*sample_in)

    return (y_p, y_s, conv_p, conv_s[None], v_p, v_s[None])
```

```python
import functools
import math

import jax
import jax.numpy as jnp
from jax import lax
from jax.experimental import pallas as pl
from jax.experimental.pallas import tpu as pltpu

CHUNK = 128
N_HEADS = 8
HEAD_W = 128
CONV_W = 3
EPS = 1e-6
LN_EPS = 1e-5
COL_U, COL_V, COL_GA, COL_C, COL_B, COL_H, COL_GB, COL_MA, COL_MB = range(9)

V7X_VMEM_LIMIT_BYTES = 60000 * 1024
PROMPT_TILE = 512
CARRY_ROWS = 8
STAGE_ROWS, STAGE_COLS = 256, 1024
STAGE_SLOTS = 8


def _gelu(x):
    c = math.sqrt(2.0 / math.pi)
    return x * (0.5 * (1.0 + jnp.tanh(c * (x + 0.044715 * (x * x * x)))))


def _sigmoid(x):
    return 0.5 * jnp.tanh(0.5 * x) + 0.5


def _silu(x):
    return x * _sigmoid(x)


def _rms_norm(x, g):
    return x * lax.rsqrt(jnp.mean(x * x, axis=-1, keepdims=True) + EPS) * g


def _layer_norm(x, g, b):
    mu = jnp.mean(x, axis=-1, keepdims=True)
    xc = x - mu
    var = jnp.mean(xc * xc, axis=-1, keepdims=True)
    return xc * lax.rsqrt(var + LN_EPS) * g + b


def _dot(a, b):
    return jnp.dot(a, b, preferred_element_type=jnp.float32)


def _layer_tile(x, p, w, spatial_mix, conv_taps, weights_ready=lambda stage: None):
    d = x.shape[-1]
    bf = jnp.bfloat16
    h = _rms_norm(x, w["norm_g"][...]).astype(bf)
    weights_ready(0)

    def proj(col):
        return _dot(h, w["w_in"][:, col * d:(col + 1) * d])

    v = _layer_norm(_gelu(proj(COL_V)), w["ln_g"][...], w["ln_b"][...])
    y_a = _gelu(proj(COL_U)) * spatial_mix(v) * _silu(proj(COL_GA))

    weights_ready(1)
    u = proj(COL_C) * proj(COL_H)
    u1, u2 = conv_taps(u)
    cw = w["conv_w"]
    conv = cw[0:1, :] * u2 + cw[1:2, :] * u1 + cw[2:3, :] * u
    y_b = proj(COL_B) * conv * _silu(proj(COL_GB))

    weights_ready(2)
    merged = (_sigmoid(proj(COL_MA)) * _dot(y_a.astype(bf), w["w_a"][...])
              + _sigmoid(proj(COL_MB)) * _dot(y_b.astype(bf), w["w_b"][...]))
    x = x + _dot(merged.astype(bf), w["w_o"][...])
    gate = _sigmoid(_dot(_rms_norm(x, w["pe_g"][...]).astype(bf), w["w_pg"][...]))
    x = x + gate * _dot(p.astype(bf), w["w_pp"][...])
    return _rms_norm(x, w["fin_g"][...]), v, u


SMALL_NAMES = ("norm_g", "ln_g", "ln_b", "mix", "mix_b", "conv_w", "pe_g", "fin_g")
MATMUL_NAMES = ("w_in", "w_a", "w_b", "w_o", "w_pg", "w_pp")
N_SMALL, N_MM = len(SMALL_NAMES), len(MATMUL_NAMES)


def _weight_chunks(shapes):
    chunks = []
    for wi, (rows, cols) in enumerate(shapes):
        rb = min(rows, STAGE_ROWS)
        assert rows % rb == 0 and cols % STAGE_COLS == 0
        for c0 in range(0, cols, STAGE_COLS):
            for r0 in range(0, rows, rb):
                chunks.append((wi, r0, rb, c0))
    return chunks


def _export_copy(wi, vmem_refs, hbm_refs, sem):
    return pltpu.make_async_copy(vmem_refs[wi], hbm_refs[wi], sem.at[wi])


def _load_weights_as_bf16(f32_hbm, bf16_vmem, bf16_hbm, stage_ref, in_sem, out_sem):
    chunks = _weight_chunks([r.shape for r in f32_hbm])
    n_slots = stage_ref.shape[0]

    def fetch(k):
        wi, r0, rb, c0 = chunks[k]
        return pltpu.make_async_copy(
            f32_hbm[wi].at[pl.ds(r0, rb), pl.ds(c0, STAGE_COLS)],
            stage_ref.at[k % n_slots, pl.ds(0, rb), :], in_sem.at[k % n_slots])

    for k in range(min(n_slots, len(chunks))):
        fetch(k).start()
    for k, (wi, r0, rb, c0) in enumerate(chunks):
        fetch(k).wait()
        bf16_vmem[wi][pl.ds(r0, rb), pl.ds(c0, STAGE_COLS)] = (
            stage_ref[k % n_slots, 0:rb, :].astype(jnp.bfloat16))
        if k + n_slots < len(chunks):
            fetch(k + n_slots).start()
    for wi in range(len(f32_hbm)):
        _export_copy(wi, bf16_vmem, bf16_hbm, out_sem).start()


def _prompt_kernel(x_ref, p_ref, *rest):
    small = rest[:N_SMALL]
    mm_f32_hbm = rest[N_SMALL:N_SMALL + N_MM]
    outs = rest[N_SMALL + N_MM:]
    y_ref, conv_ref, vrow_ref = outs[:3]
    mm_bf16_hbm = outs[3:3 + N_MM]
    scratch = outs[3 + N_MM:]
    pbuf_ref = scratch[0]
    mm_bf16 = scratch[1:1 + N_MM]
    stage_ref, in_sem, out_sem = scratch[1 + N_MM:]
    w = dict(zip(SMALL_NAMES + MATMUL_NAMES, small + mm_bf16))
    tm, d = x_ref.shape[1], x_ref.shape[2]
    bf = jnp.bfloat16
    step = pl.program_id(0) * pl.num_programs(1) + pl.program_id(1)

    @pl.when(step == 0)
    def _():
        _load_weights_as_bf16(mm_f32_hbm, mm_bf16, mm_bf16_hbm, stage_ref, in_sem, out_sem)

    @pl.when(step == 1)
    def _():
        for wi in range(N_MM):
            _export_copy(wi, mm_bf16, mm_bf16_hbm, out_sem).wait()

    @pl.when(pl.program_id(1) == 0)
    def _():
        pbuf_ref[0:CARRY_ROWS, :] = jnp.zeros((CARRY_ROWS, d), jnp.float32)

    def spatial_mix(v):
        v_b = v.astype(bf)
        row = lax.broadcasted_iota(jnp.int32, (CHUNK, CHUNK), 0)
        col = lax.broadcasted_iota(jnp.int32, (CHUNK, CHUNK), 1)
        mix = [jnp.where(col <= row, w["mix"][hd], 0.0).astype(bf)
               for hd in range(N_HEADS)]
        s_rows = []
        for c in range(tm // CHUNK):
            s_heads = [_dot(mix[hd], v_b[c * CHUNK:(c + 1) * CHUNK,
                                         hd * HEAD_W:(hd + 1) * HEAD_W])
                       for hd in range(N_HEADS)]
            s_rows.append(jnp.concatenate(s_heads, axis=1) + w["mix_b"][...])
        return jnp.concatenate(s_rows, axis=0)

    def conv_taps(u):
        pbuf_ref[CARRY_ROWS:CARRY_ROWS + tm, :] = u
        return (pbuf_ref[CARRY_ROWS - 1:CARRY_ROWS - 1 + tm, :],
                pbuf_ref[CARRY_ROWS - 2:CARRY_ROWS - 2 + tm, :])

    y, v, _ = _layer_tile(x_ref[0], p_ref[0], w, spatial_mix, conv_taps)
    y_ref[0] = y
    vrow_ref[0, 0] = v[tm - CHUNK:, :]
    tail = pbuf_ref[tm:tm + CARRY_ROWS, :]
    pbuf_ref[0:CARRY_ROWS, :] = tail
    conv_ref[0, 0] = tail[CARRY_ROWS - (CONV_W - 1):, :]


def _sample_kernel(x_ref, p_ref, st_ref, *rest):
    small = rest[:N_SMALL]
    mm_hbm = rest[N_SMALL:N_SMALL + N_MM]
    y_ref, conv_ref, vrow_ref = rest[N_SMALL + N_MM:N_SMALL + N_MM + 3]
    mm_vmem = rest[N_SMALL + N_MM + 3:N_SMALL + 2 * N_MM + 3]
    sem = rest[-1]
    w = dict(zip(SMALL_NAMES + MATMUL_NAMES, small + mm_vmem))
    n_seq, n_t, d = x_ref.shape

    a_cols = (COL_GA + 1) * d
    assert sorted((COL_U, COL_V, COL_GA)) == [0, 1, 2]
    w_in_hbm, w_in_vmem = mm_hbm[0], mm_vmem[0]
    copies = (
        [pltpu.make_async_copy(w_in_hbm.at[:, pl.ds(0, a_cols)],
                               w_in_vmem.at[:, pl.ds(0, a_cols)], sem.at[0])],
        [pltpu.make_async_copy(w_in_hbm.at[:, pl.ds(a_cols, w_in_hbm.shape[1] - a_cols)],
                               w_in_vmem.at[:, pl.ds(a_cols, w_in_hbm.shape[1] - a_cols)],
                               sem.at[1])],
        [pltpu.make_async_copy(mm_hbm[i], mm_vmem[i], sem.at[1 + i])
         for i in range(1, N_MM)])
    for group in copies:
        for c in group:
            c.start()

    def weights_ready(stage):
        for c in copies[stage]:
            c.wait()

    def slab(a, t):
        return a[t * n_seq:(t + 1) * n_seq, :]

    def gather(ref):
        return jnp.concatenate([ref[:, t, :] for t in range(ref.shape[1])], axis=0)

    def scatter(ref, a):
        for t in range(ref.shape[1]):
            ref[:, t, :] = slab(a, t)

    def spatial_mix(v):
        out = []
        for t in range(n_t):
            acc = w["mix_b"][t:t + 1, :] + w["mix"][t * n_t:t * n_t + 1, :] * slab(v, 0)
            for j in range(1, t + 1):
                acc = acc + w["mix"][t * n_t + j:t * n_t + j + 1, :] * slab(v, j)
            out.append(acc)
        return jnp.concatenate(out, axis=0)

    def conv_taps(u):
        full = [st_ref[:, j, :] for j in range(CONV_W - 1)] + [
            slab(u, t) for t in range(n_t)]
        return (jnp.concatenate(full[1:1 + n_t], axis=0),
                jnp.concatenate(full[0:n_t], axis=0))

    y, v, u = _layer_tile(gather(x_ref), gather(p_ref), w, spatial_mix, conv_taps,
                          weights_ready)
    scatter(y_ref, y)
    scatter(vrow_ref, v)
    scatter(conv_ref, u[(n_t - (CONV_W - 1)) * n_seq:, :])


def _resident(shape):
    return pl.BlockSpec(shape, lambda *_: (0,) * len(shape),
                        pipeline_mode=pl.Buffered(1))


def kernel(x_prompt, x_sample, state_conv, p_prompt, p_sample, norm_g, w_in, ln_v_g,
           ln_v_b, w_s, b_s, conv_w, w_a_out, w_b_out, w_o, pe_norm_g, w_pe_gate,
           w_pe_proj, final_norm_g):
    assert w_in.shape[0] == 1, "single-layer step only"
    batch, seq, d = x_prompt.shape
    n_seq, n_t, _ = x_sample.shape
    p_dim = p_prompt.shape[-1]
    tm = PROMPT_TILE
    assert seq % tm == 0 and tm % CHUNK == 0 and d == N_HEADS * HEAD_W
    assert batch * (seq // tm) >= 2, "weight copies are drained in the second step"
    assert CONV_W - 1 <= n_t <= CHUNK
    f32, bf = jnp.float32, jnp.bfloat16

    row2 = lambda a: a.reshape(1, -1)
    bs_full = jnp.repeat(b_s[0].T, HEAD_W, axis=1)
    small = dict(norm_g=row2(norm_g[0]), ln_g=row2(ln_v_g[0]), ln_b=row2(ln_v_b[0]),
                 mix=w_s[0], mix_b=bs_full, conv_w=conv_w[0], pe_g=row2(pe_norm_g[0]),
                 fin_g=row2(final_norm_g))
    mm_f32 = dict(w_in=w_in[0], w_a=w_a_out[0], w_b=w_b_out[0], w_o=w_o[0],
                  w_pg=w_pe_gate[0], w_pp=w_pe_proj[0])

    def specs(arrs):
        return [_resident(a.shape) for a in arrs]

    prompt_small = tuple(small[k] for k in SMALL_NAMES)
    mm_in = tuple(mm_f32[k] for k in MATMUL_NAMES)
    hbm = pl.BlockSpec(memory_space=pltpu.HBM)
    outs = pl.pallas_call(
        _prompt_kernel,
        grid=(batch, seq // tm),
        in_specs=[pl.BlockSpec((1, tm, d), lambda b, i: (b, i, 0)),
                  pl.BlockSpec((1, tm, p_dim), lambda b, i: (b, i, 0))]
        + specs(prompt_small) + [hbm] * N_MM,
        out_specs=[pl.BlockSpec((1, tm, d), lambda b, i: (b, i, 0)),
                   pl.BlockSpec((1, 1, CONV_W - 1, d), lambda b, i: (0, b, 0, 0)),
                   pl.BlockSpec((1, 1, CHUNK, d), lambda b, i: (0, b, 0, 0))]
        + [hbm] * N_MM,
        out_shape=[jax.ShapeDtypeStruct((batch, seq, d), f32),
                   jax.ShapeDtypeStruct((1, batch, CONV_W - 1, d), f32),
                   jax.ShapeDtypeStruct((1, batch, CHUNK, d), f32)]
        + [jax.ShapeDtypeStruct(a.shape, bf) for a in mm_in],
        scratch_shapes=[pltpu.VMEM((tm + CARRY_ROWS, d), f32)]
        + [pltpu.VMEM(a.shape, bf) for a in mm_in]
        + [pltpu.VMEM((STAGE_SLOTS, STAGE_ROWS, STAGE_COLS), f32),
           pltpu.SemaphoreType.DMA((STAGE_SLOTS,)), pltpu.SemaphoreType.DMA((N_MM,))],
        compiler_params=pltpu.CompilerParams(
            dimension_semantics=("arbitrary", "arbitrary"),
            vmem_limit_bytes=V7X_VMEM_LIMIT_BYTES),
        name="prompt_layer",
    )(x_prompt, p_prompt[0], *prompt_small, *mm_in)
    y_p, conv_p, v_p = outs[:3]
    mm_bf16 = tuple(outs[3:])

    wmix = jnp.repeat(jnp.transpose(w_s[0][:, :n_t, :n_t], (1, 2, 0)), HEAD_W,
                      axis=2).reshape(n_t * n_t, d)
    sample_small = tuple(dict(small, mix=wmix, mix_b=bs_full[:n_t])[k]
                         for k in SMALL_NAMES)
    sample_in = (x_sample, p_sample[0], state_conv[0]) + sample_small
    out_shapes = ((n_seq, n_t, d), (n_seq, CONV_W - 1, d), (n_seq, n_t, d))
    y_s, conv_s, v_s = pl.pallas_call(
        _sample_kernel,
        grid=(1,),
        in_specs=specs(sample_in) + [hbm] * N_MM,
        out_specs=[_resident(s) for s in out_shapes],
        out_shape=[jax.ShapeDtypeStruct(s, f32) for s in out_shapes],
        scratch_shapes=[pltpu.VMEM(a.shape, bf) for a in mm_bf16]
        + [pltpu.SemaphoreType.DMA((N_MM + 1,))],
        compiler_params=pltpu.CompilerParams(
            dimension_semantics=("arbitrary",),
            vmem_limit_bytes=V7X_VMEM_LIMIT_BYTES),
        name="sample_layer",
    )(*sample_in, *mm_bf16)

    return (y_p, y_s, conv_p, conv_s[None], v_p, v_s[None])
```

```python
import functools
import math

import jax
import jax.numpy as jnp
from jax import lax
from jax.experimental import pallas as pl
from jax.experimental.pallas import tpu as pltpu

CHUNK = 128
N_HEADS = 8
HEAD_W = 128
CONV_W = 3
EPS = 1e-6
LN_EPS = 1e-5
COL_U, COL_V, COL_GA, COL_C, COL_B, COL_H, COL_GB, COL_MA, COL_MB = range(9)

V7X_VMEM_LIMIT_BYTES = 60000 * 1024
PROMPT_TILE = 512
PROMPT_TILES_PER_STEP = 2
CARRY_ROWS = 8
STAGE_ROWS, STAGE_COLS = 256, 1024
STAGE_SLOTS = 4


def _gelu(x):
    c = math.sqrt(2.0 / math.pi)
    return x * (0.5 * (1.0 + jnp.tanh(c * (x + 0.044715 * (x * x * x)))))


def _sigmoid(x):
    return 0.5 * jnp.tanh(0.5 * x) + 0.5


def _silu(x):
    return x * _sigmoid(x)


def _rms_norm(x, g):
    return x * lax.rsqrt(jnp.mean(x * x, axis=-1, keepdims=True) + EPS) * g


def _layer_norm(x, g, b):
    mu = jnp.mean(x, axis=-1, keepdims=True)
    xc = x - mu
    var = jnp.mean(xc * xc, axis=-1, keepdims=True)
    return xc * lax.rsqrt(var + LN_EPS) * g + b


def _dot(a, b):
    return jnp.dot(a, b, preferred_element_type=jnp.float32)


def _layer_tile(x, p, w, spatial_mix, conv_taps, weights_ready=lambda stage: None):
    d = x.shape[-1]
    bf = jnp.bfloat16
    h = _rms_norm(x, w["norm_g"][...]).astype(bf)
    weights_ready(0)

    def proj(col):
        return _dot(h, w["w_in"][:, col * d:(col + 1) * d])

    v = _layer_norm(_gelu(proj(COL_V)), w["ln_g"][...], w["ln_b"][...])
    y_a = _gelu(proj(COL_U)) * spatial_mix(v) * _silu(proj(COL_GA))

    weights_ready(1)
    u = proj(COL_C) * proj(COL_H)
    u1, u2 = conv_taps(u)
    cw = w["conv_w"]
    conv = cw[0:1, :] * u2 + cw[1:2, :] * u1 + cw[2:3, :] * u
    y_b = proj(COL_B) * conv * _silu(proj(COL_GB))

    weights_ready(2)
    merged = (_sigmoid(proj(COL_MA)) * _dot(y_a.astype(bf), w["w_a"][...])
              + _sigmoid(proj(COL_MB)) * _dot(y_b.astype(bf), w["w_b"][...]))
    x = x + _dot(merged.astype(bf), w["w_o"][...])
    gate = _sigmoid(_dot(_rms_norm(x, w["pe_g"][...]).astype(bf), w["w_pg"][...]))
    x = x + gate * _dot(p.astype(bf), w["w_pp"][...])
    return _rms_norm(x, w["fin_g"][...]), v, u


SMALL_NAMES = ("norm_g", "ln_g", "ln_b", "mix", "mix_b", "conv_w", "pe_g", "fin_g")
MATMUL_NAMES = ("w_in", "w_a", "w_b", "w_o", "w_pg", "w_pp")
N_SMALL, N_MM = len(SMALL_NAMES), len(MATMUL_NAMES)


def _weight_chunks(shapes):
    chunks = []
    for wi, (rows, cols) in enumerate(shapes):
        rb = min(rows, STAGE_ROWS)
        assert rows % rb == 0 and cols % STAGE_COLS == 0
        for c0 in range(0, cols, STAGE_COLS):
            for r0 in range(0, rows, rb):
                chunks.append((wi, r0, rb, c0))
    return chunks


def _export_copy(wi, vmem_refs, hbm_refs, sem):
    return pltpu.make_async_copy(vmem_refs[wi], hbm_refs[wi], sem.at[wi])


def _load_weights_as_bf16(f32_hbm, bf16_vmem, bf16_hbm, stage_ref, in_sem, out_sem):
    chunks = _weight_chunks([r.shape for r in f32_hbm])
    n_slots = stage_ref.shape[0]

    def fetch(k):
        wi, r0, rb, c0 = chunks[k]
        return pltpu.make_async_copy(
            f32_hbm[wi].at[pl.ds(r0, rb), pl.ds(c0, STAGE_COLS)],
            stage_ref.at[k % n_slots, pl.ds(0, rb), :], in_sem.at[k % n_slots])

    for k in range(min(n_slots, len(chunks))):
        fetch(k).start()
    for k, (wi, r0, rb, c0) in enumerate(chunks):
        fetch(k).wait()
        bf16_vmem[wi][pl.ds(r0, rb), pl.ds(c0, STAGE_COLS)] = (
            stage_ref[k % n_slots, 0:rb, :].astype(jnp.bfloat16))
        if k + n_slots < len(chunks):
            fetch(k + n_slots).start()
    for wi in range(len(f32_hbm)):
        _export_copy(wi, bf16_vmem, bf16_hbm, out_sem).start()


def _prompt_kernel(x_ref, p_ref, *rest, tm):
    small = rest[:N_SMALL]
    mm_f32_hbm = rest[N_SMALL:N_SMALL + N_MM]
    outs = rest[N_SMALL + N_MM:]
    y_ref, conv_ref, vrow_ref = outs[:3]
    mm_bf16_hbm = outs[3:3 + N_MM]
    scratch = outs[3 + N_MM:]
    pbuf_ref = scratch[0]
    mm_bf16 = scratch[1:1 + N_MM]
    stage_ref, in_sem, out_sem = scratch[1 + N_MM:]
    w = dict(zip(SMALL_NAMES + MATMUL_NAMES, small + mm_bf16))
    d = x_ref.shape[2]
    bf = jnp.bfloat16
    step = pl.program_id(0) * pl.num_programs(1) + pl.program_id(1)

    @pl.when(step == 0)
    def _():
        _load_weights_as_bf16(mm_f32_hbm, mm_bf16, mm_bf16_hbm, stage_ref, in_sem, out_sem)

    @pl.when(step == 1)
    def _():
        for wi in range(N_MM):
            _export_copy(wi, mm_bf16, mm_bf16_hbm, out_sem).wait()

    @pl.when(pl.program_id(1) == 0)
    def _():
        pbuf_ref[0:CARRY_ROWS, :] = jnp.zeros((CARRY_ROWS, d), jnp.float32)

    def spatial_mix(v):
        v_b = v.astype(bf)
        row = lax.broadcasted_iota(jnp.int32, (CHUNK, CHUNK), 0)
        col = lax.broadcasted_iota(jnp.int32, (CHUNK, CHUNK), 1)
        mix = [jnp.where(col <= row, w["mix"][hd], 0.0).astype(bf)
               for hd in range(N_HEADS)]
        s_rows = []
        for c in range(tm // CHUNK):
            s_heads = [_dot(mix[hd], v_b[c * CHUNK:(c + 1) * CHUNK,
                                         hd * HEAD_W:(hd + 1) * HEAD_W])
                       for hd in range(N_HEADS)]
            s_rows.append(jnp.concatenate(s_heads, axis=1) + w["mix_b"][...])
        return jnp.concatenate(s_rows, axis=0)

    def conv_taps(u):
        pbuf_ref[CARRY_ROWS:CARRY_ROWS + tm, :] = u
        return (pbuf_ref[CARRY_ROWS - 1:CARRY_ROWS - 1 + tm, :],
                pbuf_ref[CARRY_ROWS - 2:CARRY_ROWS - 2 + tm, :])

    def tile(j, carry):
        rows = pl.ds(pl.multiple_of(j * tm, tm), tm)
        y, v, _ = _layer_tile(x_ref[0, rows, :], p_ref[0, rows, :], w, spatial_mix,
                              conv_taps)
        y_ref[0, rows, :] = y
        vrow_ref[0, 0] = v[tm - CHUNK:, :]
        tail = pbuf_ref[tm:tm + CARRY_ROWS, :]
        pbuf_ref[0:CARRY_ROWS, :] = tail
        conv_ref[0, 0] = tail[CARRY_ROWS - (CONV_W - 1):, :]
        return carry

    lax.fori_loop(0, x_ref.shape[1] // tm, tile, 0)


def _sample_kernel(x_ref, p_ref, st_ref, *rest):
    small = rest[:N_SMALL]
    mm_hbm = rest[N_SMALL:N_SMALL + N_MM]
    y_ref, conv_ref, vrow_ref = rest[N_SMALL + N_MM:N_SMALL + N_MM + 3]
    mm_vmem = rest[N_SMALL + N_MM + 3:N_SMALL + 2 * N_MM + 3]
    sem = rest[-1]
    w = dict(zip(SMALL_NAMES + MATMUL_NAMES, small + mm_vmem))
    n_seq, n_t, d = x_ref.shape

    a_cols = (COL_GA + 1) * d
    assert sorted((COL_U, COL_V, COL_GA)) == [0, 1, 2]
    w_in_hbm, w_in_vmem = mm_hbm[0], mm_vmem[0]
    copies = (
        [pltpu.make_async_copy(w_in_hbm.at[:, pl.ds(0, a_cols)],
                               w_in_vmem.at[:, pl.ds(0, a_cols)], sem.at[0])],
        [pltpu.make_async_copy(w_in_hbm.at[:, pl.ds(a_cols, w_in_hbm.shape[1] - a_cols)],
                               w_in_vmem.at[:, pl.ds(a_cols, w_in_hbm.shape[1] - a_cols)],
                               sem.at[1])],
        [pltpu.make_async_copy(mm_hbm[i], mm_vmem[i], sem.at[1 + i])
         for i in range(1, N_MM)])
    for group in copies:
        for c in group:
            c.start()

    def weights_ready(stage):
        for c in copies[stage]:
            c.wait()

    def slab(a, t):
        return a[t * n_seq:(t + 1) * n_seq, :]

    def gather(ref):
        return jnp.concatenate([ref[:, t, :] for t in range(ref.shape[1])], axis=0)

    def scatter(ref, a):
        for t in range(ref.shape[1]):
            ref[:, t, :] = slab(a, t)

    def spatial_mix(v):
        out = []
        for t in range(n_t):
            acc = w["mix_b"][t:t + 1, :] + w["mix"][t * n_t:t * n_t + 1, :] * slab(v, 0)
            for j in range(1, t + 1):
                acc = acc + w["mix"][t * n_t + j:t * n_t + j + 1, :] * slab(v, j)
            out.append(acc)
        return jnp.concatenate(out, axis=0)

    def conv_taps(u):
        full = [st_ref[:, j, :] for j in range(CONV_W - 1)] + [
            slab(u, t) for t in range(n_t)]
        return (jnp.concatenate(full[1:1 + n_t], axis=0),
                jnp.concatenate(full[0:n_t], axis=0))

    y, v, u = _layer_tile(gather(x_ref), gather(p_ref), w, spatial_mix, conv_taps,
                          weights_ready)
    scatter(y_ref, y)
    scatter(vrow_ref, v)
    scatter(conv_ref, u[(n_t - (CONV_W - 1)) * n_seq:, :])


def _resident(shape):
    return pl.BlockSpec(shape, lambda *_: (0,) * len(shape),
                        pipeline_mode=pl.Buffered(1))


def kernel(x_prompt, x_sample, state_conv, p_prompt, p_sample, norm_g, w_in, ln_v_g,
           ln_v_b, w_s, b_s, conv_w, w_a_out, w_b_out, w_o, pe_norm_g, w_pe_gate,
           w_pe_proj, final_norm_g):
    assert w_in.shape[0] == 1, "single-layer step only"
    batch, seq, d = x_prompt.shape
    n_seq, n_t, _ = x_sample.shape
    p_dim = p_prompt.shape[-1]
    tm = PROMPT_TILE
    blk = tm * PROMPT_TILES_PER_STEP
    assert seq % blk == 0 and tm % CHUNK == 0 and d == N_HEADS * HEAD_W
    assert batch * (seq // blk) >= 2, "weight copies are drained in the second step"
    assert CONV_W - 1 <= n_t <= CHUNK
    f32, bf = jnp.float32, jnp.bfloat16

    row2 = lambda a: a.reshape(1, -1)
    bs_full = jnp.repeat(b_s[0].T, HEAD_W, axis=1)
    small = dict(norm_g=row2(norm_g[0]), ln_g=row2(ln_v_g[0]), ln_b=row2(ln_v_b[0]),
                 mix=w_s[0], mix_b=bs_full, conv_w=conv_w[0], pe_g=row2(pe_norm_g[0]),
                 fin_g=row2(final_norm_g))
    mm_f32 = dict(w_in=w_in[0], w_a=w_a_out[0], w_b=w_b_out[0], w_o=w_o[0],
                  w_pg=w_pe_gate[0], w_pp=w_pe_proj[0])

    def specs(arrs):
        return [_resident(a.shape) for a in arrs]

    prompt_small = tuple(small[k] for k in SMALL_NAMES)
    mm_in = tuple(mm_f32[k] for k in MATMUL_NAMES)
    hbm = pl.BlockSpec(memory_space=pltpu.HBM)
    outs = pl.pallas_call(
        functools.partial(_prompt_kernel, tm=tm),
        grid=(batch, seq // blk),
        in_specs=[pl.BlockSpec((1, blk, d), lambda b, i: (b, i, 0)),
                  pl.BlockSpec((1, blk, p_dim), lambda b, i: (b, i, 0))]
        + specs(prompt_small) + [hbm] * N_MM,
        out_specs=[pl.BlockSpec((1, blk, d), lambda b, i: (b, i, 0)),
                   pl.BlockSpec((1, 1, CONV_W - 1, d), lambda b, i: (0, b, 0, 0)),
                   pl.BlockSpec((1, 1, CHUNK, d), lambda b, i: (0, b, 0, 0))]
        + [hbm] * N_MM,
        out_shape=[jax.ShapeDtypeStruct((batch, seq, d), f32),
                   jax.ShapeDtypeStruct((1, batch, CONV_W - 1, d), f32),
                   jax.ShapeDtypeStruct((1, batch, CHUNK, d), f32)]
        + [jax.ShapeDtypeStruct(a.shape, bf) for a in mm_in],
        scratch_shapes=[pltpu.VMEM((tm + CARRY_ROWS, d), f32)]
        + [pltpu.VMEM(a.shape, bf) for a in mm_in]
        + [pltpu.VMEM((STAGE_SLOTS, STAGE_ROWS, STAGE_COLS), f32),
           pltpu.SemaphoreType.DMA((STAGE_SLOTS,)), pltpu.SemaphoreType.DMA((N_MM,))],
        compiler_params=pltpu.CompilerParams(
            dimension_semantics=("arbitrary", "arbitrary"),
            vmem_limit_bytes=V7X_VMEM_LIMIT_BYTES),
        name="prompt_layer",
    )(x_prompt, p_prompt[0], *prompt_small, *mm_in)
    y_p, conv_p, v_p = outs[:3]
    mm_bf16 = tuple(outs[3:])

    wmix = jnp.repeat(jnp.transpose(w_s[0][:, :n_t, :n_t], (1, 2, 0)), HEAD_W,
                      axis=2).reshape(n_t * n_t, d)
    sample_small = tuple(dict(small, mix=wmix, mix_b=bs_full[:n_t])[k]
                         for k in SMALL_NAMES)
    sample_in = (x_sample, p_sample[0], state_conv[0]) + sample_small
    out_shapes = ((n_seq, n_t, d), (n_seq, CONV_W - 1, d), (n_seq, n_t, d))
    y_s, conv_s, v_s = pl.pallas_call(
        _sample_kernel,
        grid=(1,),
        in_specs=specs(sample_in) + [hbm] * N_MM,
        out_specs=[_resident(s) for s in out_shapes],
        out_shape=[jax.ShapeDtypeStruct(s, f32) for s in out_shapes],
        scratch_shapes=[pltpu.VMEM(a.shape, bf) for a in mm_bf16]
        + [pltpu.SemaphoreType.DMA((N_MM + 1,))],
        compiler_params=pltpu.CompilerParams(
            dimension_semantics=("arbitrary",),
            vmem_limit_bytes=V7X_VMEM_LIMIT_BYTES),
        name="sample_layer",
    )(*sample_in, *mm_bf16)

    return (y_p, y_s, conv_p, conv_s[None], v_p, v_s[None])
```

```python
import functools
import math

import jax
import jax.numpy as jnp
from jax import lax
from jax.experimental import pallas as pl
from jax.experimental.pallas import tpu as pltpu

CHUNK = 128
N_HEADS = 8
HEAD_W = 128
CONV_W = 3
EPS = 1e-6
LN_EPS = 1e-5
COL_U, COL_V, COL_GA, COL_C, COL_B, COL_H, COL_GB, COL_MA, COL_MB = range(9)

V7X_VMEM_LIMIT_BYTES = 60000 * 1024
PROMPT_TILE = 512
CARRY_ROWS = 8
STAGE_ROWS, STAGE_COLS = 256, 1024
STAGE_SLOTS = 8


def _gelu(x):
    c = math.sqrt(2.0 / math.pi)
    return x * (0.5 * (1.0 + jnp.tanh(c * (x + 0.044715 * (x * x * x)))))


def _sigmoid(x):
    return 0.5 * jnp.tanh(0.5 * x) + 0.5


def _silu(x):
    return x * _sigmoid(x)


def _rms_norm(x, g):
    return x * lax.rsqrt(jnp.mean(x * x, axis=-1, keepdims=True) + EPS) * g


def _layer_norm(x, g, b):
    mu = jnp.mean(x, axis=-1, keepdims=True)
    xc = x - mu
    var = jnp.mean(xc * xc, axis=-1, keepdims=True)
    return xc * lax.rsqrt(var + LN_EPS) * g + b


def _dot(a, b):
    return jnp.dot(a, b, preferred_element_type=jnp.float32)


WEIGHT_USE_ORDER = (
    ("w_in", COL_V), ("w_in", COL_U), ("w_in", COL_GA), ("w_in", COL_C), ("w_in", COL_H),
    ("w_in", COL_B), ("w_in", COL_GB), ("w_in", COL_MA), "w_a", ("w_in", COL_MB), "w_b",
    "w_o", "w_pg", "w_pp")


def _layer_tile(x, p, w, spatial_mix, conv_taps, weights_ready=lambda key: None):
    d = x.shape[-1]
    bf = jnp.bfloat16
    h = _rms_norm(x, w["norm_g"][...]).astype(bf)

    def proj(col):
        weights_ready(("w_in", col))
        return _dot(h, w["w_in"][:, col * d:(col + 1) * d])

    def mm(a, name):
        weights_ready(name)
        return _dot(a.astype(bf), w[name][...])

    v = _layer_norm(_gelu(proj(COL_V)), w["ln_g"][...], w["ln_b"][...])
    y_a = _gelu(proj(COL_U)) * spatial_mix(v) * _silu(proj(COL_GA))

    u = proj(COL_C) * proj(COL_H)
    u1, u2 = conv_taps(u)
    cw = w["conv_w"]
    conv = cw[0:1, :] * u2 + cw[1:2, :] * u1 + cw[2:3, :] * u
    y_b = proj(COL_B) * conv * _silu(proj(COL_GB))

    merged = (_sigmoid(proj(COL_MA)) * mm(y_a, "w_a")
              + _sigmoid(proj(COL_MB)) * mm(y_b, "w_b"))
    x = x + mm(merged, "w_o")
    gate = _sigmoid(mm(_rms_norm(x, w["pe_g"][...]), "w_pg"))
    x = x + gate * mm(p, "w_pp")
    return _rms_norm(x, w["fin_g"][...]), v, u


SMALL_NAMES = ("norm_g", "ln_g", "ln_b", "mix", "mix_b", "conv_w", "pe_g", "fin_g")
MATMUL_NAMES = ("w_in", "w_a", "w_b", "w_o", "w_pg", "w_pp")
N_SMALL, N_MM = len(SMALL_NAMES), len(MATMUL_NAMES)


def _export_copy(wi, vmem_refs, hbm_refs, sem):
    return pltpu.make_async_copy(vmem_refs[wi], hbm_refs[wi], sem.at[wi])


def _weight_stream(f32_hbm, bf16_vmem, stage_ref, in_sem):
    index = dict(zip(MATMUL_NAMES, range(N_MM)))
    chunks, block_end = [], []
    for key in WEIGHT_USE_ORDER:
        name, col = key if isinstance(key, tuple) else (key, None)
        rows, cols = f32_hbm[index[name]].shape
        rb = min(rows, STAGE_ROWS)
        n_blocks = len([k for k in WEIGHT_USE_ORDER if isinstance(k, tuple) and k[0] == name])
        width = cols if col is None else cols // n_blocks
        assert rows % rb == 0 and width % STAGE_COLS == 0
        c_lo = 0 if col is None else col * width
        for c0 in range(c_lo, c_lo + width, STAGE_COLS):
            for r0 in range(0, rows, rb):
                chunks.append((index[name], r0, rb, c0))
        block_end.append(len(chunks))
    n_slots = stage_ref.shape[0]
    visited = []

    def fetch(k):
        wi, r0, rb, c0 = chunks[k]
        return pltpu.make_async_copy(
            f32_hbm[wi].at[pl.ds(r0, rb), pl.ds(c0, STAGE_COLS)],
            stage_ref.at[k % n_slots, pl.ds(0, rb), :], in_sem.at[k % n_slots])

    for k in range(min(n_slots, len(chunks))):
        fetch(k).start()

    def weights_ready(key):
        i = len(visited)
        assert key == WEIGHT_USE_ORDER[i], (key, WEIGHT_USE_ORDER[i])
        visited.append(key)
        for k in range(block_end[i - 1] if i else 0, block_end[i]):
            wi, r0, rb, c0 = chunks[k]
            fetch(k).wait()
            bf16_vmem[wi][pl.ds(r0, rb), pl.ds(c0, STAGE_COLS)] = (
                stage_ref[k % n_slots, 0:rb, :].astype(jnp.bfloat16))
            if k + n_slots < len(chunks):
                fetch(k + n_slots).start()

    return weights_ready, visited


def _prompt_kernel(x_ref, p_ref, *rest):
    small = rest[:N_SMALL]
    mm_f32_hbm = rest[N_SMALL:N_SMALL + N_MM]
    outs = rest[N_SMALL + N_MM:]
    y_ref, conv_ref, vrow_ref = outs[:3]
    mm_bf16_hbm = outs[3:3 + N_MM]
    scratch = outs[3 + N_MM:]
    pbuf_ref = scratch[0]
    mm_bf16 = scratch[1:1 + N_MM]
    stage_ref, in_sem, out_sem = scratch[1 + N_MM:]
    w = dict(zip(SMALL_NAMES + MATMUL_NAMES, small + mm_bf16))
    tm, d = x_ref.shape[1], x_ref.shape[2]
    bf = jnp.bfloat16
    step = pl.program_id(0) * pl.num_programs(1) + pl.program_id(1)

    @pl.when(step == 1)
    def _():
        for wi in range(N_MM):
            _export_copy(wi, mm_bf16, mm_bf16_hbm, out_sem).wait()

    @pl.when(pl.program_id(1) == 0)
    def _():
        pbuf_ref[0:CARRY_ROWS, :] = jnp.zeros((CARRY_ROWS, d), jnp.float32)

    def spatial_mix(v):
        v_b = v.astype(bf)
        row = lax.broadcasted_iota(jnp.int32, (CHUNK, CHUNK), 0)
        col = lax.broadcasted_iota(jnp.int32, (CHUNK, CHUNK), 1)
        mix = [jnp.where(col <= row, w["mix"][hd], 0.0).astype(bf)
               for hd in range(N_HEADS)]
        s_rows = []
        for c in range(tm // CHUNK):
            s_heads = [_dot(mix[hd], v_b[c * CHUNK:(c + 1) * CHUNK,
                                         hd * HEAD_W:(hd + 1) * HEAD_W])
                       for hd in range(N_HEADS)]
            s_rows.append(jnp.concatenate(s_heads, axis=1) + w["mix_b"][...])
        return jnp.concatenate(s_rows, axis=0)

    def conv_taps(u):
        pbuf_ref[CARRY_ROWS:CARRY_ROWS + tm, :] = u
        return (pbuf_ref[CARRY_ROWS - 1:CARRY_ROWS - 1 + tm, :],
                pbuf_ref[CARRY_ROWS - 2:CARRY_ROWS - 2 + tm, :])

    def run_tile(weights_ready):
        y, v, _ = _layer_tile(x_ref[0], p_ref[0], w, spatial_mix, conv_taps,
                              weights_ready)
        y_ref[0] = y
        vrow_ref[0, 0] = v[tm - CHUNK:, :]
        tail = pbuf_ref[tm:tm + CARRY_ROWS, :]
        pbuf_ref[0:CARRY_ROWS, :] = tail
        conv_ref[0, 0] = tail[CARRY_ROWS - (CONV_W - 1):, :]

    @pl.when(step == 0)
    def _():
        weights_ready, visited = _weight_stream(mm_f32_hbm, mm_bf16, stage_ref, in_sem)
        run_tile(weights_ready)
        assert len(visited) == len(WEIGHT_USE_ORDER)
        for wi in range(N_MM):
            _export_copy(wi, mm_bf16, mm_bf16_hbm, out_sem).start()

    @pl.when(step > 0)
    def _():
        run_tile(lambda key: None)


def _sample_kernel(x_ref, p_ref, st_ref, *rest):
    small = rest[:N_SMALL]
    mm_hbm = rest[N_SMALL:N_SMALL + N_MM]
    y_ref, conv_ref, vrow_ref = rest[N_SMALL + N_MM:N_SMALL + N_MM + 3]
    mm_vmem = rest[N_SMALL + N_MM + 3:N_SMALL + 2 * N_MM + 3]
    sem = rest[-1]
    w = dict(zip(SMALL_NAMES + MATMUL_NAMES, small + mm_vmem))
    n_seq, n_t, d = x_ref.shape

    a_cols = (COL_GA + 1) * d
    assert sorted((COL_U, COL_V, COL_GA)) == [0, 1, 2]
    w_in_hbm, w_in_vmem = mm_hbm[0], mm_vmem[0]
    copies = (
        [pltpu.make_async_copy(w_in_hbm.at[:, pl.ds(0, a_cols)],
                               w_in_vmem.at[:, pl.ds(0, a_cols)], sem.at[0])],
        [pltpu.make_async_copy(w_in_hbm.at[:, pl.ds(a_cols, w_in_hbm.shape[1] - a_cols)],
                               w_in_vmem.at[:, pl.ds(a_cols, w_in_hbm.shape[1] - a_cols)],
                               sem.at[1])],
        [pltpu.make_async_copy(mm_hbm[i], mm_vmem[i], sem.at[1 + i])
         for i in range(1, N_MM)])
    for group in copies:
        for c in group:
            c.start()

    first_use = {("w_in", COL_V): 0, ("w_in", COL_C): 1, "w_a": 2}

    def weights_ready(key):
        for c in copies[first_use[key]] if key in first_use else ():
            c.wait()

    def slab(a, t):
        return a[t * n_seq:(t + 1) * n_seq, :]

    def gather(ref):
        return jnp.concatenate([ref[:, t, :] for t in range(ref.shape[1])], axis=0)

    def scatter(ref, a):
        for t in range(ref.shape[1]):
            ref[:, t, :] = slab(a, t)

    def spatial_mix(v):
        out = []
        for t in range(n_t):
            acc = w["mix_b"][t:t + 1, :] + w["mix"][t * n_t:t * n_t + 1, :] * slab(v, 0)
            for j in range(1, t + 1):
                acc = acc + w["mix"][t * n_t + j:t * n_t + j + 1, :] * slab(v, j)
            out.append(acc)
        return jnp.concatenate(out, axis=0)

    def conv_taps(u):
        full = [st_ref[:, j, :] for j in range(CONV_W - 1)] + [
            slab(u, t) for t in range(n_t)]
        return (jnp.concatenate(full[1:1 + n_t], axis=0),
                jnp.concatenate(full[0:n_t], axis=0))

    y, v, u = _layer_tile(gather(x_ref), gather(p_ref), w, spatial_mix, conv_taps,
                          weights_ready)
    scatter(y_ref, y)
    scatter(vrow_ref, v)
    scatter(conv_ref, u[(n_t - (CONV_W - 1)) * n_seq:, :])


def _resident(shape):
    return pl.BlockSpec(shape, lambda *_: (0,) * len(shape),
                        pipeline_mode=pl.Buffered(1))


def kernel(x_prompt, x_sample, state_conv, p_prompt, p_sample, norm_g, w_in, ln_v_g,
           ln_v_b, w_s, b_s, conv_w, w_a_out, w_b_out, w_o, pe_norm_g, w_pe_gate,
           w_pe_proj, final_norm_g):
    assert w_in.shape[0] == 1, "single-layer step only"
    batch, seq, d = x_prompt.shape
    n_seq, n_t, _ = x_sample.shape
    p_dim = p_prompt.shape[-1]
    tm = PROMPT_TILE
    assert seq % tm == 0 and tm % CHUNK == 0 and d == N_HEADS * HEAD_W
    assert batch * (seq // tm) >= 2, "weight copies are drained in the second step"
    assert CONV_W - 1 <= n_t <= CHUNK
    f32, bf = jnp.float32, jnp.bfloat16

    row2 = lambda a: a.reshape(1, -1)
    bs_full = jnp.repeat(b_s[0].T, HEAD_W, axis=1)
    small = dict(norm_g=row2(norm_g[0]), ln_g=row2(ln_v_g[0]), ln_b=row2(ln_v_b[0]),
                 mix=w_s[0], mix_b=bs_full, conv_w=conv_w[0], pe_g=row2(pe_norm_g[0]),
                 fin_g=row2(final_norm_g))
    mm_f32 = dict(w_in=w_in[0], w_a=w_a_out[0], w_b=w_b_out[0], w_o=w_o[0],
                  w_pg=w_pe_gate[0], w_pp=w_pe_proj[0])

    def specs(arrs):
        return [_resident(a.shape) for a in arrs]

    prompt_small = tuple(small[k] for k in SMALL_NAMES)
    mm_in = tuple(mm_f32[k] for k in MATMUL_NAMES)
    hbm = pl.BlockSpec(memory_space=pltpu.HBM)
    outs = pl.pallas_call(
        _prompt_kernel,
        grid=(batch, seq // tm),
        in_specs=[pl.BlockSpec((1, tm, d), lambda b, i: (b, i, 0)),
                  pl.BlockSpec((1, tm, p_dim), lambda b, i: (b, i, 0))]
        + specs(prompt_small) + [hbm] * N_MM,
        out_specs=[pl.BlockSpec((1, tm, d), lambda b, i: (b, i, 0)),
                   pl.BlockSpec((1, 1, CONV_W - 1, d), lambda b, i: (0, b, 0, 0)),
                   pl.BlockSpec((1, 1, CHUNK, d), lambda b, i: (0, b, 0, 0))]
        + [hbm] * N_MM,
        out_shape=[jax.ShapeDtypeStruct((batch, seq, d), f32),
                   jax.ShapeDtypeStruct((1, batch, CONV_W - 1, d), f32),
                   jax.ShapeDtypeStruct((1, batch, CHUNK, d), f32)]
        + [jax.ShapeDtypeStruct(a.shape, bf) for a in mm_in],
        scratch_shapes=[pltpu.VMEM((tm + CARRY_ROWS, d), f32)]
        + [pltpu.VMEM(a.shape, bf) for a in mm_in]
        + [pltpu.VMEM((STAGE_SLOTS, STAGE_ROWS, STAGE_COLS), f32),
           pltpu.SemaphoreType.DMA((STAGE_SLOTS,)), pltpu.SemaphoreType.DMA((N_MM,))],
        compiler_params=pltpu.CompilerParams(
            dimension_semantics=("arbitrary", "arbitrary"),
            vmem_limit_bytes=V7X_VMEM_LIMIT_BYTES),
        name="prompt_layer",
    )(x_prompt, p_prompt[0], *prompt_small, *mm_in)
    y_p, conv_p, v_p = outs[:3]
    mm_bf16 = tuple(outs[3:])

    wmix = jnp.repeat(jnp.transpose(w_s[0][:, :n_t, :n_t], (1, 2, 0)), HEAD_W,
                      axis=2).reshape(n_t * n_t, d)
    sample_small = tuple(dict(small, mix=wmix, mix_b=bs_full[:n_t])[k]
                         for k in SMALL_NAMES)
    sample_in = (x_sample, p_sample[0], state_conv[0]) + sample_small
    out_shapes = ((n_seq, n_t, d), (n_seq, CONV_W - 1, d), (n_seq, n_t, d))
    y_s, conv_s, v_s = pl.pallas_call(
        _sample_kernel,
        grid=(1,),
        in_specs=specs(sample_in) + [hbm] * N_MM,
        out_specs=[_resident(s) for s in out_shapes],
        out_shape=[jax.ShapeDtypeStruct(s, f32) for s in out_shapes],
        scratch_shapes=[pltpu.VMEM(a.shape, bf) for a in mm_bf16]
        + [pltpu.SemaphoreType.DMA((N_MM + 1,))],
        compiler_params=pltpu.CompilerParams(
            dimension_semantics=("arbitrary",),
            vmem_limit_bytes=V7X_VMEM_LIMIT_BYTES),
        name="sample_layer",
    )(*sample_in, *mm_bf16)

    return (y_p, y_s, conv_p, conv_s[None], v_p, v_s[None])
```

```python
import functools
import math

import jax
import jax.numpy as jnp
from jax import lax
from jax.experimental import pallas as pl
from jax.experimental.pallas import tpu as pltpu

CHUNK = 128
N_HEADS = 8
HEAD_W = 128
CONV_W = 3
EPS = 1e-6
LN_EPS = 1e-5
COL_U, COL_V, COL_GA, COL_C, COL_B, COL_H, COL_GB, COL_MA, COL_MB = range(9)

V7X_VMEM_LIMIT_BYTES = 60000 * 1024
PROMPT_TILE = 512
CARRY_ROWS = 8
STAGE_ROWS, STAGE_COLS = 256, 1024
STAGE_SLOTS = 12


def _gelu(x):
    c = math.sqrt(2.0 / math.pi)
    return x * (0.5 * (1.0 + jnp.tanh(c * (x + 0.044715 * (x * x * x)))))


def _sigmoid(x):
    return 0.5 * jnp.tanh(0.5 * x) + 0.5


def _silu(x):
    return x * _sigmoid(x)


def _rms_norm(x, g):
    return x * lax.rsqrt(jnp.mean(x * x, axis=-1, keepdims=True) + EPS) * g


def _layer_norm(x, g, b):
    mu = jnp.mean(x, axis=-1, keepdims=True)
    xc = x - mu
    var = jnp.mean(xc * xc, axis=-1, keepdims=True)
    return xc * lax.rsqrt(var + LN_EPS) * g + b


def _dot(a, b):
    return jnp.dot(a, b, preferred_element_type=jnp.float32)


def _layer_tile(x, p, w, spatial_mix, conv_taps, weights_ready=lambda stage: None):
    d = x.shape[-1]
    bf = jnp.bfloat16
    h = _rms_norm(x, w["norm_g"][...]).astype(bf)
    weights_ready(0)

    def proj(col):
        return _dot(h, w["w_in"][:, col * d:(col + 1) * d])

    v = _layer_norm(_gelu(proj(COL_V)), w["ln_g"][...], w["ln_b"][...])
    y_a = _gelu(proj(COL_U)) * spatial_mix(v) * _silu(proj(COL_GA))

    weights_ready(1)
    u = proj(COL_C) * proj(COL_H)
    u1, u2 = conv_taps(u)
    cw = w["conv_w"]
    conv = cw[0:1, :] * u2 + cw[1:2, :] * u1 + cw[2:3, :] * u
    y_b = proj(COL_B) * conv * _silu(proj(COL_GB))

    weights_ready(2)
    merged = (_sigmoid(proj(COL_MA)) * _dot(y_a.astype(bf), w["w_a"][...])
              + _sigmoid(proj(COL_MB)) * _dot(y_b.astype(bf), w["w_b"][...]))
    x = x + _dot(merged.astype(bf), w["w_o"][...])
    gate = _sigmoid(_dot(_rms_norm(x, w["pe_g"][...]).astype(bf), w["w_pg"][...]))
    x = x + gate * _dot(p.astype(bf), w["w_pp"][...])
    return _rms_norm(x, w["fin_g"][...]), v, u


SMALL_NAMES = ("norm_g", "ln_g", "ln_b", "mix", "mix_b", "conv_w", "pe_g", "fin_g")
MATMUL_NAMES = ("w_in", "w_a", "w_b", "w_o", "w_pg", "w_pp")
N_SMALL, N_MM = len(SMALL_NAMES), len(MATMUL_NAMES)


def _weight_chunks(shapes):
    chunks = []
    for wi, (rows, cols) in enumerate(shapes):
        rb = min(rows, STAGE_ROWS)
        assert rows % rb == 0 and cols % STAGE_COLS == 0
        for c0 in range(0, cols, STAGE_COLS):
            for r0 in range(0, rows, rb):
                chunks.append((wi, r0, rb, c0))
    return chunks


def _export_copy(wi, vmem_refs, hbm_refs, sem):
    return pltpu.make_async_copy(vmem_refs[wi], hbm_refs[wi], sem.at[wi])


def _load_weights_as_bf16(f32_hbm, bf16_vmem, bf16_hbm, stage_ref, in_sem, out_sem):
    chunks = _weight_chunks([r.shape for r in f32_hbm])
    n_slots = stage_ref.shape[0]

    def fetch(k):
        wi, r0, rb, c0 = chunks[k]
        return pltpu.make_async_copy(
            f32_hbm[wi].at[pl.ds(r0, rb), pl.ds(c0, STAGE_COLS)],
            stage_ref.at[k % n_slots, pl.ds(0, rb), :], in_sem.at[k % n_slots])

    for k in range(min(n_slots, len(chunks))):
        fetch(k).start()
    for k, (wi, r0, rb, c0) in enumerate(chunks):
        fetch(k).wait()
        bf16_vmem[wi][pl.ds(r0, rb), pl.ds(c0, STAGE_COLS)] = (
            stage_ref[k % n_slots, 0:rb, :].astype(jnp.bfloat16))
        if k + n_slots < len(chunks):
            fetch(k + n_slots).start()
    for wi in range(len(f32_hbm)):
        _export_copy(wi, bf16_vmem, bf16_hbm, out_sem).start()


def _prompt_kernel(x_ref, p_ref, *rest):
    small = rest[:N_SMALL]
    mm_f32_hbm = rest[N_SMALL:N_SMALL + N_MM]
    outs = rest[N_SMALL + N_MM:]
    y_ref, conv_ref, vrow_ref = outs[:3]
    mm_bf16_hbm = outs[3:3 + N_MM]
    scratch = outs[3 + N_MM:]
    pbuf_ref = scratch[0]
    mm_bf16 = scratch[1:1 + N_MM]
    stage_ref, in_sem, out_sem = scratch[1 + N_MM:]
    w = dict(zip(SMALL_NAMES + MATMUL_NAMES, small + mm_bf16))
    tm, d = x_ref.shape[1], x_ref.shape[2]
    bf = jnp.bfloat16
    step = pl.program_id(0) * pl.num_programs(1) + pl.program_id(1)

    @pl.when(step == 0)
    def _():
        _load_weights_as_bf16(mm_f32_hbm, mm_bf16, mm_bf16_hbm, stage_ref, in_sem, out_sem)

    @pl.when(step == 1)
    def _():
        for wi in range(N_MM):
            _export_copy(wi, mm_bf16, mm_bf16_hbm, out_sem).wait()

    @pl.when(pl.program_id(1) == 0)
    def _():
        pbuf_ref[0:CARRY_ROWS, :] = jnp.zeros((CARRY_ROWS, d), jnp.float32)

    def spatial_mix(v):
        v_b = v.astype(bf)
        row = lax.broadcasted_iota(jnp.int32, (CHUNK, CHUNK), 0)
        col = lax.broadcasted_iota(jnp.int32, (CHUNK, CHUNK), 1)
        mix = [jnp.where(col <= row, w["mix"][hd], 0.0).astype(bf)
               for hd in range(N_HEADS)]
        s_rows = []
        for c in range(tm // CHUNK):
            s_heads = [_dot(mix[hd], v_b[c * CHUNK:(c + 1) * CHUNK,
                                         hd * HEAD_W:(hd + 1) * HEAD_W])
                       for hd in range(N_HEADS)]
            s_rows.append(jnp.concatenate(s_heads, axis=1) + w["mix_b"][...])
        return jnp.concatenate(s_rows, axis=0)

    def conv_taps(u):
        pbuf_ref[CARRY_ROWS:CARRY_ROWS + tm, :] = u
        return (pbuf_ref[CARRY_ROWS - 1:CARRY_ROWS - 1 + tm, :],
                pbuf_ref[CARRY_ROWS - 2:CARRY_ROWS - 2 + tm, :])

    y, v, _ = _layer_tile(x_ref[0], p_ref[0], w, spatial_mix, conv_taps)
    y_ref[0] = y
    vrow_ref[0, 0] = v[tm - CHUNK:, :]
    tail = pbuf_ref[tm:tm + CARRY_ROWS, :]
    pbuf_ref[0:CARRY_ROWS, :] = tail
    conv_ref[0, 0] = tail[CARRY_ROWS - (CONV_W - 1):, :]


def _sample_kernel(x_ref, p_ref, st_ref, *rest):
    small = rest[:N_SMALL]
    mm_hbm = rest[N_SMALL:N_SMALL + N_MM]
    y_ref, conv_ref, vrow_ref = rest[N_SMALL + N_MM:N_SMALL + N_MM + 3]
    mm_vmem = rest[N_SMALL + N_MM + 3:N_SMALL + 2 * N_MM + 3]
    sem = rest[-1]
    w = dict(zip(SMALL_NAMES + MATMUL_NAMES, small + mm_vmem))
    n_seq, n_t, d = x_ref.shape

    a_cols = (COL_GA + 1) * d
    assert sorted((COL_U, COL_V, COL_GA)) == [0, 1, 2]
    w_in_hbm, w_in_vmem = mm_hbm[0], mm_vmem[0]
    copies = (
        [pltpu.make_async_copy(w_in_hbm.at[:, pl.ds(0, a_cols)],
                               w_in_vmem.at[:, pl.ds(0, a_cols)], sem.at[0])],
        [pltpu.make_async_copy(w_in_hbm.at[:, pl.ds(a_cols, w_in_hbm.shape[1] - a_cols)],
                               w_in_vmem.at[:, pl.ds(a_cols, w_in_hbm.shape[1] - a_cols)],
                               sem.at[1])],
        [pltpu.make_async_copy(mm_hbm[i], mm_vmem[i], sem.at[1 + i])
         for i in range(1, N_MM)])
    for group in copies:
        for c in group:
            c.start()

    def weights_ready(stage):
        for c in copies[stage]:
            c.wait()

    def slab(a, t):
        return a[t * n_seq:(t + 1) * n_seq, :]

    def gather(ref):
        return jnp.concatenate([ref[:, t, :] for t in range(ref.shape[1])], axis=0)

    def scatter(ref, a):
        for t in range(ref.shape[1]):
            ref[:, t, :] = slab(a, t)

    def spatial_mix(v):
        out = []
        for t in range(n_t):
            acc = w["mix_b"][t:t + 1, :] + w["mix"][t * n_t:t * n_t + 1, :] * slab(v, 0)
            for j in range(1, t + 1):
                acc = acc + w["mix"][t * n_t + j:t * n_t + j + 1, :] * slab(v, j)
            out.append(acc)
        return jnp.concatenate(out, axis=0)

    def conv_taps(u):
        full = [st_ref[:, j, :] for j in range(CONV_W - 1)] + [
            slab(u, t) for t in range(n_t)]
        return (jnp.concatenate(full[1:1 + n_t], axis=0),
                jnp.concatenate(full[0:n_t], axis=0))

    y, v, u = _layer_tile(gather(x_ref), gather(p_ref), w, spatial_mix, conv_taps,
                          weights_ready)
    scatter(y_ref, y)
    scatter(vrow_ref, v)
    scatter(conv_ref, u[(n_t - (CONV_W - 1)) * n_seq:, :])


def _resident(shape):
    return pl.BlockSpec(shape, lambda *_: (0,) * len(shape),
                        pipeline_mode=pl.Buffered(1))


def kernel(x_prompt, x_sample, state_conv, p_prompt, p_sample, norm_g, w_in, ln_v_g,
           ln_v_b, w_s, b_s, conv_w, w_a_out, w_b_out, w_o, pe_norm_g, w_pe_gate,
           w_pe_proj, final_norm_g):
    assert w_in.shape[0] == 1, "single-layer step only"
    batch, seq, d = x_prompt.shape
    n_seq, n_t, _ = x_sample.shape
    p_dim = p_prompt.shape[-1]
    tm = PROMPT_TILE
    assert seq % tm == 0 and tm % CHUNK == 0 and d == N_HEADS * HEAD_W
    assert batch * (seq // tm) >= 2, "weight copies are drained in the second step"
    assert CONV_W - 1 <= n_t <= CHUNK
    f32, bf = jnp.float32, jnp.bfloat16

    row2 = lambda a: a.reshape(1, -1)
    bs_full = jnp.repeat(b_s[0].T, HEAD_W, axis=1)
    small = dict(norm_g=row2(norm_g[0]), ln_g=row2(ln_v_g[0]), ln_b=row2(ln_v_b[0]),
                 mix=w_s[0], mix_b=bs_full, conv_w=conv_w[0], pe_g=row2(pe_norm_g[0]),
                 fin_g=row2(final_norm_g))
    mm_f32 = dict(w_in=w_in[0], w_a=w_a_out[0], w_b=w_b_out[0], w_o=w_o[0],
                  w_pg=w_pe_gate[0], w_pp=w_pe_proj[0])

    def specs(arrs):
        return [_resident(a.shape) for a in arrs]

    prompt_small = tuple(small[k] for k in SMALL_NAMES)
    mm_in = tuple(mm_f32[k] for k in MATMUL_NAMES)
    hbm = pl.BlockSpec(memory_space=pltpu.HBM)
    outs = pl.pallas_call(
        _prompt_kernel,
        grid=(batch, seq // tm),
        in_specs=[pl.BlockSpec((1, tm, d), lambda b, i: (b, i, 0)),
                  pl.BlockSpec((1, tm, p_dim), lambda b, i: (b, i, 0))]
        + specs(prompt_small) + [hbm] * N_MM,
        out_specs=[pl.BlockSpec((1, tm, d), lambda b, i: (b, i, 0)),
                   pl.BlockSpec((1, 1, CONV_W - 1, d), lambda b, i: (0, b, 0, 0)),
                   pl.BlockSpec((1, 1, CHUNK, d), lambda b, i: (0, b, 0, 0))]
        + [hbm] * N_MM,
        out_shape=[jax.ShapeDtypeStruct((batch, seq, d), f32),
                   jax.ShapeDtypeStruct((1, batch, CONV_W - 1, d), f32),
                   jax.ShapeDtypeStruct((1, batch, CHUNK, d), f32)]
        + [jax.ShapeDtypeStruct(a.shape, bf) for a in mm_in],
        scratch_shapes=[pltpu.VMEM((tm + CARRY_ROWS, d), f32)]
        + [pltpu.VMEM(a.shape, bf) for a in mm_in]
        + [pltpu.VMEM((STAGE_SLOTS, STAGE_ROWS, STAGE_COLS), f32),
           pltpu.SemaphoreType.DMA((STAGE_SLOTS,)), pltpu.SemaphoreType.DMA((N_MM,))],
        compiler_params=pltpu.CompilerParams(
            dimension_semantics=("arbitrary", "arbitrary"),
            vmem_limit_bytes=V7X_VMEM_LIMIT_BYTES),
        name="prompt_layer",
    )(x_prompt, p_prompt[0], *prompt_small, *mm_in)
    y_p, conv_p, v_p = outs[:3]
    mm_bf16 = tuple(outs[3:])

    wmix = jnp.repeat(jnp.transpose(w_s[0][:, :n_t, :n_t], (1, 2, 0)), HEAD_W,
                      axis=2).reshape(n_t * n_t, d)
    sample_small = tuple(dict(small, mix=wmix, mix_b=bs_full[:n_t])[k]
                         for k in SMALL_NAMES)
    sample_in = (x_sample, p_sample[0], state_conv[0]) + sample_small
    out_shapes = ((n_seq, n_t, d), (n_seq, CONV_W - 1, d), (n_seq, n_t, d))
    y_s, conv_s, v_s = pl.pallas_call(
        _sample_kernel,
        grid=(1,),
        in_specs=specs(sample_in) + [hbm] * N_MM,
        out_specs=[_resident(s) for s in out_shapes],
        out_shape=[jax.ShapeDtypeStruct(s, f32) for s in out_shapes],
        scratch_shapes=[pltpu.VMEM(a.shape, bf) for a in mm_bf16]
        + [pltpu.SemaphoreType.DMA((N_MM + 1,))],
        compiler_params=pltpu.CompilerParams(
            dimension_semantics=("arbitrary",),
            vmem_limit_bytes=V7X_VMEM_LIMIT_BYTES),
        name="sample_layer",
    )(*sample_in, *mm_bf16)

    return (y_p, y_s, conv_p, conv_s[None], v_p, v_s[None])
```

```python
import math

import jax
import jax.numpy as jnp
from jax import lax
from jax.experimental import pallas as pl
from jax.experimental.pallas import tpu as pltpu

CHUNK = 128
N_HEADS = 8
HEAD_W = 128
CONV_W = 3
EPS = 1e-6
LN_EPS = 1e-5
COL_U, COL_V, COL_GA, COL_C, COL_B, COL_H, COL_GB, COL_MA, COL_MB = range(9)

V7X_VMEM_LIMIT_BYTES = 60000 * 1024
PROMPT_TILE = 512
CARRY_ROWS = 8
STAGE_ROWS, STAGE_COLS = 256, 1024
STAGE_SLOTS = 8


def _gelu(x):
    c = math.sqrt(2.0 / math.pi)
    return x * (0.5 * (1.0 + jnp.tanh(c * (x + 0.044715 * (x * x * x)))))


def _sigmoid(x):
    return 0.5 * jnp.tanh(0.5 * x) + 0.5


def _silu(x):
    return x * _sigmoid(x)


def _rms_norm(x, g):
    return x * lax.rsqrt(jnp.mean(x * x, axis=-1, keepdims=True) + EPS) * g


def _layer_norm(x, g, b):
    mu = jnp.mean(x, axis=-1, keepdims=True)
    xc = x - mu
    var = jnp.mean(xc * xc, axis=-1, keepdims=True)
    return xc * lax.rsqrt(var + LN_EPS) * g + b


def _dot(a, b):
    return jnp.dot(a, b, preferred_element_type=jnp.float32)


def _layer_tile(x, p, w, spatial_mix, conv_taps, weights_ready=lambda stage: None):
    d = x.shape[-1]
    bf = jnp.bfloat16
    h = _rms_norm(x, w["norm_g"][...]).astype(bf)
    weights_ready(0)

    def proj(col):
        return _dot(h, w["w_in"][:, col * d:(col + 1) * d])

    v = _layer_norm(_gelu(proj(COL_V)), w["ln_g"][...], w["ln_b"][...])
    y_a = _gelu(proj(COL_U)) * spatial_mix(v) * _silu(proj(COL_GA))

    weights_ready(1)
    u = proj(COL_C) * proj(COL_H)
    u1, u2 = conv_taps(u)
    cw = w["conv_w"]
    conv = cw[0, 0:1, :] * u2 + cw[0, 1:2, :] * u1 + cw[0, 2:3, :] * u
    y_b = proj(COL_B) * conv * _silu(proj(COL_GB))

    weights_ready(2)
    merged = (_sigmoid(proj(COL_MA)) * _dot(y_a.astype(bf), w["w_a"][...])
              + _sigmoid(proj(COL_MB)) * _dot(y_b.astype(bf), w["w_b"][...]))
    x = x + _dot(merged.astype(bf), w["w_o"][...])
    gate = _sigmoid(_dot(_rms_norm(x, w["pe_g"][...]).astype(bf), w["w_pg"][...]))
    x = x + gate * _dot(p.astype(bf), w["w_pp"][...])
    return _rms_norm(x, w["fin_g"][...]), v, u


SMALL_NAMES = ("norm_g", "ln_g", "ln_b", "mix", "mix_b", "conv_w", "pe_g", "fin_g")
MATMUL_NAMES = ("w_in", "w_a", "w_b", "w_o", "w_pg", "w_pp")
N_SMALL, N_MM = len(SMALL_NAMES), len(MATMUL_NAMES)


def _weight_chunks(shapes):
    chunks = []
    for wi, (rows, cols) in enumerate(shapes):
        rb = min(rows, STAGE_ROWS)
        assert rows % rb == 0 and cols % STAGE_COLS == 0
        for c0 in range(0, cols, STAGE_COLS):
            for r0 in range(0, rows, rb):
                chunks.append((wi, r0, rb, c0))
    return chunks


def _export_copy(wi, vmem_refs, hbm_refs, sem):
    return pltpu.make_async_copy(vmem_refs[wi], hbm_refs[wi], sem.at[wi])


def _load_weights_as_bf16(f32_hbm, bf16_vmem, bf16_hbm, stage_ref, in_sem, out_sem):
    chunks = _weight_chunks([r.shape for r in f32_hbm])
    n_slots = stage_ref.shape[0]

    def fetch(k):
        wi, r0, rb, c0 = chunks[k]
        return pltpu.make_async_copy(
            f32_hbm[wi].at[pl.ds(r0, rb), pl.ds(c0, STAGE_COLS)],
            stage_ref.at[k % n_slots, pl.ds(0, rb), :], in_sem.at[k % n_slots])

    for k in range(min(n_slots, len(chunks))):
        fetch(k).start()
    for k, (wi, r0, rb, c0) in enumerate(chunks):
        fetch(k).wait()
        bf16_vmem[wi][pl.ds(r0, rb), pl.ds(c0, STAGE_COLS)] = (
            stage_ref[k % n_slots, 0:rb, :].astype(jnp.bfloat16))
        if k + n_slots < len(chunks):
            fetch(k + n_slots).start()
    for wi in range(len(f32_hbm)):
        _export_copy(wi, bf16_vmem, bf16_hbm, out_sem).start()


def _prompt_kernel(x_ref, p_ref, *rest):
    small = rest[:N_SMALL]
    mm_f32_hbm = rest[N_SMALL:N_SMALL + N_MM]
    outs = rest[N_SMALL + N_MM:]
    y_ref, conv_ref, vrow_ref = outs[:3]
    mm_bf16_hbm = outs[3:3 + N_MM]
    scratch = outs[3 + N_MM:]
    pbuf_ref = scratch[0]
    mm_bf16 = scratch[1:1 + N_MM]
    stage_ref, in_sem, out_sem = scratch[1 + N_MM:]
    w = dict(zip(SMALL_NAMES + MATMUL_NAMES, small + mm_bf16))
    tm, d = x_ref.shape[1], x_ref.shape[2]
    bf = jnp.bfloat16
    step = pl.program_id(0) * pl.num_programs(1) + pl.program_id(1)

    @pl.when(step == 0)
    def _():
        _load_weights_as_bf16(mm_f32_hbm, mm_bf16, mm_bf16_hbm, stage_ref, in_sem, out_sem)

    @pl.when(step == 1)
    def _():
        for wi in range(N_MM):
            _export_copy(wi, mm_bf16, mm_bf16_hbm, out_sem).wait()

    @pl.when(pl.program_id(1) == 0)
    def _():
        pbuf_ref[0:CARRY_ROWS, :] = jnp.zeros((CARRY_ROWS, d), jnp.float32)

    def spatial_mix(v):
        v_b = v.astype(bf)
        row = lax.broadcasted_iota(jnp.int32, (CHUNK, CHUNK), 0)
        col = lax.broadcasted_iota(jnp.int32, (CHUNK, CHUNK), 1)
        mix = [jnp.where(col <= row, w["mix"][hd], 0.0).astype(bf)
               for hd in range(N_HEADS)]
        s_rows = []
        for c in range(tm // CHUNK):
            s_heads = [_dot(mix[hd], v_b[c * CHUNK:(c + 1) * CHUNK,
                                         hd * HEAD_W:(hd + 1) * HEAD_W])
                       for hd in range(N_HEADS)]
            s_rows.append(jnp.concatenate(s_heads, axis=1) + w["mix_b"][...])
        return jnp.concatenate(s_rows, axis=0)

    def conv_taps(u):
        pbuf_ref[CARRY_ROWS:CARRY_ROWS + tm, :] = u
        return (pbuf_ref[CARRY_ROWS - 1:CARRY_ROWS - 1 + tm, :],
                pbuf_ref[CARRY_ROWS - 2:CARRY_ROWS - 2 + tm, :])

    y, v, _ = _layer_tile(x_ref[0], p_ref[0], w, spatial_mix, conv_taps)
    y_ref[0] = y
    vrow_ref[0, 0] = v[tm - CHUNK:, :]
    tail = pbuf_ref[tm:tm + CARRY_ROWS, :]
    pbuf_ref[0:CARRY_ROWS, :] = tail
    conv_ref[0, 0] = tail[CARRY_ROWS - (CONV_W - 1):, :]


def _sample_kernel(x_ref, p_ref, st_ref, *rest):
    small = rest[:N_SMALL]
    mm_hbm = rest[N_SMALL:N_SMALL + N_MM]
    y_ref, conv_ref, vrow_ref = rest[N_SMALL + N_MM:N_SMALL + N_MM + 3]
    mm_vmem = rest[N_SMALL + N_MM + 3:N_SMALL + 2 * N_MM + 3]
    sem = rest[-1]
    w = dict(zip(SMALL_NAMES + MATMUL_NAMES, small + mm_vmem))
    n_seq, n_t, d = x_ref.shape

    a_cols = (COL_GA + 1) * d
    assert sorted((COL_U, COL_V, COL_GA)) == [0, 1, 2]
    w_in_hbm, w_in_vmem = mm_hbm[0], mm_vmem[0]
    copies = (
        [pltpu.make_async_copy(w_in_hbm.at[:, pl.ds(0, a_cols)],
                               w_in_vmem.at[:, pl.ds(0, a_cols)], sem.at[0])],
        [pltpu.make_async_copy(w_in_hbm.at[:, pl.ds(a_cols, w_in_hbm.shape[1] - a_cols)],
                               w_in_vmem.at[:, pl.ds(a_cols, w_in_hbm.shape[1] - a_cols)],
                               sem.at[1])],
        [pltpu.make_async_copy(mm_hbm[i], mm_vmem[i], sem.at[1 + i])
         for i in range(1, N_MM)])
    for group in copies:
        for c in group:
            c.start()

    def weights_ready(stage):
        for c in copies[stage]:
            c.wait()

    def slab(a, t):
        return a[t * n_seq:(t + 1) * n_seq, :]

    def gather(ref):
        return jnp.concatenate([ref[:, t, :] for t in range(ref.shape[1])], axis=0)

    def scatter(ref, a):
        for t in range(ref.shape[1]):
            ref[:, t, :] = slab(a, t)

    def spatial_mix(v):
        out = []
        for t in range(n_t):
            acc = w["mix_b"][t:t + 1, :] + w["mix"][t * n_t:t * n_t + 1, :] * slab(v, 0)
            for j in range(1, t + 1):
                acc = acc + w["mix"][t * n_t + j:t * n_t + j + 1, :] * slab(v, j)
            out.append(acc)
        return jnp.concatenate(out, axis=0)

    def conv_taps(u):
        full = [st_ref[:, j, :] for j in range(CONV_W - 1)] + [
            slab(u, t) for t in range(n_t)]
        return (jnp.concatenate(full[1:1 + n_t], axis=0),
                jnp.concatenate(full[0:n_t], axis=0))

    y, v, u = _layer_tile(gather(x_ref), gather(p_ref), w, spatial_mix, conv_taps,
                          weights_ready)
    scatter(y_ref, y)
    scatter(vrow_ref, v)
    scatter(conv_ref, u[(n_t - (CONV_W - 1)) * n_seq:, :])


def _resident(shape):
    return pl.BlockSpec(shape, lambda *_: (0,) * len(shape),
                        pipeline_mode=pl.Buffered(1))


def kernel(x_prompt, x_sample, state_conv, p_prompt, p_sample, norm_g, w_in, ln_v_g,
           ln_v_b, w_s, b_s, conv_w, w_a_out, w_b_out, w_o, pe_norm_g, w_pe_gate,
           w_pe_proj, final_norm_g):
    assert w_in.shape[0] == 1, "single-layer step only"
    batch, seq, d = x_prompt.shape
    n_seq, n_t, _ = x_sample.shape
    p_dim = p_prompt.shape[-1]
    tm = PROMPT_TILE
    assert seq % tm == 0 and tm % CHUNK == 0 and d == N_HEADS * HEAD_W
    assert batch * (seq // tm) >= 2, "weight copies are drained in the second step"
    assert CONV_W - 1 <= n_t <= CHUNK
    f32, bf = jnp.float32, jnp.bfloat16

    row2 = lambda a: a.reshape(1, -1)
    bs_full = jnp.repeat(b_s[0].T, HEAD_W, axis=1)
    small = dict(norm_g=row2(norm_g[0]), ln_g=row2(ln_v_g[0]), ln_b=row2(ln_v_b[0]),
                 mix=w_s[0], mix_b=bs_full, conv_w=conv_w, pe_g=row2(pe_norm_g[0]),
                 fin_g=row2(final_norm_g))
    mm_f32 = dict(w_in=w_in[0], w_a=w_a_out[0], w_b=w_b_out[0], w_o=w_o[0],
                  w_pg=w_pe_gate[0], w_pp=w_pe_proj[0])

    def specs(arrs):
        return [_resident(a.shape) for a in arrs]

    prompt_small = tuple(small[k] for k in SMALL_NAMES)
    mm_in = tuple(mm_f32[k] for k in MATMUL_NAMES)
    hbm = pl.BlockSpec(memory_space=pltpu.HBM)
    outs = pl.pallas_call(
        _prompt_kernel,
        grid=(batch, seq // tm),
        in_specs=[pl.BlockSpec((1, tm, d), lambda b, i: (b, i, 0)),
                  pl.BlockSpec((1, tm, p_dim), lambda b, i: (b, i, 0))]
        + specs(prompt_small) + [hbm] * N_MM,
        out_specs=[pl.BlockSpec((1, tm, d), lambda b, i: (b, i, 0)),
                   pl.BlockSpec((1, 1, CONV_W - 1, d), lambda b, i: (0, b, 0, 0)),
                   pl.BlockSpec((1, 1, CHUNK, d), lambda b, i: (0, b, 0, 0))]
        + [hbm] * N_MM,
        out_shape=[jax.ShapeDtypeStruct((batch, seq, d), f32),
                   jax.ShapeDtypeStruct((1, batch, CONV_W - 1, d), f32),
                   jax.ShapeDtypeStruct((1, batch, CHUNK, d), f32)]
        + [jax.ShapeDtypeStruct(a.shape, bf) for a in mm_in],
        scratch_shapes=[pltpu.VMEM((tm + CARRY_ROWS, d), f32)]
        + [pltpu.VMEM(a.shape, bf) for a in mm_in]
        + [pltpu.VMEM((STAGE_SLOTS, STAGE_ROWS, STAGE_COLS), f32),
           pltpu.SemaphoreType.DMA((STAGE_SLOTS,)), pltpu.SemaphoreType.DMA((N_MM,))],
        compiler_params=pltpu.CompilerParams(
            dimension_semantics=("arbitrary", "arbitrary"),
            vmem_limit_bytes=V7X_VMEM_LIMIT_BYTES),
        name="prompt_layer",
    )(x_prompt, p_prompt[0], *prompt_small, *mm_in)
    y_p, conv_p, v_p = outs[:3]
    mm_bf16 = tuple(outs[3:])

    wmix = jnp.repeat(jnp.transpose(w_s[0][:, :n_t, :n_t], (1, 2, 0)), HEAD_W,
                      axis=2).reshape(n_t * n_t, d)
    sample_small = tuple(dict(small, mix=wmix)[k] for k in SMALL_NAMES)
    sample_in = (x_sample, p_sample[0], state_conv[0]) + sample_small
    out_shapes = ((n_seq, n_t, d), (n_seq, CONV_W - 1, d), (n_seq, n_t, d))
    y_s, conv_s, v_s = pl.pallas_call(
        _sample_kernel,
        grid=(1,),
        in_specs=specs(sample_in) + [hbm] * N_MM,
        out_specs=[_resident(s) for s in out_shapes],
        out_shape=[jax.ShapeDtypeStruct(s, f32) for s in out_shapes],
        scratch_shapes=[pltpu.VMEM(a.shape, bf) for a in mm_bf16]
        + [pltpu.SemaphoreType.DMA((N_MM + 1,))],
        compiler_params=pltpu.CompilerParams(
            dimension_semantics=("arbitrary",),
            vmem_limit_bytes=V7X_VMEM_LIMIT_BYTES),
        name="sample_layer",
    )(*sample_in, *mm_bf16)

    return (y_p, y_s, conv_p, conv_s[None], v_p, v_s[None])
```

```python
import math

import jax
import jax.numpy as jnp
from jax import lax
from jax.experimental import pallas as pl
from jax.experimental.pallas import tpu as pltpu

CHUNK = 128
N_HEADS = 8
HEAD_W = 128
CONV_W = 3
EPS = 1e-6
LN_EPS = 1e-5
COL_U, COL_V, COL_GA, COL_C, COL_B, COL_H, COL_GB, COL_MA, COL_MB = range(9)

V7X_VMEM_LIMIT_BYTES = 60000 * 1024
PROMPT_TILE = 512
CARRY_ROWS = 8
STAGE_ROWS, STAGE_COLS = 256, 1024
STAGE_SLOTS = 8


def _gelu(x):
    c = math.sqrt(2.0 / math.pi)
    return x * (0.5 * (1.0 + jnp.tanh(c * (x + 0.044715 * (x * x * x)))))


def _sigmoid(x):
    return 0.5 * jnp.tanh(0.5 * x) + 0.5


def _silu(x):
    return x * _sigmoid(x)


def _rms_norm(x, g):
    return x * lax.rsqrt(jnp.mean(x * x, axis=-1, keepdims=True) + EPS) * g


def _layer_norm(x, g, b):
    mu = jnp.mean(x, axis=-1, keepdims=True)
    xc = x - mu
    var = jnp.mean(xc * xc, axis=-1, keepdims=True)
    return xc * lax.rsqrt(var + LN_EPS) * g + b


def _dot(a, b):
    return jnp.dot(a, b, preferred_element_type=jnp.float32)


def _layer_tile(x, p, w, spatial_mix, conv_taps, weights_ready=lambda stage: None):
    d = x.shape[-1]
    bf = jnp.bfloat16
    h = _rms_norm(x, w["norm_g"][...]).astype(bf)
    weights_ready(0)

    def proj(col):
        return _dot(h, w["w_in"][:, col * d:(col + 1) * d])

    v = _layer_norm(_gelu(proj(COL_V)), w["ln_g"][...], w["ln_b"][...])
    y_a = _gelu(proj(COL_U)) * spatial_mix(v) * _silu(proj(COL_GA))

    weights_ready(1)
    u = proj(COL_C) * proj(COL_H)
    u1, u2 = conv_taps(u)
    cw = w["conv_w"]
    conv = cw[0] * u2 + cw[1] * u1 + cw[2] * u
    y_b = proj(COL_B) * conv * _silu(proj(COL_GB))

    weights_ready(2)
    merged = (_sigmoid(proj(COL_MA)) * _dot(y_a.astype(bf), w["w_a"][...])
              + _sigmoid(proj(COL_MB)) * _dot(y_b.astype(bf), w["w_b"][...]))
    x = x + _dot(merged.astype(bf), w["w_o"][...])
    gate = _sigmoid(_dot(_rms_norm(x, w["pe_g"][...]).astype(bf), w["w_pg"][...]))
    x = x + gate * _dot(p.astype(bf), w["w_pp"][...])
    return _rms_norm(x, w["fin_g"][...]), v, u


SMALL_NAMES = ("norm_g", "ln_g", "ln_b", "mix", "mix_b", "conv_w", "pe_g", "fin_g")
MATMUL_NAMES = ("w_in", "w_a", "w_b", "w_o", "w_pg", "w_pp")
N_SMALL, N_MM = len(SMALL_NAMES), len(MATMUL_NAMES)


def _weight_chunks(shapes):
    chunks = []
    for wi, (rows, cols) in enumerate(shapes):
        rb = min(rows, STAGE_ROWS)
        assert rows % rb == 0 and cols % STAGE_COLS == 0
        for c0 in range(0, cols, STAGE_COLS):
            for r0 in range(0, rows, rb):
                chunks.append((wi, r0, rb, c0))
    return chunks


def _export_copy(wi, vmem_refs, hbm_refs, sem):
    return pltpu.make_async_copy(vmem_refs[wi], hbm_refs[wi], sem.at[wi])


def _load_weights_as_bf16(f32_hbm, bf16_vmem, bf16_hbm, stage_ref, in_sem, out_sem):
    chunks = _weight_chunks([r.shape for r in f32_hbm])
    n_slots = stage_ref.shape[0]

    def fetch(k):
        wi, r0, rb, c0 = chunks[k]
        return pltpu.make_async_copy(
            f32_hbm[wi].at[pl.ds(r0, rb), pl.ds(c0, STAGE_COLS)],
            stage_ref.at[k % n_slots, pl.ds(0, rb), :], in_sem.at[k % n_slots])

    for k in range(min(n_slots, len(chunks))):
        fetch(k).start()
    for k, (wi, r0, rb, c0) in enumerate(chunks):
        fetch(k).wait()
        bf16_vmem[wi][pl.ds(r0, rb), pl.ds(c0, STAGE_COLS)] = (
            stage_ref[k % n_slots, 0:rb, :].astype(jnp.bfloat16))
        if k + n_slots < len(chunks):
            fetch(k + n_slots).start()
    for wi in range(len(f32_hbm)):
        _export_copy(wi, bf16_vmem, bf16_hbm, out_sem).start()


def _prompt_kernel(x_ref, p_ref, *rest):
    small = rest[:N_SMALL]
    mm_f32_hbm = rest[N_SMALL:N_SMALL + N_MM]
    outs = rest[N_SMALL + N_MM:]
    y_ref, conv_ref, vrow_ref = outs[:3]
    mm_bf16_hbm = outs[3:3 + N_MM]
    scratch = outs[3 + N_MM:]
    pbuf_ref = scratch[0]
    mm_bf16 = scratch[1:1 + N_MM]
    stage_ref, in_sem, out_sem = scratch[1 + N_MM:]
    w = dict(zip(SMALL_NAMES + MATMUL_NAMES, small + mm_bf16))
    tm, d = x_ref.shape[1], x_ref.shape[2]
    bf = jnp.bfloat16
    step = pl.program_id(0) * pl.num_programs(1) + pl.program_id(1)

    @pl.when(step == 0)
    def _():
        _load_weights_as_bf16(mm_f32_hbm, mm_bf16, mm_bf16_hbm, stage_ref, in_sem, out_sem)

    @pl.when(step == 1)
    def _():
        for wi in range(N_MM):
            _export_copy(wi, mm_bf16, mm_bf16_hbm, out_sem).wait()

    @pl.when(pl.program_id(1) == 0)
    def _():
        pbuf_ref[0:CARRY_ROWS, :] = jnp.zeros((CARRY_ROWS, d), jnp.float32)

    def spatial_mix(v):
        v_b = v.astype(bf)
        row = lax.broadcasted_iota(jnp.int32, (CHUNK, CHUNK), 0)
        col = lax.broadcasted_iota(jnp.int32, (CHUNK, CHUNK), 1)
        mix = [jnp.where(col <= row, w["mix"][hd], 0.0).astype(bf)
               for hd in range(N_HEADS)]
        s_rows = []
        for c in range(tm // CHUNK):
            s_heads = [_dot(mix[hd], v_b[c * CHUNK:(c + 1) * CHUNK,
                                         hd * HEAD_W:(hd + 1) * HEAD_W])
                       for hd in range(N_HEADS)]
            s_rows.append(jnp.concatenate(s_heads, axis=1) + w["mix_b"][...])
        return jnp.concatenate(s_rows, axis=0)

    def conv_taps(u):
        pbuf_ref[CARRY_ROWS:CARRY_ROWS + tm, :] = u
        return (pbuf_ref[CARRY_ROWS - 1:CARRY_ROWS - 1 + tm, :],
                pbuf_ref[CARRY_ROWS - 2:CARRY_ROWS - 2 + tm, :])

    y, v, _ = _layer_tile(x_ref[0], p_ref[0], w, spatial_mix, conv_taps)
    y_ref[0] = y
    vrow_ref[0, 0] = v[tm - CHUNK:, :]
    tail = pbuf_ref[tm:tm + CARRY_ROWS, :]
    pbuf_ref[0:CARRY_ROWS, :] = tail
    conv_ref[0, 0] = tail[CARRY_ROWS - (CONV_W - 1):, :]


def _sample_kernel(x_ref, p_ref, st_ref, *rest):
    small = rest[:N_SMALL]
    mm_hbm = rest[N_SMALL:N_SMALL + N_MM]
    y_ref, conv_ref, vrow_ref = rest[N_SMALL + N_MM:N_SMALL + N_MM + 3]
    mm_vmem = rest[N_SMALL + N_MM + 3:N_SMALL + 2 * N_MM + 3]
    sem = rest[-1]
    w = dict(zip(SMALL_NAMES + MATMUL_NAMES, small + mm_vmem))
    n_seq, n_t, d = x_ref.shape

    a_cols = (COL_GA + 1) * d
    assert sorted((COL_U, COL_V, COL_GA)) == [0, 1, 2]
    w_in_hbm, w_in_vmem = mm_hbm[0], mm_vmem[0]
    copies = (
        [pltpu.make_async_copy(w_in_hbm.at[:, pl.ds(0, a_cols)],
                               w_in_vmem.at[:, pl.ds(0, a_cols)], sem.at[0])],
        [pltpu.make_async_copy(w_in_hbm.at[:, pl.ds(a_cols, w_in_hbm.shape[1] - a_cols)],
                               w_in_vmem.at[:, pl.ds(a_cols, w_in_hbm.shape[1] - a_cols)],
                               sem.at[1])],
        [pltpu.make_async_copy(mm_hbm[i], mm_vmem[i], sem.at[1 + i])
         for i in range(1, N_MM)])
    for group in copies:
        for c in group:
            c.start()

    def weights_ready(stage):
        for c in copies[stage]:
            c.wait()

    def slab(a, t):
        return a[t * n_seq:(t + 1) * n_seq, :]

    def gather(ref):
        return jnp.concatenate([ref[:, t, :] for t in range(ref.shape[1])], axis=0)

    def scatter(ref, a):
        for t in range(ref.shape[1]):
            ref[:, t, :] = slab(a, t)

    def spatial_mix(v):
        out = []
        for t in range(n_t):
            acc = w["mix_b"][t:t + 1, :] + w["mix"][t * n_t:t * n_t + 1, :] * slab(v, 0)
            for j in range(1, t + 1):
                acc = acc + w["mix"][t * n_t + j:t * n_t + j + 1, :] * slab(v, j)
            out.append(acc)
        return jnp.concatenate(out, axis=0)

    def conv_taps(u):
        full = [st_ref[:, j, :] for j in range(CONV_W - 1)] + [
            slab(u, t) for t in range(n_t)]
        return (jnp.concatenate(full[1:1 + n_t], axis=0),
                jnp.concatenate(full[0:n_t], axis=0))

    y, v, u = _layer_tile(gather(x_ref), gather(p_ref), w, spatial_mix, conv_taps,
                          weights_ready)
    scatter(y_ref, y)
    scatter(vrow_ref, v)
    scatter(conv_ref, u[(n_t - (CONV_W - 1)) * n_seq:, :])


def _resident(shape):
    return pl.BlockSpec(shape, lambda *_: (0,) * len(shape),
                        pipeline_mode=pl.Buffered(1))


def kernel(x_prompt, x_sample, state_conv, p_prompt, p_sample, norm_g, w_in, ln_v_g,
           ln_v_b, w_s, b_s, conv_w, w_a_out, w_b_out, w_o, pe_norm_g, w_pe_gate,
           w_pe_proj, final_norm_g):
    assert w_in.shape[0] == 1, "single-layer step only"
    batch, seq, d = x_prompt.shape
    n_seq, n_t, _ = x_sample.shape
    p_dim = p_prompt.shape[-1]
    tm = PROMPT_TILE
    assert seq % tm == 0 and tm % CHUNK == 0 and d == N_HEADS * HEAD_W
    assert batch * (seq // tm) >= 2, "weight copies are drained in the second step"
    assert CONV_W - 1 <= n_t <= CHUNK
    f32, bf = jnp.float32, jnp.bfloat16

    row2 = lambda a: a.reshape(1, -1)
    bs_full = jnp.repeat(b_s[0].T, HEAD_W, axis=1)
    small = dict(norm_g=row2(norm_g[0]), ln_g=row2(ln_v_g[0]), ln_b=row2(ln_v_b[0]),
                 mix=w_s[0], mix_b=bs_full, conv_w=jnp.transpose(conv_w, (1, 0, 2)),
                 pe_g=row2(pe_norm_g[0]),
                 fin_g=row2(final_norm_g))
    mm_f32 = dict(w_in=w_in[0], w_a=w_a_out[0], w_b=w_b_out[0], w_o=w_o[0],
                  w_pg=w_pe_gate[0], w_pp=w_pe_proj[0])

    def specs(arrs):
        return [_resident(a.shape) for a in arrs]

    prompt_small = tuple(small[k] for k in SMALL_NAMES)
    mm_in = tuple(mm_f32[k] for k in MATMUL_NAMES)
    hbm = pl.BlockSpec(memory_space=pltpu.HBM)
    outs = pl.pallas_call(
        _prompt_kernel,
        grid=(batch, seq // tm),
        in_specs=[pl.BlockSpec((1, tm, d), lambda b, i: (b, i, 0)),
                  pl.BlockSpec((1, tm, p_dim), lambda b, i: (b, i, 0))]
        + specs(prompt_small) + [hbm] * N_MM,
        out_specs=[pl.BlockSpec((1, tm, d), lambda b, i: (b, i, 0)),
                   pl.BlockSpec((1, 1, CONV_W - 1, d), lambda b, i: (0, b, 0, 0)),
                   pl.BlockSpec((1, 1, CHUNK, d), lambda b, i: (0, b, 0, 0))]
        + [hbm] * N_MM,
        out_shape=[jax.ShapeDtypeStruct((batch, seq, d), f32),
                   jax.ShapeDtypeStruct((1, batch, CONV_W - 1, d), f32),
                   jax.ShapeDtypeStruct((1, batch, CHUNK, d), f32)]
        + [jax.ShapeDtypeStruct(a.shape, bf) for a in mm_in],
        scratch_shapes=[pltpu.VMEM((tm + CARRY_ROWS, d), f32)]
        + [pltpu.VMEM(a.shape, bf) for a in mm_in]
        + [pltpu.VMEM((STAGE_SLOTS, STAGE_ROWS, STAGE_COLS), f32),
           pltpu.SemaphoreType.DMA((STAGE_SLOTS,)), pltpu.SemaphoreType.DMA((N_MM,))],
        compiler_params=pltpu.CompilerParams(
            dimension_semantics=("arbitrary", "arbitrary"),
            vmem_limit_bytes=V7X_VMEM_LIMIT_BYTES),
        name="prompt_layer",
    )(x_prompt, p_prompt[0], *prompt_small, *mm_in)
    y_p, conv_p, v_p = outs[:3]
    mm_bf16 = tuple(outs[3:])

    wmix = jnp.repeat(jnp.transpose(w_s[0][:, :n_t, :n_t], (1, 2, 0)), HEAD_W,
                      axis=2).reshape(n_t * n_t, d)
    sample_small = tuple(dict(small, mix=wmix)[k] for k in SMALL_NAMES)
    sample_in = (x_sample, p_sample[0], state_conv[0]) + sample_small
    out_shapes = ((n_seq, n_t, d), (n_seq, CONV_W - 1, d), (n_seq, n_t, d))
    y_s, conv_s, v_s = pl.pallas_call(
        _sample_kernel,
        grid=(1,),
        in_specs=specs(sample_in) + [hbm] * N_MM,
        out_specs=[_resident(s) for s in out_shapes],
        out_shape=[jax.ShapeDtypeStruct(s, f32) for s in out_shapes],
        scratch_shapes=[pltpu.VMEM(a.shape, bf) for a in mm_bf16]
        + [pltpu.SemaphoreType.DMA((N_MM + 1,))],
        compiler_params=pltpu.CompilerParams(
            dimension_semantics=("arbitrary",),
            vmem_limit_bytes=V7X_VMEM_LIMIT_BYTES),
        name="sample_layer",
    )(*sample_in, *mm_bf16)

    return (y_p, y_s, conv_p, conv_s[None], v_p, v_s[None])
```

```python
import math

import jax
import jax.numpy as jnp
from jax import lax
from jax.experimental import pallas as pl
from jax.experimental.pallas import tpu as pltpu

CHUNK = 128
N_HEADS = 8
HEAD_W = 128
CONV_W = 3
EPS = 1e-6
LN_EPS = 1e-5
COL_U, COL_V, COL_GA, COL_C, COL_B, COL_H, COL_GB, COL_MA, COL_MB = range(9)

V7X_VMEM_LIMIT_BYTES = 60000 * 1024
PROMPT_TILE = 512
CARRY_ROWS = 8
STAGE_ROWS, STAGE_COLS = 256, 1024
STAGE_SLOTS = 8


def _gelu(x):
    c = math.sqrt(2.0 / math.pi)
    return x * (0.5 * (1.0 + jnp.tanh(c * (x + 0.044715 * (x * x * x)))))


def _sigmoid(x):
    return 0.5 * jnp.tanh(0.5 * x) + 0.5


def _silu(x):
    return x * _sigmoid(x)


def _rms_norm(x, g):
    return x * lax.rsqrt(jnp.mean(x * x, axis=-1, keepdims=True) + EPS) * g


def _layer_norm(x, g, b):
    mu = jnp.mean(x, axis=-1, keepdims=True)
    xc = x - mu
    var = jnp.mean(xc * xc, axis=-1, keepdims=True)
    return xc * lax.rsqrt(var + LN_EPS) * g + b


def _dot(a, b):
    return jnp.dot(a, b, preferred_element_type=jnp.float32)


SMALL_NAMES = ("norm_g", "ln_g", "ln_b", "mix", "mix_b", "conv_w", "pe_g", "fin_g")
MATMUL_NAMES = ("w_in", "w_a", "w_b", "w_o", "w_pg", "w_pp")
N_SMALL, N_MM = len(SMALL_NAMES), len(MATMUL_NAMES)
WEIGHT_USE_ORDER = (
    ("w_in", COL_V), ("w_in", COL_U), ("w_in", COL_GA), ("w_in", COL_C), ("w_in", COL_H),
    ("w_in", COL_B), ("w_in", COL_GB), ("w_in", COL_MA), "w_a", ("w_in", COL_MB), "w_b",
    "w_o", "w_pg", "w_pp")


def _layer_tile(x, p, w, spatial_mix, conv_taps, weights_ready=lambda key: None):
    d = x.shape[-1]
    bf = jnp.bfloat16
    h = _rms_norm(x, w["norm_g"][...]).astype(bf)

    def proj(col):
        weights_ready(("w_in", col))
        return _dot(h, w["w_in"][:, col * d:(col + 1) * d])

    def mm(a, name):
        weights_ready(name)
        return _dot(a.astype(bf), w[name][...])

    v = _layer_norm(_gelu(proj(COL_V)), w["ln_g"][...], w["ln_b"][...])
    y_a = _gelu(proj(COL_U)) * spatial_mix(v) * _silu(proj(COL_GA))

    u = proj(COL_C) * proj(COL_H)
    u1, u2 = conv_taps(u)
    cw = w["conv_w"]
    conv = cw[0] * u2 + cw[1] * u1 + cw[2] * u
    y_b = proj(COL_B) * conv * _silu(proj(COL_GB))

    merged = (_sigmoid(proj(COL_MA)) * mm(y_a, "w_a")
              + _sigmoid(proj(COL_MB)) * mm(y_b, "w_b"))
    x = x + mm(merged, "w_o")
    gate = _sigmoid(mm(_rms_norm(x, w["pe_g"][...]), "w_pg"))
    x = x + gate * mm(p, "w_pp")
    return _rms_norm(x, w["fin_g"][...]), v, u


def _export_copy(wi, vmem_refs, hbm_refs, sem):
    return pltpu.make_async_copy(vmem_refs[wi], hbm_refs[wi], sem.at[wi])


def _weight_stream(f32_hbm, bf16_vmem, bf16_hbm, stage_ref, in_sem, out_sem):
    index = dict(zip(MATMUL_NAMES, range(N_MM)))
    chunks, block_end = [], []
    for key in WEIGHT_USE_ORDER:
        name, col = key if isinstance(key, tuple) else (key, None)
        rows, cols = f32_hbm[index[name]].shape
        rb = min(rows, STAGE_ROWS)
        n_blocks = len([k for k in WEIGHT_USE_ORDER if isinstance(k, tuple) and k[0] == name])
        width = cols if col is None else cols // n_blocks
        assert rows % rb == 0 and width % STAGE_COLS == 0
        c_lo = 0 if col is None else col * width
        for c0 in range(c_lo, c_lo + width, STAGE_COLS):
            for r0 in range(0, rows, rb):
                chunks.append((index[name], r0, rb, c0))
        block_end.append(len(chunks))
    last_block = {index[k[0] if isinstance(k, tuple) else k]: i
                  for i, k in enumerate(WEIGHT_USE_ORDER)}
    n_slots = stage_ref.shape[0]
    visited = []

    def fetch(k):
        wi, r0, rb, c0 = chunks[k]
        return pltpu.make_async_copy(
            f32_hbm[wi].at[pl.ds(r0, rb), pl.ds(c0, STAGE_COLS)],
            stage_ref.at[k % n_slots, pl.ds(0, rb), :], in_sem.at[k % n_slots])

    for k in range(min(n_slots, len(chunks))):
        fetch(k).start()

    def weights_ready(key):
        i = len(visited)
        assert key == WEIGHT_USE_ORDER[i], (key, WEIGHT_USE_ORDER[i])
        visited.append(key)
        for k in range(block_end[i - 1] if i else 0, block_end[i]):
            wi, r0, rb, c0 = chunks[k]
            fetch(k).wait()
            bf16_vmem[wi][pl.ds(r0, rb), pl.ds(c0, STAGE_COLS)] = (
                stage_ref[k % n_slots, 0:rb, :].astype(jnp.bfloat16))
            if k + n_slots < len(chunks):
                fetch(k + n_slots).start()
        for wi, last in last_block.items():
            if last == i:
                _export_copy(wi, bf16_vmem, bf16_hbm, out_sem).start()

    return weights_ready, visited


def _prompt_kernel(x_ref, p_ref, *rest):
    w = dict(zip(SMALL_NAMES + MATMUL_NAMES, rest[:N_SMALL + N_MM]))
    y_ref, conv_ref, vrow_ref, pbuf_ref = rest[N_SMALL + N_MM:]
    tm, d = x_ref.shape[1], x_ref.shape[2]
    bf = jnp.bfloat16

    @pl.when(pl.program_id(1) == 0)
    def _():
        pbuf_ref[0:CARRY_ROWS, :] = jnp.zeros((CARRY_ROWS, d), jnp.float32)

    def spatial_mix(v):
        v_b = v.astype(bf)
        row = lax.broadcasted_iota(jnp.int32, (CHUNK, CHUNK), 0)
        col = lax.broadcasted_iota(jnp.int32, (CHUNK, CHUNK), 1)
        mix = [jnp.where(col <= row, w["mix"][hd], 0.0).astype(bf)
               for hd in range(N_HEADS)]
        s_rows = []
        for c in range(tm // CHUNK):
            s_heads = [_dot(mix[hd], v_b[c * CHUNK:(c + 1) * CHUNK,
                                         hd * HEAD_W:(hd + 1) * HEAD_W])
                       for hd in range(N_HEADS)]
            s_rows.append(jnp.concatenate(s_heads, axis=1) + w["mix_b"][...])
        return jnp.concatenate(s_rows, axis=0)

    def conv_taps(u):
        pbuf_ref[CARRY_ROWS:CARRY_ROWS + tm, :] = u
        return (pbuf_ref[CARRY_ROWS - 1:CARRY_ROWS - 1 + tm, :],
                pbuf_ref[CARRY_ROWS - 2:CARRY_ROWS - 2 + tm, :])

    y, v, _ = _layer_tile(x_ref[0], p_ref[0], w, spatial_mix, conv_taps)
    y_ref[0] = y
    vrow_ref[0, 0] = v[tm - CHUNK:, :]
    tail = pbuf_ref[tm:tm + CARRY_ROWS, :]
    pbuf_ref[0:CARRY_ROWS, :] = tail
    conv_ref[0, 0] = tail[CARRY_ROWS - (CONV_W - 1):, :]


def _sample_kernel(x_ref, p_ref, st_ref, *rest):
    small = rest[:N_SMALL]
    mm_f32_hbm = rest[N_SMALL:N_SMALL + N_MM]
    outs = rest[N_SMALL + N_MM:]
    y_ref, conv_ref, vrow_ref = outs[:3]
    mm_bf16_hbm = outs[3:3 + N_MM]
    mm_bf16 = outs[3 + N_MM:3 + 2 * N_MM]
    stage_ref, in_sem, out_sem = outs[3 + 2 * N_MM:]
    w = dict(zip(SMALL_NAMES + MATMUL_NAMES, small + mm_bf16))
    n_seq, n_t, _ = x_ref.shape

    def slab(a, t):
        return a[t * n_seq:(t + 1) * n_seq, :]

    def gather(ref):
        return jnp.concatenate([ref[:, t, :] for t in range(ref.shape[1])], axis=0)

    def scatter(ref, a):
        for t in range(ref.shape[1]):
            ref[:, t, :] = slab(a, t)

    def spatial_mix(v):
        out = []
        for t in range(n_t):
            acc = w["mix_b"][t:t + 1, :] + w["mix"][t * n_t:t * n_t + 1, :] * slab(v, 0)
            for j in range(1, t + 1):
                acc = acc + w["mix"][t * n_t + j:t * n_t + j + 1, :] * slab(v, j)
            out.append(acc)
        return jnp.concatenate(out, axis=0)

    def conv_taps(u):
        full = [st_ref[:, j, :] for j in range(CONV_W - 1)] + [
            slab(u, t) for t in range(n_t)]
        return (jnp.concatenate(full[1:1 + n_t], axis=0),
                jnp.concatenate(full[0:n_t], axis=0))

    weights_ready, visited = _weight_stream(mm_f32_hbm, mm_bf16, mm_bf16_hbm, stage_ref,
                                            in_sem, out_sem)
    y, v, u = _layer_tile(gather(x_ref), gather(p_ref), w, spatial_mix, conv_taps,
                          weights_ready)
    assert len(visited) == len(WEIGHT_USE_ORDER)
    scatter(y_ref, y)
    scatter(vrow_ref, v)
    scatter(conv_ref, u[(n_t - (CONV_W - 1)) * n_seq:, :])
    for wi in range(N_MM):
        _export_copy(wi, mm_bf16, mm_bf16_hbm, out_sem).wait()


def _resident(shape):
    return pl.BlockSpec(shape, lambda *_: (0,) * len(shape),
                        pipeline_mode=pl.Buffered(1))


def kernel(x_prompt, x_sample, state_conv, p_prompt, p_sample, norm_g, w_in, ln_v_g,
           ln_v_b, w_s, b_s, conv_w, w_a_out, w_b_out, w_o, pe_norm_g, w_pe_gate,
           w_pe_proj, final_norm_g):
    assert w_in.shape[0] == 1, "single-layer step only"
    batch, seq, d = x_prompt.shape
    n_seq, n_t, _ = x_sample.shape
    p_dim = p_prompt.shape[-1]
    tm = PROMPT_TILE
    assert seq % tm == 0 and tm % CHUNK == 0 and d == N_HEADS * HEAD_W
    assert CONV_W - 1 <= n_t <= CHUNK
    f32, bf = jnp.float32, jnp.bfloat16

    row2 = lambda a: a.reshape(1, -1)
    bs_full = jnp.repeat(b_s[0].T, HEAD_W, axis=1)
    small = dict(norm_g=row2(norm_g[0]), ln_g=row2(ln_v_g[0]), ln_b=row2(ln_v_b[0]),
                 mix=w_s[0], mix_b=bs_full, conv_w=jnp.transpose(conv_w, (1, 0, 2)),
                 pe_g=row2(pe_norm_g[0]), fin_g=row2(final_norm_g))
    mm_f32 = dict(w_in=w_in[0], w_a=w_a_out[0], w_b=w_b_out[0], w_o=w_o[0],
                  w_pg=w_pe_gate[0], w_pp=w_pe_proj[0])
    mm_in = tuple(mm_f32[k] for k in MATMUL_NAMES)

    def specs(arrs):
        return [_resident(a.shape) for a in arrs]

    hbm = pl.BlockSpec(memory_space=pltpu.HBM)

    wmix = jnp.repeat(jnp.transpose(w_s[0][:, :n_t, :n_t], (1, 2, 0)), HEAD_W,
                      axis=2).reshape(n_t * n_t, d)
    sample_small = tuple(dict(small, mix=wmix)[k] for k in SMALL_NAMES)
    sample_in = (x_sample, p_sample[0], state_conv[0]) + sample_small
    out_shapes = ((n_seq, n_t, d), (n_seq, CONV_W - 1, d), (n_seq, n_t, d))
    outs = pl.pallas_call(
        _sample_kernel,
        grid=(1,),
        in_specs=specs(sample_in) + [hbm] * N_MM,
        out_specs=[_resident(s) for s in out_shapes] + [hbm] * N_MM,
        out_shape=[jax.ShapeDtypeStruct(s, f32) for s in out_shapes]
        + [jax.ShapeDtypeStruct(a.shape, bf) for a in mm_in],
        scratch_shapes=[pltpu.VMEM(a.shape, bf) for a in mm_in]
        + [pltpu.VMEM((STAGE_SLOTS, STAGE_ROWS, STAGE_COLS), f32),
           pltpu.SemaphoreType.DMA((STAGE_SLOTS,)), pltpu.SemaphoreType.DMA((N_MM,))],
        compiler_params=pltpu.CompilerParams(
            dimension_semantics=("arbitrary",),
            vmem_limit_bytes=V7X_VMEM_LIMIT_BYTES),
        name="sample_layer",
    )(*sample_in, *mm_in)
    y_s, conv_s, v_s = outs[:3]
    mm_bf16 = tuple(outs[3:])

    prompt_w = tuple(small[k] for k in SMALL_NAMES) + mm_bf16
    y_p, conv_p, v_p = pl.pallas_call(
        _prompt_kernel,
        grid=(batch, seq // tm),
        in_specs=[pl.BlockSpec((1, tm, d), lambda b, i: (b, i, 0)),
                  pl.BlockSpec((1, tm, p_dim), lambda b, i: (b, i, 0))] + specs(prompt_w),
        out_specs=[pl.BlockSpec((1, tm, d), lambda b, i: (b, i, 0)),
                   pl.BlockSpec((1, 1, CONV_W - 1, d), lambda b, i: (0, b, 0, 0)),
                   pl.BlockSpec((1, 1, CHUNK, d), lambda b, i: (0, b, 0, 0))],
        out_shape=[jax.ShapeDtypeStruct((batch, seq, d), f32),
                   jax.ShapeDtypeStruct((1, batch, CONV_W - 1, d), f32),
                   jax.ShapeDtypeStruct((1, batch, CHUNK, d), f32)],
        scratch_shapes=[pltpu.VMEM((tm + CARRY_ROWS, d), f32)],
        compiler_params=pltpu.CompilerParams(
            dimension_semantics=("arbitrary", "arbitrary"),
            vmem_limit_bytes=V7X_VMEM_LIMIT_BYTES),
        name="prompt_layer",
    )(x_prompt, p_prompt[0], *prompt_w)

    return (y_p, y_s, conv_p, conv_s[None], v_p, v_s[None])
```

```python
import math

import jax
import jax.numpy as jnp
from jax import lax
from jax.experimental import pallas as pl
from jax.experimental.pallas import tpu as pltpu

CHUNK = 128
N_HEADS = 8
HEAD_W = 128
CONV_W = 3
EPS = 1e-6
LN_EPS = 1e-5
COL_U, COL_V, COL_GA, COL_C, COL_B, COL_H, COL_GB, COL_MA, COL_MB = range(9)

V7X_VMEM_LIMIT_BYTES = 60000 * 1024
PROMPT_TILE = 512
CARRY_ROWS = 8
STAGE_ROWS, STAGE_COLS = 256, 1024
STAGE_SLOTS = 8


def _gelu(x):
    c = math.sqrt(2.0 / math.pi)
    return x * (0.5 * (1.0 + jnp.tanh(c * (x + 0.044715 * (x * x * x)))))


def _sigmoid(x):
    return 0.5 * jnp.tanh(0.5 * x) + 0.5


def _silu(x):
    return x * _sigmoid(x)


def _rms_norm(x, g):
    return x * lax.rsqrt(jnp.mean(x * x, axis=-1, keepdims=True) + EPS) * g


def _layer_norm(x, g, b):
    mu = jnp.mean(x, axis=-1, keepdims=True)
    xc = x - mu
    var = jnp.mean(xc * xc, axis=-1, keepdims=True)
    return xc * lax.rsqrt(var + LN_EPS) * g + b


def _dot(a, b):
    return jnp.dot(a, b, preferred_element_type=jnp.float32)


SMALL_NAMES = ("norm_g", "ln_g", "ln_b", "mix", "mix_b", "conv_w", "pe_g", "fin_g")
MATMUL_NAMES = ("w_in", "w_a", "w_b", "w_o", "w_pg", "w_pp")
N_SMALL, N_MM = len(SMALL_NAMES), len(MATMUL_NAMES)
WEIGHT_USE_ORDER = (
    ("w_in", COL_V), ("w_in", COL_U), ("w_in", COL_GA), ("w_in", COL_C), ("w_in", COL_H),
    ("w_in", COL_B), ("w_in", COL_GB), ("w_in", COL_MA), "w_a", ("w_in", COL_MB), "w_b",
    "w_o", "w_pg", "w_pp")


def _layer_tile(x, p, w, spatial_mix, conv_taps, weights_ready=lambda key: None):
    d = x.shape[-1]
    bf = jnp.bfloat16
    h = _rms_norm(x, w["norm_g"][...]).astype(bf)

    def proj(col):
        weights_ready(("w_in", col))
        return _dot(h, w["w_in"][:, col * d:(col + 1) * d])

    def mm(a, name):
        weights_ready(name)
        return _dot(a.astype(bf), w[name][...])

    v = _layer_norm(_gelu(proj(COL_V)), w["ln_g"][...], w["ln_b"][...])
    y_a = _gelu(proj(COL_U)) * spatial_mix(v) * _silu(proj(COL_GA))

    u = proj(COL_C) * proj(COL_H)
    u1, u2 = conv_taps(u)
    cw = w["conv_w"]
    conv = cw[0] * u2 + cw[1] * u1 + cw[2] * u
    y_b = proj(COL_B) * conv * _silu(proj(COL_GB))

    merged = (_sigmoid(proj(COL_MA)) * mm(y_a, "w_a")
              + _sigmoid(proj(COL_MB)) * mm(y_b, "w_b"))
    x = x + mm(merged, "w_o")
    gate = _sigmoid(mm(_rms_norm(x, w["pe_g"][...]), "w_pg"))
    x = x + gate * mm(p, "w_pp")
    return _rms_norm(x, w["fin_g"][...]), v, u


def _weight_stream(f32_hbm, bf16_vmem, bf16_hbm, stage_ref, in_sem, out_sem):
    index = dict(zip(MATMUL_NAMES, range(N_MM)))
    chunks, block_end, block_cols = [], [], []
    for key in WEIGHT_USE_ORDER:
        name, col = key if isinstance(key, tuple) else (key, None)
        rows, cols = f32_hbm[index[name]].shape
        rb = min(rows, STAGE_ROWS)
        n_blocks = len([k for k in WEIGHT_USE_ORDER if isinstance(k, tuple) and k[0] == name])
        width = cols if col is None else cols // n_blocks
        assert rows % rb == 0 and width % STAGE_COLS == 0
        c_lo = 0 if col is None else col * width
        for c0 in range(c_lo, c_lo + width, STAGE_COLS):
            for r0 in range(0, rows, rb):
                chunks.append((index[name], r0, rb, c0))
        block_end.append(len(chunks))
        block_cols.append((index[name], c_lo, width))
    n_slots = stage_ref.shape[0]
    visited = []

    def fetch(k):
        wi, r0, rb, c0 = chunks[k]
        return pltpu.make_async_copy(
            f32_hbm[wi].at[pl.ds(r0, rb), pl.ds(c0, STAGE_COLS)],
            stage_ref.at[k % n_slots, pl.ds(0, rb), :], in_sem.at[k % n_slots])

    def export(i):
        wi, c_lo, width = block_cols[i]
        return pltpu.make_async_copy(bf16_vmem[wi].at[:, pl.ds(c_lo, width)],
                                     bf16_hbm[wi].at[:, pl.ds(c_lo, width)], out_sem.at[i])

    for k in range(min(n_slots, len(chunks))):
        fetch(k).start()

    def weights_ready(key):
        i = len(visited)
        assert key == WEIGHT_USE_ORDER[i], (key, WEIGHT_USE_ORDER[i])
        visited.append(key)
        for k in range(block_end[i - 1] if i else 0, block_end[i]):
            wi, r0, rb, c0 = chunks[k]
            fetch(k).wait()
            bf16_vmem[wi][pl.ds(r0, rb), pl.ds(c0, STAGE_COLS)] = (
                stage_ref[k % n_slots, 0:rb, :].astype(jnp.bfloat16))
            if k + n_slots < len(chunks):
                fetch(k + n_slots).start()
        export(i).start()

    def drain():
        assert len(visited) == len(WEIGHT_USE_ORDER)
        for i in range(len(WEIGHT_USE_ORDER)):
            export(i).wait()

    return weights_ready, drain


def _prompt_kernel(x_ref, p_ref, *rest):
    w = dict(zip(SMALL_NAMES + MATMUL_NAMES, rest[:N_SMALL + N_MM]))
    y_ref, conv_ref, vrow_ref, pbuf_ref = rest[N_SMALL + N_MM:]
    tm, d = x_ref.shape[1], x_ref.shape[2]
    bf = jnp.bfloat16

    @pl.when(pl.program_id(1) == 0)
    def _():
        pbuf_ref[0:CARRY_ROWS, :] = jnp.zeros((CARRY_ROWS, d), jnp.float32)

    def spatial_mix(v):
        v_b = v.astype(bf)
        row = lax.broadcasted_iota(jnp.int32, (CHUNK, CHUNK), 0)
        col = lax.broadcasted_iota(jnp.int32, (CHUNK, CHUNK), 1)
        mix = [jnp.where(col <= row, w["mix"][hd], 0.0).astype(bf)
               for hd in range(N_HEADS)]
        s_rows = []
        for c in range(tm // CHUNK):
            s_heads = [_dot(mix[hd], v_b[c * CHUNK:(c + 1) * CHUNK,
                                         hd * HEAD_W:(hd + 1) * HEAD_W])
                       for hd in range(N_HEADS)]
            s_rows.append(jnp.concatenate(s_heads, axis=1) + w["mix_b"][...])
        return jnp.concatenate(s_rows, axis=0)

    def conv_taps(u):
        pbuf_ref[CARRY_ROWS:CARRY_ROWS + tm, :] = u
        return (pbuf_ref[CARRY_ROWS - 1:CARRY_ROWS - 1 + tm, :],
                pbuf_ref[CARRY_ROWS - 2:CARRY_ROWS - 2 + tm, :])

    y, v, _ = _layer_tile(x_ref[0], p_ref[0], w, spatial_mix, conv_taps)
    y_ref[0] = y
    vrow_ref[0, 0] = v[tm - CHUNK:, :]
    tail = pbuf_ref[tm:tm + CARRY_ROWS, :]
    pbuf_ref[0:CARRY_ROWS, :] = tail
    conv_ref[0, 0] = tail[CARRY_ROWS - (CONV_W - 1):, :]


def _sample_kernel(x_ref, p_ref, st_ref, *rest):
    small = rest[:N_SMALL]
    mm_f32_hbm = rest[N_SMALL:N_SMALL + N_MM]
    outs = rest[N_SMALL + N_MM:]
    y_ref, conv_ref, vrow_ref = outs[:3]
    mm_bf16_hbm = outs[3:3 + N_MM]
    mm_bf16 = outs[3 + N_MM:3 + 2 * N_MM]
    stage_ref, in_sem, out_sem = outs[3 + 2 * N_MM:]
    w = dict(zip(SMALL_NAMES + MATMUL_NAMES, small + mm_bf16))
    n_seq, n_t, _ = x_ref.shape

    def slab(a, t):
        return a[t * n_seq:(t + 1) * n_seq, :]

    def gather(ref):
        return jnp.concatenate([ref[:, t, :] for t in range(ref.shape[1])], axis=0)

    def scatter(ref, a):
        for t in range(ref.shape[1]):
            ref[:, t, :] = slab(a, t)

    def spatial_mix(v):
        out = []
        for t in range(n_t):
            acc = w["mix_b"][t:t + 1, :] + w["mix"][t * n_t:t * n_t + 1, :] * slab(v, 0)
            for j in range(1, t + 1):
                acc = acc + w["mix"][t * n_t + j:t * n_t + j + 1, :] * slab(v, j)
            out.append(acc)
        return jnp.concatenate(out, axis=0)

    def conv_taps(u):
        full = [st_ref[:, j, :] for j in range(CONV_W - 1)] + [
            slab(u, t) for t in range(n_t)]
        return (jnp.concatenate(full[1:1 + n_t], axis=0),
                jnp.concatenate(full[0:n_t], axis=0))

    weights_ready, drain = _weight_stream(mm_f32_hbm, mm_bf16, mm_bf16_hbm, stage_ref,
                                          in_sem, out_sem)
    y, v, u = _layer_tile(gather(x_ref), gather(p_ref), w, spatial_mix, conv_taps,
                          weights_ready)
    scatter(y_ref, y)
    scatter(vrow_ref, v)
    scatter(conv_ref, u[(n_t - (CONV_W - 1)) * n_seq:, :])
    drain()


def _resident(shape):
    return pl.BlockSpec(shape, lambda *_: (0,) * len(shape),
                        pipeline_mode=pl.Buffered(1))


def kernel(x_prompt, x_sample, state_conv, p_prompt, p_sample, norm_g, w_in, ln_v_g,
           ln_v_b, w_s, b_s, conv_w, w_a_out, w_b_out, w_o, pe_norm_g, w_pe_gate,
           w_pe_proj, final_norm_g):
    assert w_in.shape[0] == 1, "single-layer step only"
    batch, seq, d = x_prompt.shape
    n_seq, n_t, _ = x_sample.shape
    p_dim = p_prompt.shape[-1]
    tm = PROMPT_TILE
    assert seq % tm == 0 and tm % CHUNK == 0 and d == N_HEADS * HEAD_W
    assert CONV_W - 1 <= n_t <= CHUNK
    f32, bf = jnp.float32, jnp.bfloat16

    row2 = lambda a: a.reshape(1, -1)
    bs_full = jnp.repeat(b_s[0].T, HEAD_W, axis=1)
    small = dict(norm_g=row2(norm_g[0]), ln_g=row2(ln_v_g[0]), ln_b=row2(ln_v_b[0]),
                 mix=w_s[0], mix_b=bs_full, conv_w=jnp.transpose(conv_w, (1, 0, 2)),
                 pe_g=row2(pe_norm_g[0]), fin_g=row2(final_norm_g))
    mm_f32 = dict(w_in=w_in[0], w_a=w_a_out[0], w_b=w_b_out[0], w_o=w_o[0],
                  w_pg=w_pe_gate[0], w_pp=w_pe_proj[0])
    mm_in = tuple(mm_f32[k] for k in MATMUL_NAMES)

    def specs(arrs):
        return [_resident(a.shape) for a in arrs]

    hbm = pl.BlockSpec(memory_space=pltpu.HBM)

    wmix = jnp.repeat(jnp.transpose(w_s[0][:, :n_t, :n_t], (1, 2, 0)), HEAD_W,
                      axis=2).reshape(n_t * n_t, d)
    sample_small = tuple(dict(small, mix=wmix)[k] for k in SMALL_NAMES)
    sample_in = (x_sample, p_sample[0], state_conv[0]) + sample_small
    out_shapes = ((n_seq, n_t, d), (n_seq, CONV_W - 1, d), (n_seq, n_t, d))
    outs = pl.pallas_call(
        _sample_kernel,
        grid=(1,),
        in_specs=specs(sample_in) + [hbm] * N_MM,
        out_specs=[_resident(s) for s in out_shapes] + [hbm] * N_MM,
        out_shape=[jax.ShapeDtypeStruct(s, f32) for s in out_shapes]
        + [jax.ShapeDtypeStruct(a.shape, bf) for a in mm_in],
        scratch_shapes=[pltpu.VMEM(a.shape, bf) for a in mm_in]
        + [pltpu.VMEM((STAGE_SLOTS, STAGE_ROWS, STAGE_COLS), f32),
           pltpu.SemaphoreType.DMA((STAGE_SLOTS,)),
           pltpu.SemaphoreType.DMA((len(WEIGHT_USE_ORDER),))],
        compiler_params=pltpu.CompilerParams(
            dimension_semantics=("arbitrary",),
            vmem_limit_bytes=V7X_VMEM_LIMIT_BYTES),
        name="sample_layer",
    )(*sample_in, *mm_in)
    y_s, conv_s, v_s = outs[:3]
    mm_bf16 = tuple(outs[3:])

    prompt_w = tuple(small[k] for k in SMALL_NAMES) + mm_bf16
    y_p, conv_p, v_p = pl.pallas_call(
        _prompt_kernel,
        grid=(batch, seq // tm),
        in_specs=[pl.BlockSpec((1, tm, d), lambda b, i: (b, i, 0)),
                  pl.BlockSpec((1, tm, p_dim), lambda b, i: (b, i, 0))] + specs(prompt_w),
        out_specs=[pl.BlockSpec((1, tm, d), lambda b, i: (b, i, 0)),
                   pl.BlockSpec((1, 1, CONV_W - 1, d), lambda b, i: (0, b, 0, 0)),
                   pl.BlockSpec((1, 1, CHUNK, d), lambda b, i: (0, b, 0, 0))],
        out_shape=[jax.ShapeDtypeStruct((batch, seq, d), f32),
                   jax.ShapeDtypeStruct((1, batch, CONV_W - 1, d), f32),
                   jax.ShapeDtypeStruct((1, batch, CHUNK, d), f32)],
        scratch_shapes=[pltpu.VMEM((tm + CARRY_ROWS, d), f32)],
        compiler_params=pltpu.CompilerParams(
            dimension_semantics=("arbitrary", "arbitrary"),
            vmem_limit_bytes=V7X_VMEM_LIMIT_BYTES),
        name="prompt_layer",
    )(x_prompt, p_prompt[0], *prompt_w)

    return (y_p, y_s, conv_p, conv_s[None], v_p, v_s[None])
```

```python
import math

import jax
import jax.numpy as jnp
from jax import lax
from jax.experimental import pallas as pl
from jax.experimental.pallas import tpu as pltpu

CHUNK = 128
N_HEADS = 8
HEAD_W = 128
CONV_W = 3
EPS = 1e-6
LN_EPS = 1e-5
COL_U, COL_V, COL_GA, COL_C, COL_B, COL_H, COL_GB, COL_MA, COL_MB = range(9)

V7X_VMEM_LIMIT_BYTES = 60000 * 1024
PROMPT_TILE = 512
CARRY_ROWS = 8
STAGE_ROWS, STAGE_COLS = 256, 1024
STAGE_SLOTS = 8


def _gelu(x):
    c = math.sqrt(2.0 / math.pi)
    return x * (0.5 * (1.0 + jnp.tanh(c * (x + 0.044715 * (x * x * x)))))


def _sigmoid(x):
    return 0.5 * jnp.tanh(0.5 * x) + 0.5


def _silu(x):
    return x * _sigmoid(x)


def _rms_norm(x, g):
    return x * lax.rsqrt(jnp.mean(x * x, axis=-1, keepdims=True) + EPS) * g


def _layer_norm(x, g, b):
    mu = jnp.mean(x, axis=-1, keepdims=True)
    xc = x - mu
    var = jnp.mean(xc * xc, axis=-1, keepdims=True)
    return xc * lax.rsqrt(var + LN_EPS) * g + b


def _dot(a, b):
    return jnp.dot(a, b, preferred_element_type=jnp.float32)


SMALL_NAMES = ("norm_g", "ln_g", "ln_b", "mix", "mix_b", "conv_w", "pe_g", "fin_g")
MATMUL_NAMES = ("w_in", "w_a", "w_b", "w_o", "w_pg", "w_pp")
N_SMALL, N_MM = len(SMALL_NAMES), len(MATMUL_NAMES)
WEIGHT_USE_ORDER = (
    ("w_in", COL_V), ("w_in", COL_U), ("w_in", COL_GA), ("w_in", COL_C), ("w_in", COL_H),
    ("w_in", COL_B), ("w_in", COL_GB), ("w_in", COL_MA), "w_a", ("w_in", COL_MB), "w_b",
    "w_o", "w_pg", "w_pp")


def _layer_tile(x, p, w, spatial_mix, conv_taps, weights_ready=lambda key: None):
    d = x.shape[-1]
    bf = jnp.bfloat16
    h = _rms_norm(x, w["norm_g"][...]).astype(bf)

    def proj(col):
        weights_ready(("w_in", col))
        return _dot(h, w["w_in"][:, col * d:(col + 1) * d])

    def mm(a, name):
        weights_ready(name)
        return _dot(a.astype(bf), w[name][...])

    v = _layer_norm(_gelu(proj(COL_V)), w["ln_g"][...], w["ln_b"][...])
    y_a = _gelu(proj(COL_U)) * spatial_mix(v) * _silu(proj(COL_GA))

    u = proj(COL_C) * proj(COL_H)
    u1, u2 = conv_taps(u)
    cw = w["conv_w"]
    conv = cw[0] * u2 + cw[1] * u1 + cw[2] * u
    y_b = proj(COL_B) * conv * _silu(proj(COL_GB))

    merged = (_sigmoid(proj(COL_MA)) * mm(y_a, "w_a")
              + _sigmoid(proj(COL_MB)) * mm(y_b, "w_b"))
    x = x + mm(merged, "w_o")
    gate = _sigmoid(mm(_rms_norm(x, w["pe_g"][...]), "w_pg"))
    x = x + gate * mm(p, "w_pp")
    return _rms_norm(x, w["fin_g"][...]), v, u


def _export_copy(wi, vmem_refs, hbm_refs, sem):
    return pltpu.make_async_copy(vmem_refs[wi], hbm_refs[wi], sem.at[wi])


def _weight_stream(f32_hbm, bf16_vmem, bf16_hbm, stage_ref, in_sem, out_sem):
    index = dict(zip(MATMUL_NAMES, range(N_MM)))
    chunks, block_end = [], []
    for key in WEIGHT_USE_ORDER:
        name, col = key if isinstance(key, tuple) else (key, None)
        rows, cols = f32_hbm[index[name]].shape
        rb = min(rows, STAGE_ROWS)
        n_blocks = len([k for k in WEIGHT_USE_ORDER if isinstance(k, tuple) and k[0] == name])
        width = cols if col is None else cols // n_blocks
        assert rows % rb == 0 and width % STAGE_COLS == 0
        c_lo = 0 if col is None else col * width
        for c0 in range(c_lo, c_lo + width, STAGE_COLS):
            for r0 in range(0, rows, rb):
                chunks.append((index[name], r0, rb, c0))
        block_end.append(len(chunks))
    last_block = {index[k[0] if isinstance(k, tuple) else k]: i
                  for i, k in enumerate(WEIGHT_USE_ORDER)}
    n_slots = stage_ref.shape[0]
    visited = []

    def fetch(k):
        wi, r0, rb, c0 = chunks[k]
        return pltpu.make_async_copy(
            f32_hbm[wi].at[pl.ds(r0, rb), pl.ds(c0, STAGE_COLS)],
            stage_ref.at[k % n_slots, pl.ds(0, rb), :], in_sem.at[k % n_slots])

    for k in range(min(n_slots, len(chunks))):
        fetch(k).start()

    def weights_ready(key):
        i = len(visited)
        assert key == WEIGHT_USE_ORDER[i], (key, WEIGHT_USE_ORDER[i])
        visited.append(key)
        for k in range(block_end[i - 1] if i else 0, block_end[i]):
            wi, r0, rb, c0 = chunks[k]
            fetch(k).wait()
            bf16_vmem[wi][pl.ds(r0, rb), pl.ds(c0, STAGE_COLS)] = (
                stage_ref[k % n_slots, 0:rb, :].astype(jnp.bfloat16))
            if k + n_slots < len(chunks):
                fetch(k + n_slots).start()
        for wi, last in last_block.items():
            if last == i:
                _export_copy(wi, bf16_vmem, bf16_hbm, out_sem).start()

    return weights_ready, visited


def _prompt_kernel(x_ref, p_ref, *rest):
    w = dict(zip(SMALL_NAMES + MATMUL_NAMES, rest[:N_SMALL + N_MM]))
    y_ref, conv_ref, vrow_ref, pbuf_ref = rest[N_SMALL + N_MM:]
    tm, d = x_ref.shape[1], x_ref.shape[2]
    bf = jnp.bfloat16

    @pl.when(pl.program_id(1) == 0)
    def _():
        pbuf_ref[0:CARRY_ROWS, :] = jnp.zeros((CARRY_ROWS, d), jnp.float32)

    def spatial_mix(v):
        v_b = v.astype(bf)
        row = lax.broadcasted_iota(jnp.int32, (CHUNK, CHUNK), 0)
        col = lax.broadcasted_iota(jnp.int32, (CHUNK, CHUNK), 1)
        mix = [jnp.where(col <= row, w["mix"][hd], 0.0).astype(bf)
               for hd in range(N_HEADS)]
        s_rows = []
        for c in range(tm // CHUNK):
            s_heads = [_dot(mix[hd], v_b[c * CHUNK:(c + 1) * CHUNK,
                                         hd * HEAD_W:(hd + 1) * HEAD_W])
                       for hd in range(N_HEADS)]
            s_rows.append(jnp.concatenate(s_heads, axis=1) + w["mix_b"][...])
        return jnp.concatenate(s_rows, axis=0)

    def conv_taps(u):
        pbuf_ref[CARRY_ROWS:CARRY_ROWS + tm, :] = u
        return (pbuf_ref[CARRY_ROWS - 1:CARRY_ROWS - 1 + tm, :],
                pbuf_ref[CARRY_ROWS - 2:CARRY_ROWS - 2 + tm, :])

    y, v, _ = _layer_tile(x_ref[0], p_ref[0], w, spatial_mix, conv_taps)
    y_ref[0] = y
    vrow_ref[0, 0] = v[tm - CHUNK:, :]
    tail = pbuf_ref[tm:tm + CARRY_ROWS, :]
    pbuf_ref[0:CARRY_ROWS, :] = tail
    conv_ref[0, 0] = tail[CARRY_ROWS - (CONV_W - 1):, :]


def _sample_kernel(x_ref, p_ref, st_ref, *rest):
    small = rest[:N_SMALL]
    mm_f32_hbm = rest[N_SMALL:N_SMALL + N_MM]
    outs = rest[N_SMALL + N_MM:]
    y_ref, conv_ref, vrow_ref, bias_ref = outs[:4]
    mm_bf16_hbm = outs[4:4 + N_MM]
    mm_bf16 = outs[4 + N_MM:4 + 2 * N_MM]
    stage_ref, in_sem, out_sem = outs[4 + 2 * N_MM:]
    w = dict(zip(SMALL_NAMES + MATMUL_NAMES, small + mm_bf16))
    n_seq, n_t, _ = x_ref.shape

    b_t = jnp.transpose(w["mix_b"][...])
    bias_tile = jnp.concatenate(
        [jnp.broadcast_to(b_t[:, hd:hd + 1], (CHUNK, HEAD_W)) for hd in range(N_HEADS)],
        axis=1)
    bias_ref[...] = bias_tile

    def mix_row(t, j):
        return jnp.concatenate(
            [jnp.broadcast_to(w["mix"][hd, t:t + 1, j:j + 1], (1, HEAD_W))
             for hd in range(N_HEADS)], axis=1)

    def slab(a, t):
        return a[t * n_seq:(t + 1) * n_seq, :]

    def gather(ref):
        return jnp.concatenate([ref[:, t, :] for t in range(ref.shape[1])], axis=0)

    def scatter(ref, a):
        for t in range(ref.shape[1]):
            ref[:, t, :] = slab(a, t)

    def spatial_mix(v):
        out = []
        for t in range(n_t):
            acc = bias_tile[t:t + 1, :] + mix_row(t, 0) * slab(v, 0)
            for j in range(1, t + 1):
                acc = acc + mix_row(t, j) * slab(v, j)
            out.append(acc)
        return jnp.concatenate(out, axis=0)

    def conv_taps(u):
        full = [st_ref[:, j, :] for j in range(CONV_W - 1)] + [
            slab(u, t) for t in range(n_t)]
        return (jnp.concatenate(full[1:1 + n_t], axis=0),
                jnp.concatenate(full[0:n_t], axis=0))

    weights_ready, visited = _weight_stream(mm_f32_hbm, mm_bf16, mm_bf16_hbm, stage_ref,
                                            in_sem, out_sem)
    y, v, u = _layer_tile(gather(x_ref), gather(p_ref), w, spatial_mix, conv_taps,
                          weights_ready)
    assert len(visited) == len(WEIGHT_USE_ORDER)
    scatter(y_ref, y)
    scatter(vrow_ref, v)
    scatter(conv_ref, u[(n_t - (CONV_W - 1)) * n_seq:, :])
    for wi in range(N_MM):
        _export_copy(wi, mm_bf16, mm_bf16_hbm, out_sem).wait()


def _resident(shape):
    return pl.BlockSpec(shape, lambda *_: (0,) * len(shape),
                        pipeline_mode=pl.Buffered(1))


def kernel(x_prompt, x_sample, state_conv, p_prompt, p_sample, norm_g, w_in, ln_v_g,
           ln_v_b, w_s, b_s, conv_w, w_a_out, w_b_out, w_o, pe_norm_g, w_pe_gate,
           w_pe_proj, final_norm_g):
    assert w_in.shape[0] == 1, "single-layer step only"
    batch, seq, d = x_prompt.shape
    n_seq, n_t, _ = x_sample.shape
    p_dim = p_prompt.shape[-1]
    tm = PROMPT_TILE
    assert seq % tm == 0 and tm % CHUNK == 0 and d == N_HEADS * HEAD_W
    assert CONV_W - 1 <= n_t <= CHUNK
    f32, bf = jnp.float32, jnp.bfloat16

    row2 = lambda a: a.reshape(1, -1)
    small = dict(norm_g=row2(norm_g[0]), ln_g=row2(ln_v_g[0]), ln_b=row2(ln_v_b[0]),
                 mix=w_s[0], mix_b=b_s[0], conv_w=jnp.transpose(conv_w, (1, 0, 2)),
                 pe_g=row2(pe_norm_g[0]), fin_g=row2(final_norm_g))
    mm_f32 = dict(w_in=w_in[0], w_a=w_a_out[0], w_b=w_b_out[0], w_o=w_o[0],
                  w_pg=w_pe_gate[0], w_pp=w_pe_proj[0])
    mm_in = tuple(mm_f32[k] for k in MATMUL_NAMES)

    def specs(arrs):
        return [_resident(a.shape) for a in arrs]

    hbm = pl.BlockSpec(memory_space=pltpu.HBM)

    sample_in = (x_sample, p_sample[0], state_conv[0]) + tuple(small[k] for k in SMALL_NAMES)
    out_shapes = ((n_seq, n_t, d), (n_seq, CONV_W - 1, d), (n_seq, n_t, d), (CHUNK, d))
    outs = pl.pallas_call(
        _sample_kernel,
        grid=(1,),
        in_specs=specs(sample_in) + [hbm] * N_MM,
        out_specs=[_resident(s) for s in out_shapes] + [hbm] * N_MM,
        out_shape=[jax.ShapeDtypeStruct(s, f32) for s in out_shapes]
        + [jax.ShapeDtypeStruct(a.shape, bf) for a in mm_in],
        scratch_shapes=[pltpu.VMEM(a.shape, bf) for a in mm_in]
        + [pltpu.VMEM((STAGE_SLOTS, STAGE_ROWS, STAGE_COLS), f32),
           pltpu.SemaphoreType.DMA((STAGE_SLOTS,)), pltpu.SemaphoreType.DMA((N_MM,))],
        compiler_params=pltpu.CompilerParams(
            dimension_semantics=("arbitrary",),
            vmem_limit_bytes=V7X_VMEM_LIMIT_BYTES),
        name="sample_layer",
    )(*sample_in, *mm_in)
    y_s, conv_s, v_s, bias_tile = outs[:4]
    mm_bf16 = tuple(outs[4:])

    prompt_w = tuple(dict(small, mix_b=bias_tile)[k] for k in SMALL_NAMES) + mm_bf16
    y_p, conv_p, v_p = pl.pallas_call(
        _prompt_kernel,
        grid=(batch, seq // tm),
        in_specs=[pl.BlockSpec((1, tm, d), lambda b, i: (b, i, 0)),
                  pl.BlockSpec((1, tm, p_dim), lambda b, i: (b, i, 0))] + specs(prompt_w),
        out_specs=[pl.BlockSpec((1, tm, d), lambda b, i: (b, i, 0)),
                   pl.BlockSpec((1, 1, CONV_W - 1, d), lambda b, i: (0, b, 0, 0)),
                   pl.BlockSpec((1, 1, CHUNK, d), lambda b, i: (0, b, 0, 0))],
        out_shape=[jax.ShapeDtypeStruct((batch, seq, d), f32),
                   jax.ShapeDtypeStruct((1, batch, CONV_W - 1, d), f32),
                   jax.ShapeDtypeStruct((1, batch, CHUNK, d), f32)],
        scratch_shapes=[pltpu.VMEM((tm + CARRY_ROWS, d), f32)],
        compiler_params=pltpu.CompilerParams(
            dimension_semantics=("arbitrary", "arbitrary"),
            vmem_limit_bytes=V7X_VMEM_LIMIT_BYTES),
        name="prompt_layer",
    )(x_prompt, p_prompt[0], *prompt_w)

    return (y_p, y_s, conv_p, conv_s[None], v_p, v_s[None])
```

```python
import math

import jax
import jax.numpy as jnp
from jax import lax
from jax.experimental import pallas as pl
from jax.experimental.pallas import tpu as pltpu

CHUNK = 128
N_HEADS = 8
HEAD_W = 128
CONV_W = 3
EPS = 1e-6
LN_EPS = 1e-5
COL_U, COL_V, COL_GA, COL_C, COL_B, COL_H, COL_GB, COL_MA, COL_MB = range(9)

V7X_VMEM_LIMIT_BYTES = 60000 * 1024
PROMPT_TILE = 512
CARRY_ROWS = 8
STAGE_ROWS, STAGE_COLS = 256, 1024
STAGE_SLOTS = 8


def _gelu(x):
    c = math.sqrt(2.0 / math.pi)
    return x * (0.5 * (1.0 + jnp.tanh(c * (x + 0.044715 * (x * x * x)))))


def _sigmoid(x):
    return 0.5 * jnp.tanh(0.5 * x) + 0.5


def _silu(x):
    return x * _sigmoid(x)


def _rms_norm(x, g):
    return x * lax.rsqrt(jnp.mean(x * x, axis=-1, keepdims=True) + EPS) * g


def _layer_norm(x, g, b):
    mu = jnp.mean(x, axis=-1, keepdims=True)
    xc = x - mu
    var = jnp.mean(xc * xc, axis=-1, keepdims=True)
    return xc * lax.rsqrt(var + LN_EPS) * g + b


def _dot(a, b):
    return jnp.dot(a, b, preferred_element_type=jnp.float32)


SMALL_NAMES = ("norm_g", "ln_g", "ln_b", "mix", "mix_b", "conv_w", "pe_g", "fin_g")
MATMUL_NAMES = ("w_in", "w_a", "w_b", "w_o", "w_pg", "w_pp")
N_SMALL, N_MM = len(SMALL_NAMES), len(MATMUL_NAMES)
WEIGHT_USE_ORDER = (
    ("w_in", COL_V), ("w_in", COL_U), ("w_in", COL_GA), ("w_in", COL_C), ("w_in", COL_H),
    ("w_in", COL_B), ("w_in", COL_GB), ("w_in", COL_MA), "w_a", ("w_in", COL_MB), "w_b",
    "w_o", "w_pg", "w_pp")


def _layer_tile(x, p, w, spatial_mix, conv_taps, weights_ready=lambda key: None):
    d = x.shape[-1]
    bf = jnp.bfloat16
    h = _rms_norm(x, w["norm_g"][...]).astype(bf)

    def proj(col):
        weights_ready(("w_in", col))
        return _dot(h, w["w_in"][:, col * d:(col + 1) * d])

    def mm(a, name):
        weights_ready(name)
        return _dot(a.astype(bf), w[name][...])

    v = _layer_norm(_gelu(proj(COL_V)), w["ln_g"][...], w["ln_b"][...])
    y_a = _gelu(proj(COL_U)) * spatial_mix(v) * _silu(proj(COL_GA))

    u = proj(COL_C) * proj(COL_H)
    u1, u2 = conv_taps(u)
    cw = w["conv_w"]
    conv = cw[0] * u2 + cw[1] * u1 + cw[2] * u
    y_b = proj(COL_B) * conv * _silu(proj(COL_GB))

    merged = (_sigmoid(proj(COL_MA)) * mm(y_a, "w_a")
              + _sigmoid(proj(COL_MB)) * mm(y_b, "w_b"))
    x = x + mm(merged, "w_o")
    gate = _sigmoid(mm(_rms_norm(x, w["pe_g"][...]), "w_pg"))
    x = x + gate * mm(p, "w_pp")
    return _rms_norm(x, w["fin_g"][...]), v, u


def _weight_stream(f32_hbm, bf16_vmem, bf16_hbm, stage_ref, in_sem, out_sem):
    index = dict(zip(MATMUL_NAMES, range(N_MM)))
    chunks, block_end, block_cols = [], [], []
    for key in WEIGHT_USE_ORDER:
        name, col = key if isinstance(key, tuple) else (key, None)
        rows, cols = f32_hbm[index[name]].shape
        rb = min(rows, STAGE_ROWS)
        n_blocks = len([k for k in WEIGHT_USE_ORDER if isinstance(k, tuple) and k[0] == name])
        width = cols if col is None else cols // n_blocks
        assert rows % rb == 0 and width % STAGE_COLS == 0
        c_lo = 0 if col is None else col * width
        for c0 in range(c_lo, c_lo + width, STAGE_COLS):
            for r0 in range(0, rows, rb):
                chunks.append((index[name], r0, rb, c0))
        block_end.append(len(chunks))
        block_cols.append((index[name], c_lo, width))
    n_slots = stage_ref.shape[0]
    visited = []

    def fetch(k):
        wi, r0, rb, c0 = chunks[k]
        return pltpu.make_async_copy(
            f32_hbm[wi].at[pl.ds(r0, rb), pl.ds(c0, STAGE_COLS)],
            stage_ref.at[k % n_slots, pl.ds(0, rb), :], in_sem.at[k % n_slots])

    def export(i):
        wi, c_lo, width = block_cols[i]
        return pltpu.make_async_copy(bf16_vmem[wi].at[:, pl.ds(c_lo, width)],
                                     bf16_hbm[wi].at[:, pl.ds(c_lo, width)], out_sem.at[i])

    for k in range(min(n_slots, len(chunks))):
        fetch(k).start()

    def weights_ready(key):
        i = len(visited)
        assert key == WEIGHT_USE_ORDER[i], (key, WEIGHT_USE_ORDER[i])
        visited.append(key)
        for k in range(block_end[i - 1] if i else 0, block_end[i]):
            wi, r0, rb, c0 = chunks[k]
            fetch(k).wait()
            bf16_vmem[wi][pl.ds(r0, rb), pl.ds(c0, STAGE_COLS)] = (
                stage_ref[k % n_slots, 0:rb, :].astype(jnp.bfloat16))
            if k + n_slots < len(chunks):
                fetch(k + n_slots).start()
        export(i).start()

    def drain():
        assert len(visited) == len(WEIGHT_USE_ORDER)
        for i in range(len(WEIGHT_USE_ORDER)):
            export(i).wait()

    return weights_ready, drain


def _prompt_kernel(x_ref, p_ref, *rest):
    w = dict(zip(SMALL_NAMES + MATMUL_NAMES, rest[:N_SMALL + N_MM]))
    y_ref, conv_ref, vrow_ref, pbuf_ref = rest[N_SMALL + N_MM:]
    tm, d = x_ref.shape[1], x_ref.shape[2]
    bf = jnp.bfloat16

    @pl.when(pl.program_id(1) == 0)
    def _():
        pbuf_ref[0:CARRY_ROWS, :] = jnp.zeros((CARRY_ROWS, d), jnp.float32)

    def spatial_mix(v):
        v_b = v.astype(bf)
        row = lax.broadcasted_iota(jnp.int32, (CHUNK, CHUNK), 0)
        col = lax.broadcasted_iota(jnp.int32, (CHUNK, CHUNK), 1)
        mix = [jnp.where(col <= row, w["mix"][hd], 0.0).astype(bf)
               for hd in range(N_HEADS)]
        s_rows = []
        for c in range(tm // CHUNK):
            s_heads = [_dot(mix[hd], v_b[c * CHUNK:(c + 1) * CHUNK,
                                         hd * HEAD_W:(hd + 1) * HEAD_W])
                       for hd in range(N_HEADS)]
            s_rows.append(jnp.concatenate(s_heads, axis=1) + w["mix_b"][...])
        return jnp.concatenate(s_rows, axis=0)

    def conv_taps(u):
        pbuf_ref[CARRY_ROWS:CARRY_ROWS + tm, :] = u
        return (pbuf_ref[CARRY_ROWS - 1:CARRY_ROWS - 1 + tm, :],
                pbuf_ref[CARRY_ROWS - 2:CARRY_ROWS - 2 + tm, :])

    y, v, _ = _layer_tile(x_ref[0], p_ref[0], w, spatial_mix, conv_taps)
    y_ref[0] = y
    vrow_ref[0, 0] = v[tm - CHUNK:, :]
    tail = pbuf_ref[tm:tm + CARRY_ROWS, :]
    pbuf_ref[0:CARRY_ROWS, :] = tail
    conv_ref[0, 0] = tail[CARRY_ROWS - (CONV_W - 1):, :]


def _sample_kernel(x_ref, p_ref, st_ref, *rest):
    small = rest[:N_SMALL]
    mm_f32_hbm = rest[N_SMALL:N_SMALL + N_MM]
    outs = rest[N_SMALL + N_MM:]
    y_ref, conv_ref, vrow_ref, bias_ref = outs[:4]
    mm_bf16_hbm = outs[4:4 + N_MM]
    mm_bf16 = outs[4 + N_MM:4 + 2 * N_MM]
    stage_ref, in_sem, out_sem = outs[4 + 2 * N_MM:]
    w = dict(zip(SMALL_NAMES + MATMUL_NAMES, small + mm_bf16))
    n_seq, n_t, _ = x_ref.shape

    b_t = jnp.transpose(w["mix_b"][...])
    bias_tile = jnp.concatenate(
        [jnp.broadcast_to(b_t[:, hd:hd + 1], (CHUNK, HEAD_W)) for hd in range(N_HEADS)],
        axis=1)
    bias_ref[...] = bias_tile

    def mix_row(t, j):
        return jnp.concatenate(
            [jnp.broadcast_to(w["mix"][hd, t:t + 1, j:j + 1], (1, HEAD_W))
             for hd in range(N_HEADS)], axis=1)

    def slab(a, t):
        return a[t * n_seq:(t + 1) * n_seq, :]

    def gather(ref):
        return jnp.concatenate([ref[:, t, :] for t in range(ref.shape[1])], axis=0)

    def scatter(ref, a):
        for t in range(ref.shape[1]):
            ref[:, t, :] = slab(a, t)

    def spatial_mix(v):
        out = []
        for t in range(n_t):
            acc = bias_tile[t:t + 1, :] + mix_row(t, 0) * slab(v, 0)
            for j in range(1, t + 1):
                acc = acc + mix_row(t, j) * slab(v, j)
            out.append(acc)
        return jnp.concatenate(out, axis=0)

    def conv_taps(u):
        full = [st_ref[:, j, :] for j in range(CONV_W - 1)] + [
            slab(u, t) for t in range(n_t)]
        return (jnp.concatenate(full[1:1 + n_t], axis=0),
                jnp.concatenate(full[0:n_t], axis=0))

    weights_ready, drain = _weight_stream(mm_f32_hbm, mm_bf16, mm_bf16_hbm, stage_ref,
                                          in_sem, out_sem)
    y, v, u = _layer_tile(gather(x_ref), gather(p_ref), w, spatial_mix, conv_taps,
                          weights_ready)
    scatter(y_ref, y)
    scatter(vrow_ref, v)
    scatter(conv_ref, u[(n_t - (CONV_W - 1)) * n_seq:, :])
    drain()


def _resident(shape):
    return pl.BlockSpec(shape, lambda *_: (0,) * len(shape),
                        pipeline_mode=pl.Buffered(1))


def kernel(x_prompt, x_sample, state_conv, p_prompt, p_sample, norm_g, w_in, ln_v_g,
           ln_v_b, w_s, b_s, conv_w, w_a_out, w_b_out, w_o, pe_norm_g, w_pe_gate,
           w_pe_proj, final_norm_g):
    assert w_in.shape[0] == 1, "single-layer step only"
    batch, seq, d = x_prompt.shape
    n_seq, n_t, _ = x_sample.shape
    p_dim = p_prompt.shape[-1]
    tm = PROMPT_TILE
    assert seq % tm == 0 and tm % CHUNK == 0 and d == N_HEADS * HEAD_W
    assert CONV_W - 1 <= n_t <= CHUNK
    f32, bf = jnp.float32, jnp.bfloat16

    row2 = lambda a: a.reshape(1, -1)
    small = dict(norm_g=row2(norm_g[0]), ln_g=row2(ln_v_g[0]), ln_b=row2(ln_v_b[0]),
                 mix=w_s[0], mix_b=b_s[0], conv_w=jnp.transpose(conv_w, (1, 0, 2)),
                 pe_g=row2(pe_norm_g[0]), fin_g=row2(final_norm_g))
    mm_f32 = dict(w_in=w_in[0], w_a=w_a_out[0], w_b=w_b_out[0], w_o=w_o[0],
                  w_pg=w_pe_gate[0], w_pp=w_pe_proj[0])
    mm_in = tuple(mm_f32[k] for k in MATMUL_NAMES)

    def specs(arrs):
        return [_resident(a.shape) for a in arrs]

    hbm = pl.BlockSpec(memory_space=pltpu.HBM)

    sample_in = (x_sample, p_sample[0], state_conv[0]) + tuple(small[k] for k in SMALL_NAMES)
    out_shapes = ((n_seq, n_t, d), (n_seq, CONV_W - 1, d), (n_seq, n_t, d), (CHUNK, d))
    outs = pl.pallas_call(
        _sample_kernel,
        grid=(1,),
        in_specs=specs(sample_in) + [hbm] * N_MM,
        out_specs=[_resident(s) for s in out_shapes] + [hbm] * N_MM,
        out_shape=[jax.ShapeDtypeStruct(s, f32) for s in out_shapes]
        + [jax.ShapeDtypeStruct(a.shape, bf) for a in mm_in],
        scratch_shapes=[pltpu.VMEM(a.shape, bf) for a in mm_in]
        + [pltpu.VMEM((STAGE_SLOTS, STAGE_ROWS, STAGE_COLS), f32),
           pltpu.SemaphoreType.DMA((STAGE_SLOTS,)),
           pltpu.SemaphoreType.DMA((len(WEIGHT_USE_ORDER),))],
        compiler_params=pltpu.CompilerParams(
            dimension_semantics=("arbitrary",),
            vmem_limit_bytes=V7X_VMEM_LIMIT_BYTES),
        name="sample_layer",
    )(*sample_in, *mm_in)
    y_s, conv_s, v_s, bias_tile = outs[:4]
    mm_bf16 = tuple(outs[4:])

    prompt_w = tuple(dict(small, mix_b=bias_tile)[k] for k in SMALL_NAMES) + mm_bf16
    y_p, conv_p, v_p = pl.pallas_call(
        _prompt_kernel,
        grid=(batch, seq // tm),
        in_specs=[pl.BlockSpec((1, tm, d), lambda b, i: (b, i, 0)),
                  pl.BlockSpec((1, tm, p_dim), lambda b, i: (b, i, 0))] + specs(prompt_w),
        out_specs=[pl.BlockSpec((1, tm, d), lambda b, i: (b, i, 0)),
                   pl.BlockSpec((1, 1, CONV_W - 1, d), lambda b, i: (0, b, 0, 0)),
                   pl.BlockSpec((1, 1, CHUNK, d), lambda b, i: (0, b, 0, 0))],
        out_shape=[jax.ShapeDtypeStruct((batch, seq, d), f32),
                   jax.ShapeDtypeStruct((1, batch, CONV_W - 1, d), f32),
                   jax.ShapeDtypeStruct((1, batch, CHUNK, d), f32)],
        scratch_shapes=[pltpu.VMEM((tm + CARRY_ROWS, d), f32)],
        compiler_params=pltpu.CompilerParams(
            dimension_semantics=("arbitrary", "arbitrary"),
            vmem_limit_bytes=V7X_VMEM_LIMIT_BYTES),
        name="prompt_layer",
    )(x_prompt, p_prompt[0], *prompt_w)

    return (y_p, y_s, conv_p, conv_s[None], v_p, v_s[None])
```

```python
import math

import jax
import jax.numpy as jnp
from jax import lax
from jax.experimental import pallas as pl
from jax.experimental.pallas import tpu as pltpu

CHUNK = 128
N_HEADS = 8
HEAD_W = 128
CONV_W = 3
EPS = 1e-6
LN_EPS = 1e-5
COL_U, COL_V, COL_GA, COL_C, COL_B, COL_H, COL_GB, COL_MA, COL_MB = range(9)

V7X_VMEM_LIMIT_BYTES = 60000 * 1024
PROMPT_TILE = 256
CARRY_ROWS = 8
STAGE_ROWS, STAGE_COLS = 256, 1024
STAGE_SLOTS = 8


def _gelu(x):
    c = math.sqrt(2.0 / math.pi)
    return x * (0.5 * (1.0 + jnp.tanh(c * (x + 0.044715 * (x * x * x)))))


def _sigmoid(x):
    return 0.5 * jnp.tanh(0.5 * x) + 0.5


def _silu(x):
    return x * _sigmoid(x)


def _rms_norm(x, g):
    return x * lax.rsqrt(jnp.mean(x * x, axis=-1, keepdims=True) + EPS) * g


def _layer_norm(x, g, b):
    mu = jnp.mean(x, axis=-1, keepdims=True)
    xc = x - mu
    var = jnp.mean(xc * xc, axis=-1, keepdims=True)
    return xc * lax.rsqrt(var + LN_EPS) * g + b


def _dot(a, b):
    return jnp.dot(a, b, preferred_element_type=jnp.float32)


SMALL_NAMES = ("norm_g", "ln_g", "ln_b", "mix", "mix_b", "conv_w", "pe_g", "fin_g")
MATMUL_NAMES = ("w_in", "w_a", "w_b", "w_o", "w_pg", "w_pp")
N_SMALL, N_MM = len(SMALL_NAMES), len(MATMUL_NAMES)
WEIGHT_USE_ORDER = (
    ("w_in", COL_V), ("w_in", COL_U), ("w_in", COL_GA), ("w_in", COL_C), ("w_in", COL_H),
    ("w_in", COL_B), ("w_in", COL_GB), ("w_in", COL_MA), "w_a", ("w_in", COL_MB), "w_b",
    "w_o", "w_pg", "w_pp")


def _layer_tile(x, p, w, spatial_mix, conv_taps, weights_ready=lambda key: None):
    d = x.shape[-1]
    bf = jnp.bfloat16
    h = _rms_norm(x, w["norm_g"][...]).astype(bf)

    def proj(col):
        weights_ready(("w_in", col))
        return _dot(h, w["w_in"][:, col * d:(col + 1) * d])

    def mm(a, name):
        weights_ready(name)
        return _dot(a.astype(bf), w[name][...])

    v = _layer_norm(_gelu(proj(COL_V)), w["ln_g"][...], w["ln_b"][...])
    y_a = _gelu(proj(COL_U)) * spatial_mix(v) * _silu(proj(COL_GA))

    u = proj(COL_C) * proj(COL_H)
    u1, u2 = conv_taps(u)
    cw = w["conv_w"]
    conv = cw[0] * u2 + cw[1] * u1 + cw[2] * u
    y_b = proj(COL_B) * conv * _silu(proj(COL_GB))

    merged = (_sigmoid(proj(COL_MA)) * mm(y_a, "w_a")
              + _sigmoid(proj(COL_MB)) * mm(y_b, "w_b"))
    x = x + mm(merged, "w_o")
    gate = _sigmoid(mm(_rms_norm(x, w["pe_g"][...]), "w_pg"))
    x = x + gate * mm(p, "w_pp")
    return _rms_norm(x, w["fin_g"][...]), v, u


def _weight_stream(f32_hbm, bf16_vmem, bf16_hbm, stage_ref, in_sem, out_sem):
    index = dict(zip(MATMUL_NAMES, range(N_MM)))
    chunks, block_end, block_cols = [], [], []
    for key in WEIGHT_USE_ORDER:
        name, col = key if isinstance(key, tuple) else (key, None)
        rows, cols = f32_hbm[index[name]].shape
        rb = min(rows, STAGE_ROWS)
        n_blocks = len([k for k in WEIGHT_USE_ORDER if isinstance(k, tuple) and k[0] == name])
        width = cols if col is None else cols // n_blocks
        assert rows % rb == 0 and width % STAGE_COLS == 0
        c_lo = 0 if col is None else col * width
        for c0 in range(c_lo, c_lo + width, STAGE_COLS):
            for r0 in range(0, rows, rb):
                chunks.append((index[name], r0, rb, c0))
        block_end.append(len(chunks))
        block_cols.append((index[name], c_lo, width))
    n_slots = stage_ref.shape[0]
    visited = []

    def fetch(k):
        wi, r0, rb, c0 = chunks[k]
        return pltpu.make_async_copy(
            f32_hbm[wi].at[pl.ds(r0, rb), pl.ds(c0, STAGE_COLS)],
            stage_ref.at[k % n_slots, pl.ds(0, rb), :], in_sem.at[k % n_slots])

    def export(i):
        wi, c_lo, width = block_cols[i]
        return pltpu.make_async_copy(bf16_vmem[wi].at[:, pl.ds(c_lo, width)],
                                     bf16_hbm[wi].at[:, pl.ds(c_lo, width)], out_sem.at[i])

    for k in range(min(n_slots, len(chunks))):
        fetch(k).start()

    def weights_ready(key):
        i = len(visited)
        assert key == WEIGHT_USE_ORDER[i], (key, WEIGHT_USE_ORDER[i])
        visited.append(key)
        for k in range(block_end[i - 1] if i else 0, block_end[i]):
            wi, r0, rb, c0 = chunks[k]
            fetch(k).wait()
            bf16_vmem[wi][pl.ds(r0, rb), pl.ds(c0, STAGE_COLS)] = (
                stage_ref[k % n_slots, 0:rb, :].astype(jnp.bfloat16))
            if k + n_slots < len(chunks):
                fetch(k + n_slots).start()
        export(i).start()

    def drain():
        assert len(visited) == len(WEIGHT_USE_ORDER)
        for i in range(len(WEIGHT_USE_ORDER)):
            export(i).wait()

    return weights_ready, drain


def _prompt_kernel(x_ref, p_ref, *rest):
    w = dict(zip(SMALL_NAMES + MATMUL_NAMES, rest[:N_SMALL + N_MM]))
    y_ref, conv_ref, vrow_ref, pbuf_ref = rest[N_SMALL + N_MM:]
    tm, d = x_ref.shape[1], x_ref.shape[2]
    bf = jnp.bfloat16

    @pl.when(pl.program_id(1) == 0)
    def _():
        pbuf_ref[0:CARRY_ROWS, :] = jnp.zeros((CARRY_ROWS, d), jnp.float32)

    def spatial_mix(v):
        v_b = v.astype(bf)
        row = lax.broadcasted_iota(jnp.int32, (CHUNK, CHUNK), 0)
        col = lax.broadcasted_iota(jnp.int32, (CHUNK, CHUNK), 1)
        mix = [jnp.where(col <= row, w["mix"][hd], 0.0).astype(bf)
               for hd in range(N_HEADS)]
        s_rows = []
        for c in range(tm // CHUNK):
            s_heads = [_dot(mix[hd], v_b[c * CHUNK:(c + 1) * CHUNK,
                                         hd * HEAD_W:(hd + 1) * HEAD_W])
                       for hd in range(N_HEADS)]
            s_rows.append(jnp.concatenate(s_heads, axis=1) + w["mix_b"][...])
        return jnp.concatenate(s_rows, axis=0)

    def conv_taps(u):
        pbuf_ref[CARRY_ROWS:CARRY_ROWS + tm, :] = u
        return (pbuf_ref[CARRY_ROWS - 1:CARRY_ROWS - 1 + tm, :],
                pbuf_ref[CARRY_ROWS - 2:CARRY_ROWS - 2 + tm, :])

    y, v, _ = _layer_tile(x_ref[0], p_ref[0], w, spatial_mix, conv_taps)
    y_ref[0] = y
    vrow_ref[0, 0] = v[tm - CHUNK:, :]
    tail = pbuf_ref[tm:tm + CARRY_ROWS, :]
    pbuf_ref[0:CARRY_ROWS, :] = tail
    conv_ref[0, 0] = tail[CARRY_ROWS - (CONV_W - 1):, :]


def _sample_kernel(x_ref, p_ref, st_ref, *rest):
    small = rest[:N_SMALL]
    mm_f32_hbm = rest[N_SMALL:N_SMALL + N_MM]
    outs = rest[N_SMALL + N_MM:]
    y_ref, conv_ref, vrow_ref, bias_ref = outs[:4]
    mm_bf16_hbm = outs[4:4 + N_MM]
    mm_bf16 = outs[4 + N_MM:4 + 2 * N_MM]
    stage_ref, in_sem, out_sem = outs[4 + 2 * N_MM:]
    w = dict(zip(SMALL_NAMES + MATMUL_NAMES, small + mm_bf16))
    n_seq, n_t, _ = x_ref.shape

    b_t = jnp.transpose(w["mix_b"][...])
    bias_tile = jnp.concatenate(
        [jnp.broadcast_to(b_t[:, hd:hd + 1], (CHUNK, HEAD_W)) for hd in range(N_HEADS)],
        axis=1)
    bias_ref[...] = bias_tile

    def mix_row(t, j):
        return jnp.concatenate(
            [jnp.broadcast_to(w["mix"][hd, t:t + 1, j:j + 1], (1, HEAD_W))
             for hd in range(N_HEADS)], axis=1)

    def slab(a, t):
        return a[t * n_seq:(t + 1) * n_seq, :]

    def gather(ref):
        return jnp.concatenate([ref[:, t, :] for t in range(ref.shape[1])], axis=0)

    def scatter(ref, a):
        for t in range(ref.shape[1]):
            ref[:, t, :] = slab(a, t)

    def spatial_mix(v):
        out = []
        for t in range(n_t):
            acc = bias_tile[t:t + 1, :] + mix_row(t, 0) * slab(v, 0)
            for j in range(1, t + 1):
                acc = acc + mix_row(t, j) * slab(v, j)
            out.append(acc)
        return jnp.concatenate(out, axis=0)

    def conv_taps(u):
        full = [st_ref[:, j, :] for j in range(CONV_W - 1)] + [
            slab(u, t) for t in range(n_t)]
        return (jnp.concatenate(full[1:1 + n_t], axis=0),
                jnp.concatenate(full[0:n_t], axis=0))

    weights_ready, drain = _weight_stream(mm_f32_hbm, mm_bf16, mm_bf16_hbm, stage_ref,
                                          in_sem, out_sem)
    y, v, u = _layer_tile(gather(x_ref), gather(p_ref), w, spatial_mix, conv_taps,
                          weights_ready)
    scatter(y_ref, y)
    scatter(vrow_ref, v)
    scatter(conv_ref, u[(n_t - (CONV_W - 1)) * n_seq:, :])
    drain()


def _resident(shape):
    return pl.BlockSpec(shape, lambda *_: (0,) * len(shape),
                        pipeline_mode=pl.Buffered(1))


def kernel(x_prompt, x_sample, state_conv, p_prompt, p_sample, norm_g, w_in, ln_v_g,
           ln_v_b, w_s, b_s, conv_w, w_a_out, w_b_out, w_o, pe_norm_g, w_pe_gate,
           w_pe_proj, final_norm_g):
    assert w_in.shape[0] == 1, "single-layer step only"
    batch, seq, d = x_prompt.shape
    n_seq, n_t, _ = x_sample.shape
    p_dim = p_prompt.shape[-1]
    tm = PROMPT_TILE
    assert seq % tm == 0 and tm % CHUNK == 0 and d == N_HEADS * HEAD_W
    assert CONV_W - 1 <= n_t <= CHUNK
    f32, bf = jnp.float32, jnp.bfloat16

    row2 = lambda a: a.reshape(1, -1)
    small = dict(norm_g=row2(norm_g[0]), ln_g=row2(ln_v_g[0]), ln_b=row2(ln_v_b[0]),
                 mix=w_s[0], mix_b=b_s[0], conv_w=jnp.transpose(conv_w, (1, 0, 2)),
                 pe_g=row2(pe_norm_g[0]), fin_g=row2(final_norm_g))
    mm_f32 = dict(w_in=w_in[0], w_a=w_a_out[0], w_b=w_b_out[0], w_o=w_o[0],
                  w_pg=w_pe_gate[0], w_pp=w_pe_proj[0])
    mm_in = tuple(mm_f32[k] for k in MATMUL_NAMES)

    def specs(arrs):
        return [_resident(a.shape) for a in arrs]

    hbm = pl.BlockSpec(memory_space=pltpu.HBM)

    sample_in = (x_sample, p_sample[0], state_conv[0]) + tuple(small[k] for k in SMALL_NAMES)
    out_shapes = ((n_seq, n_t, d), (n_seq, CONV_W - 1, d), (n_seq, n_t, d), (CHUNK, d))
    outs = pl.pallas_call(
        _sample_kernel,
        grid=(1,),
        in_specs=specs(sample_in) + [hbm] * N_MM,
        out_specs=[_resident(s) for s in out_shapes] + [hbm] * N_MM,
        out_shape=[jax.ShapeDtypeStruct(s, f32) for s in out_shapes]
        + [jax.ShapeDtypeStruct(a.shape, bf) for a in mm_in],
        scratch_shapes=[pltpu.VMEM(a.shape, bf) for a in mm_in]
        + [pltpu.VMEM((STAGE_SLOTS, STAGE_ROWS, STAGE_COLS), f32),
           pltpu.SemaphoreType.DMA((STAGE_SLOTS,)),
           pltpu.SemaphoreType.DMA((len(WEIGHT_USE_ORDER),))],
        compiler_params=pltpu.CompilerParams(
            dimension_semantics=("arbitrary",),
            vmem_limit_bytes=V7X_VMEM_LIMIT_BYTES),
        name="sample_layer",
    )(*sample_in, *mm_in)
    y_s, conv_s, v_s, bias_tile = outs[:4]
    mm_bf16 = tuple(outs[4:])

    prompt_w = tuple(dict(small, mix_b=bias_tile)[k] for k in SMALL_NAMES) + mm_bf16
    y_p, conv_p, v_p = pl.pallas_call(
        _prompt_kernel,
        grid=(batch, seq // tm),
        in_specs=[pl.BlockSpec((1, tm, d), lambda b, i: (b, i, 0)),
                  pl.BlockSpec((1, tm, p_dim), lambda b, i: (b, i, 0))] + specs(prompt_w),
        out_specs=[pl.BlockSpec((1, tm, d), lambda b, i: (b, i, 0)),
                   pl.BlockSpec((1, 1, CONV_W - 1, d), lambda b, i: (0, b, 0, 0)),
                   pl.BlockSpec((1, 1, CHUNK, d), lambda b, i: (0, b, 0, 0))],
        out_shape=[jax.ShapeDtypeStruct((batch, seq, d), f32),
                   jax.ShapeDtypeStruct((1, batch, CONV_W - 1, d), f32),
                   jax.ShapeDtypeStruct((1, batch, CHUNK, d), f32)],
        scratch_shapes=[pltpu.VMEM((tm + CARRY_ROWS, d), f32)],
        compiler_params=pltpu.CompilerParams(
            dimension_semantics=("arbitrary", "arbitrary"),
            vmem_limit_bytes=V7X_VMEM_LIMIT_BYTES),
        name="prompt_layer",
    )(x_prompt, p_prompt[0], *prompt_w)

    return (y_p, y_s, conv_p, conv_s[None], v_p, v_s[None])
```

```python
import math

import jax
import jax.numpy as jnp
from jax import lax
from jax.experimental import pallas as pl
from jax.experimental.pallas import tpu as pltpu

CHUNK = 128
N_HEADS = 8
HEAD_W = 128
CONV_W = 3
EPS = 1e-6
LN_EPS = 1e-5
COL_U, COL_V, COL_GA, COL_C, COL_B, COL_H, COL_GB, COL_MA, COL_MB = range(9)

V7X_VMEM_LIMIT_BYTES = 60000 * 1024
PROMPT_TILE = 512
CARRY_ROWS = 8
STAGE_ROWS, STAGE_COLS = 256, 1024
STAGE_SLOTS = 8


def _gelu(x):
    c = math.sqrt(2.0 / math.pi)
    return x * (0.5 * (1.0 + jnp.tanh(c * (x + 0.044715 * (x * x * x)))))


def _sigmoid(x):
    return 0.5 * jnp.tanh(0.5 * x) + 0.5


def _silu(x):
    return x * _sigmoid(x)


def _rms_norm(x, g):
    return x * lax.rsqrt(jnp.mean(x * x, axis=-1, keepdims=True) + EPS) * g


def _layer_norm(x, g, b):
    mu = jnp.mean(x, axis=-1, keepdims=True)
    xc = x - mu
    var = jnp.mean(xc * xc, axis=-1, keepdims=True)
    return xc * lax.rsqrt(var + LN_EPS) * g + b


def _dot(a, b):
    return jnp.dot(a, b, preferred_element_type=jnp.float32)


SMALL_NAMES = ("norm_g", "ln_g", "ln_b", "mix", "mix_b", "conv_w", "pe_g", "fin_g")
MATMUL_NAMES = ("w_in", "w_a", "w_b", "w_o", "w_pg", "w_pp")
N_SMALL, N_MM = len(SMALL_NAMES), len(MATMUL_NAMES)
WEIGHT_USE_ORDER = (
    ("w_in", COL_V), ("w_in", COL_U), ("w_in", COL_GA), ("w_in", COL_C), ("w_in", COL_H),
    ("w_in", COL_B), ("w_in", COL_GB), ("w_in", COL_MA), "w_a", ("w_in", COL_MB), "w_b",
    "w_o", "w_pg", "w_pp")


def _layer_tile(x, p, w, spatial_mix, conv_taps, weights_ready=lambda key: None):
    d = x.shape[-1]
    bf = jnp.bfloat16
    h = _rms_norm(x, w["norm_g"][...]).astype(bf)

    def proj(col):
        weights_ready(("w_in", col))
        return _dot(h, w["w_in"][:, col * d:(col + 1) * d])

    def mm(a, name):
        weights_ready(name)
        return _dot(a.astype(bf), w[name][...])

    v = _layer_norm(_gelu(proj(COL_V)), w["ln_g"][...], w["ln_b"][...])
    y_a = _gelu(proj(COL_U)) * spatial_mix(v) * _silu(proj(COL_GA))

    u = proj(COL_C) * proj(COL_H)
    u1, u2 = conv_taps(u)
    cw = w["conv_w"]
    conv = cw[0] * u2 + cw[1] * u1 + cw[2] * u
    y_b = proj(COL_B) * conv * _silu(proj(COL_GB))

    merged = (_sigmoid(proj(COL_MA)) * mm(y_a, "w_a")
              + _sigmoid(proj(COL_MB)) * mm(y_b, "w_b"))
    x = x + mm(merged, "w_o")
    gate = _sigmoid(mm(_rms_norm(x, w["pe_g"][...]), "w_pg"))
    x = x + gate * mm(p, "w_pp")
    return _rms_norm(x, w["fin_g"][...]), v, u


def _weight_stream(f32_hbm, bf16_vmem, bf16_hbm, stage_ref, in_sem, out_sem):
    index = dict(zip(MATMUL_NAMES, range(N_MM)))
    chunks, block_end, block_cols = [], [], []
    for key in WEIGHT_USE_ORDER:
        name, col = key if isinstance(key, tuple) else (key, None)
        rows, cols = f32_hbm[index[name]].shape
        rb = min(rows, STAGE_ROWS)
        n_blocks = len([k for k in WEIGHT_USE_ORDER if isinstance(k, tuple) and k[0] == name])
        width = cols if col is None else cols // n_blocks
        assert rows % rb == 0 and width % STAGE_COLS == 0
        c_lo = 0 if col is None else col * width
        for c0 in range(c_lo, c_lo + width, STAGE_COLS):
            for r0 in range(0, rows, rb):
                chunks.append((index[name], r0, rb, c0))
        block_end.append(len(chunks))
        block_cols.append((index[name], c_lo, width))
    n_slots = stage_ref.shape[0]
    visited = []

    def fetch(k):
        wi, r0, rb, c0 = chunks[k]
        return pltpu.make_async_copy(
            f32_hbm[wi].at[pl.ds(r0, rb), pl.ds(c0, STAGE_COLS)],
            stage_ref.at[k % n_slots, pl.ds(0, rb), :], in_sem.at[k % n_slots])

    def export(i):
        wi, c_lo, width = block_cols[i]
        return pltpu.make_async_copy(bf16_vmem[wi].at[:, pl.ds(c_lo, width)],
                                     bf16_hbm[wi].at[:, pl.ds(c_lo, width)], out_sem.at[i])

    for k in range(min(n_slots, len(chunks))):
        fetch(k).start(priority=k % 2)

    def weights_ready(key):
        i = len(visited)
        assert key == WEIGHT_USE_ORDER[i], (key, WEIGHT_USE_ORDER[i])
        visited.append(key)
        for k in range(block_end[i - 1] if i else 0, block_end[i]):
            wi, r0, rb, c0 = chunks[k]
            fetch(k).wait()
            bf16_vmem[wi][pl.ds(r0, rb), pl.ds(c0, STAGE_COLS)] = (
                stage_ref[k % n_slots, 0:rb, :].astype(jnp.bfloat16))
            if k + n_slots < len(chunks):
                fetch(k + n_slots).start(priority=(k + n_slots) % 2)
        export(i).start(priority=i % 2)

    def drain():
        assert len(visited) == len(WEIGHT_USE_ORDER)
        for i in range(len(WEIGHT_USE_ORDER)):
            export(i).wait()

    return weights_ready, drain


def _prompt_kernel(x_ref, p_ref, *rest):
    w = dict(zip(SMALL_NAMES + MATMUL_NAMES, rest[:N_SMALL + N_MM]))
    y_ref, conv_ref, vrow_ref, pbuf_ref = rest[N_SMALL + N_MM:]
    tm, d = x_ref.shape[1], x_ref.shape[2]
    bf = jnp.bfloat16

    @pl.when(pl.program_id(1) == 0)
    def _():
        pbuf_ref[0:CARRY_ROWS, :] = jnp.zeros((CARRY_ROWS, d), jnp.float32)

    def spatial_mix(v):
        v_b = v.astype(bf)
        row = lax.broadcasted_iota(jnp.int32, (CHUNK, CHUNK), 0)
        col = lax.broadcasted_iota(jnp.int32, (CHUNK, CHUNK), 1)
        mix = [jnp.where(col <= row, w["mix"][hd], 0.0).astype(bf)
               for hd in range(N_HEADS)]
        s_rows = []
        for c in range(tm // CHUNK):
            s_heads = [_dot(mix[hd], v_b[c * CHUNK:(c + 1) * CHUNK,
                                         hd * HEAD_W:(hd + 1) * HEAD_W])
                       for hd in range(N_HEADS)]
            s_rows.append(jnp.concatenate(s_heads, axis=1) + w["mix_b"][...])
        return jnp.concatenate(s_rows, axis=0)

    def conv_taps(u):
        pbuf_ref[CARRY_ROWS:CARRY_ROWS + tm, :] = u
        return (pbuf_ref[CARRY_ROWS - 1:CARRY_ROWS - 1 + tm, :],
                pbuf_ref[CARRY_ROWS - 2:CARRY_ROWS - 2 + tm, :])

    y, v, _ = _layer_tile(x_ref[0], p_ref[0], w, spatial_mix, conv_taps)
    y_ref[0] = y
    vrow_ref[0, 0] = v[tm - CHUNK:, :]
    tail = pbuf_ref[tm:tm + CARRY_ROWS, :]
    pbuf_ref[0:CARRY_ROWS, :] = tail
    conv_ref[0, 0] = tail[CARRY_ROWS - (CONV_W - 1):, :]


def _sample_kernel(x_ref, p_ref, st_ref, *rest):
    small = rest[:N_SMALL]
    mm_f32_hbm = rest[N_SMALL:N_SMALL + N_MM]
    outs = rest[N_SMALL + N_MM:]
    y_ref, conv_ref, vrow_ref, bias_ref = outs[:4]
    mm_bf16_hbm = outs[4:4 + N_MM]
    mm_bf16 = outs[4 + N_MM:4 + 2 * N_MM]
    stage_ref, in_sem, out_sem = outs[4 + 2 * N_MM:]
    w = dict(zip(SMALL_NAMES + MATMUL_NAMES, small + mm_bf16))
    n_seq, n_t, _ = x_ref.shape

    b_t = jnp.transpose(w["mix_b"][...])
    bias_tile = jnp.concatenate(
        [jnp.broadcast_to(b_t[:, hd:hd + 1], (CHUNK, HEAD_W)) for hd in range(N_HEADS)],
        axis=1)
    bias_ref[...] = bias_tile

    def mix_row(t, j):
        return jnp.concatenate(
            [jnp.broadcast_to(w["mix"][hd, t:t + 1, j:j + 1], (1, HEAD_W))
             for hd in range(N_HEADS)], axis=1)

    def slab(a, t):
        return a[t * n_seq:(t + 1) * n_seq, :]

    def gather(ref):
        return jnp.concatenate([ref[:, t, :] for t in range(ref.shape[1])], axis=0)

    def scatter(ref, a):
        for t in range(ref.shape[1]):
            ref[:, t, :] = slab(a, t)

    def spatial_mix(v):
        out = []
        for t in range(n_t):
            acc = bias_tile[t:t + 1, :] + mix_row(t, 0) * slab(v, 0)
            for j in range(1, t + 1):
                acc = acc + mix_row(t, j) * slab(v, j)
            out.append(acc)
        return jnp.concatenate(out, axis=0)

    def conv_taps(u):
        full = [st_ref[:, j, :] for j in range(CONV_W - 1)] + [
            slab(u, t) for t in range(n_t)]
        return (jnp.concatenate(full[1:1 + n_t], axis=0),
                jnp.concatenate(full[0:n_t], axis=0))

    weights_ready, drain = _weight_stream(mm_f32_hbm, mm_bf16, mm_bf16_hbm, stage_ref,
                                          in_sem, out_sem)
    y, v, u = _layer_tile(gather(x_ref), gather(p_ref), w, spatial_mix, conv_taps,
                          weights_ready)
    scatter(y_ref, y)
    scatter(vrow_ref, v)
    scatter(conv_ref, u[(n_t - (CONV_W - 1)) * n_seq:, :])
    drain()


def _resident(shape):
    return pl.BlockSpec(shape, lambda *_: (0,) * len(shape),
                        pipeline_mode=pl.Buffered(1))


def kernel(x_prompt, x_sample, state_conv, p_prompt, p_sample, norm_g, w_in, ln_v_g,
           ln_v_b, w_s, b_s, conv_w, w_a_out, w_b_out, w_o, pe_norm_g, w_pe_gate,
           w_pe_proj, final_norm_g):
    assert w_in.shape[0] == 1, "single-layer step only"
    batch, seq, d = x_prompt.shape
    n_seq, n_t, _ = x_sample.shape
    p_dim = p_prompt.shape[-1]
    tm = PROMPT_TILE
    assert seq % tm == 0 and tm % CHUNK == 0 and d == N_HEADS * HEAD_W
    assert CONV_W - 1 <= n_t <= CHUNK
    f32, bf = jnp.float32, jnp.bfloat16

    row2 = lambda a: a.reshape(1, -1)
    small = dict(norm_g=row2(norm_g[0]), ln_g=row2(ln_v_g[0]), ln_b=row2(ln_v_b[0]),
                 mix=w_s[0], mix_b=b_s[0], conv_w=jnp.transpose(conv_w, (1, 0, 2)),
                 pe_g=row2(pe_norm_g[0]), fin_g=row2(final_norm_g))
    mm_f32 = dict(w_in=w_in[0], w_a=w_a_out[0], w_b=w_b_out[0], w_o=w_o[0],
                  w_pg=w_pe_gate[0], w_pp=w_pe_proj[0])
    mm_in = tuple(mm_f32[k] for k in MATMUL_NAMES)

    def specs(arrs):
        return [_resident(a.shape) for a in arrs]

    hbm = pl.BlockSpec(memory_space=pltpu.HBM)

    sample_in = (x_sample, p_sample[0], state_conv[0]) + tuple(small[k] for k in SMALL_NAMES)
    out_shapes = ((n_seq, n_t, d), (n_seq, CONV_W - 1, d), (n_seq, n_t, d), (CHUNK, d))
    outs = pl.pallas_call(
        _sample_kernel,
        grid=(1,),
        in_specs=specs(sample_in) + [hbm] * N_MM,
        out_specs=[_resident(s) for s in out_shapes] + [hbm] * N_MM,
        out_shape=[jax.ShapeDtypeStruct(s, f32) for s in out_shapes]
        + [jax.ShapeDtypeStruct(a.shape, bf) for a in mm_in],
        scratch_shapes=[pltpu.VMEM(a.shape, bf) for a in mm_in]
        + [pltpu.VMEM((STAGE_SLOTS, STAGE_ROWS, STAGE_COLS), f32),
           pltpu.SemaphoreType.DMA((STAGE_SLOTS,)),
           pltpu.SemaphoreType.DMA((len(WEIGHT_USE_ORDER),))],
        compiler_params=pltpu.CompilerParams(
            dimension_semantics=("arbitrary",),
            vmem_limit_bytes=V7X_VMEM_LIMIT_BYTES),
        name="sample_layer",
    )(*sample_in, *mm_in)
    y_s, conv_s, v_s, bias_tile = outs[:4]
    mm_bf16 = tuple(outs[4:])

    prompt_w = tuple(dict(small, mix_b=bias_tile)[k] for k in SMALL_NAMES) + mm_bf16
    y_p, conv_p, v_p = pl.pallas_call(
        _prompt_kernel,
        grid=(batch, seq // tm),
        in_specs=[pl.BlockSpec((1, tm, d), lambda b, i: (b, i, 0)),
                  pl.BlockSpec((1, tm, p_dim), lambda b, i: (b, i, 0))] + specs(prompt_w),
        out_specs=[pl.BlockSpec((1, tm, d), lambda b, i: (b, i, 0)),
                   pl.BlockSpec((1, 1, CONV_W - 1, d), lambda b, i: (0, b, 0, 0)),
                   pl.BlockSpec((1, 1, CHUNK, d), lambda b, i: (0, b, 0, 0))],
        out_shape=[jax.ShapeDtypeStruct((batch, seq, d), f32),
                   jax.ShapeDtypeStruct((1, batch, CONV_W - 1, d), f32),
                   jax.ShapeDtypeStruct((1, batch, CHUNK, d), f32)],
        scratch_shapes=[pltpu.VMEM((tm + CARRY_ROWS, d), f32)],
        compiler_params=pltpu.CompilerParams(
            dimension_semantics=("arbitrary", "arbitrary"),
            vmem_limit_bytes=V7X_VMEM_LIMIT_BYTES),
        name="prompt_layer",
    )(x_prompt, p_prompt[0], *prompt_w)

    return (y_p, y_s, conv_p, conv_s[None], v_p, v_s[None])
```

```python
import math

import jax
import jax.numpy as jnp
from jax import lax
from jax.experimental import pallas as pl
from jax.experimental.pallas import tpu as pltpu

CHUNK = 128
N_HEADS = 8
HEAD_W = 128
CONV_W = 3
EPS = 1e-6
LN_EPS = 1e-5
COL_U, COL_V, COL_GA, COL_C, COL_B, COL_H, COL_GB, COL_MA, COL_MB = range(9)

V7X_VMEM_LIMIT_BYTES = 60000 * 1024
PROMPT_TILE = 512
CARRY_ROWS = 8
STAGE_ROWS, STAGE_COLS = 256, 1024
STAGE_SLOTS = 8


def _gelu(x):
    c = math.sqrt(2.0 / math.pi)
    hx = 0.5 * x
    return hx + hx * jnp.tanh(x * (c + (c * 0.044715) * (x * x)))


def _sigmoid(x):
    return 0.5 * jnp.tanh(0.5 * x) + 0.5


def _silu(x):
    hx = 0.5 * x
    return hx + hx * jnp.tanh(hx)


def _rms_norm(x, g):
    return x * lax.rsqrt(jnp.mean(x * x, axis=-1, keepdims=True) + EPS) * g


def _layer_norm(x, g, b):
    mu = jnp.mean(x, axis=-1, keepdims=True)
    xc = x - mu
    var = jnp.mean(xc * xc, axis=-1, keepdims=True)
    return xc * lax.rsqrt(var + LN_EPS) * g + b


def _dot(a, b):
    return jnp.dot(a, b, preferred_element_type=jnp.float32)


SMALL_NAMES = ("norm_g", "ln_g", "ln_b", "mix", "mix_b", "conv_w", "pe_g", "fin_g")
MATMUL_NAMES = ("w_in", "w_a", "w_b", "w_o", "w_pg", "w_pp")
N_SMALL, N_MM = len(SMALL_NAMES), len(MATMUL_NAMES)
WEIGHT_USE_ORDER = (
    ("w_in", COL_V), ("w_in", COL_U), ("w_in", COL_GA), ("w_in", COL_C), ("w_in", COL_H),
    ("w_in", COL_B), ("w_in", COL_GB), ("w_in", COL_MA), "w_a", ("w_in", COL_MB), "w_b",
    "w_o", "w_pg", "w_pp")


def _layer_tile(x, p, w, spatial_mix, conv_taps, weights_ready=lambda key: None):
    d = x.shape[-1]
    bf = jnp.bfloat16
    h = _rms_norm(x, w["norm_g"][...]).astype(bf)

    def proj(col):
        weights_ready(("w_in", col))
        return _dot(h, w["w_in"][:, col * d:(col + 1) * d])

    def mm(a, name):
        weights_ready(name)
        return _dot(a.astype(bf), w[name][...])

    v = _layer_norm(_gelu(proj(COL_V)), w["ln_g"][...], w["ln_b"][...])
    y_a = _gelu(proj(COL_U)) * spatial_mix(v) * _silu(proj(COL_GA))

    u = proj(COL_C) * proj(COL_H)
    u1, u2 = conv_taps(u)
    cw = w["conv_w"]
    conv = cw[0] * u2 + cw[1] * u1 + cw[2] * u
    y_b = proj(COL_B) * conv * _silu(proj(COL_GB))

    merged = (_sigmoid(proj(COL_MA)) * mm(y_a, "w_a")
              + _sigmoid(proj(COL_MB)) * mm(y_b, "w_b"))
    x = x + mm(merged, "w_o")
    inv_rms = lax.rsqrt(jnp.mean(x * x, axis=-1, keepdims=True) + EPS)
    gate = _sigmoid(inv_rms * mm(x * w["pe_g"][...], "w_pg"))
    x = x + gate * mm(p, "w_pp")
    return _rms_norm(x, w["fin_g"][...]), v, u


def _weight_stream(f32_hbm, bf16_vmem, bf16_hbm, stage_ref, in_sem, out_sem):
    index = dict(zip(MATMUL_NAMES, range(N_MM)))
    chunks, block_end, block_cols = [], [], []
    for key in WEIGHT_USE_ORDER:
        name, col = key if isinstance(key, tuple) else (key, None)
        rows, cols = f32_hbm[index[name]].shape
        rb = min(rows, STAGE_ROWS)
        n_blocks = len([k for k in WEIGHT_USE_ORDER if isinstance(k, tuple) and k[0] == name])
        width = cols if col is None else cols // n_blocks
        assert rows % rb == 0 and width % STAGE_COLS == 0
        c_lo = 0 if col is None else col * width
        for c0 in range(c_lo, c_lo + width, STAGE_COLS):
            for r0 in range(0, rows, rb):
                chunks.append((index[name], r0, rb, c0))
        block_end.append(len(chunks))
        block_cols.append((index[name], c_lo, width))
    n_slots = stage_ref.shape[0]
    visited = []

    def fetch(k):
        wi, r0, rb, c0 = chunks[k]
        return pltpu.make_async_copy(
            f32_hbm[wi].at[pl.ds(r0, rb), pl.ds(c0, STAGE_COLS)],
            stage_ref.at[k % n_slots, pl.ds(0, rb), :], in_sem.at[k % n_slots])

    def export(i):
        wi, c_lo, width = block_cols[i]
        return pltpu.make_async_copy(bf16_vmem[wi].at[:, pl.ds(c_lo, width)],
                                     bf16_hbm[wi].at[:, pl.ds(c_lo, width)], out_sem.at[i])

    for k in range(min(n_slots, len(chunks))):
        fetch(k).start()

    def weights_ready(key):
        i = len(visited)
        assert key == WEIGHT_USE_ORDER[i], (key, WEIGHT_USE_ORDER[i])
        visited.append(key)
        for k in range(block_end[i - 1] if i else 0, block_end[i]):
            wi, r0, rb, c0 = chunks[k]
            fetch(k).wait()
            bf16_vmem[wi][pl.ds(r0, rb), pl.ds(c0, STAGE_COLS)] = (
                stage_ref[k % n_slots, 0:rb, :].astype(jnp.bfloat16))
            if k + n_slots < len(chunks):
                fetch(k + n_slots).start()
        export(i).start()

    def drain():
        assert len(visited) == len(WEIGHT_USE_ORDER)
        for i in range(len(WEIGHT_USE_ORDER)):
            export(i).wait()

    return weights_ready, drain


def _prompt_kernel(x_ref, p_ref, *rest):
    w = dict(zip(SMALL_NAMES + MATMUL_NAMES, rest[:N_SMALL + N_MM]))
    y_ref, conv_ref, vrow_ref, pbuf_ref = rest[N_SMALL + N_MM:]
    tm, d = x_ref.shape[1], x_ref.shape[2]
    bf = jnp.bfloat16

    @pl.when(pl.program_id(1) == 0)
    def _():
        pbuf_ref[0:CARRY_ROWS, :] = jnp.zeros((CARRY_ROWS, d), jnp.float32)

    def spatial_mix(v):
        v_b = v.astype(bf)
        row = lax.broadcasted_iota(jnp.int32, (CHUNK, CHUNK), 0)
        col = lax.broadcasted_iota(jnp.int32, (CHUNK, CHUNK), 1)
        mix = [jnp.where(col <= row, w["mix"][hd], 0.0).astype(bf)
               for hd in range(N_HEADS)]
        s_rows = []
        for c in range(tm // CHUNK):
            s_heads = [_dot(mix[hd], v_b[c * CHUNK:(c + 1) * CHUNK,
                                         hd * HEAD_W:(hd + 1) * HEAD_W])
                       for hd in range(N_HEADS)]
            s_rows.append(jnp.concatenate(s_heads, axis=1) + w["mix_b"][...])
        return jnp.concatenate(s_rows, axis=0)

    def conv_taps(u):
        pbuf_ref[CARRY_ROWS:CARRY_ROWS + tm, :] = u
        return (pbuf_ref[CARRY_ROWS - 1:CARRY_ROWS - 1 + tm, :],
                pbuf_ref[CARRY_ROWS - 2:CARRY_ROWS - 2 + tm, :])

    y, v, _ = _layer_tile(x_ref[0], p_ref[0], w, spatial_mix, conv_taps)
    y_ref[0] = y
    vrow_ref[0, 0] = v[tm - CHUNK:, :]
    tail = pbuf_ref[tm:tm + CARRY_ROWS, :]
    pbuf_ref[0:CARRY_ROWS, :] = tail
    conv_ref[0, 0] = tail[CARRY_ROWS - (CONV_W - 1):, :]


def _sample_kernel(x_ref, p_ref, st_ref, *rest):
    small = rest[:N_SMALL]
    mm_f32_hbm = rest[N_SMALL:N_SMALL + N_MM]
    outs = rest[N_SMALL + N_MM:]
    y_ref, conv_ref, vrow_ref, bias_ref = outs[:4]
    mm_bf16_hbm = outs[4:4 + N_MM]
    mm_bf16 = outs[4 + N_MM:4 + 2 * N_MM]
    stage_ref, in_sem, out_sem = outs[4 + 2 * N_MM:]
    w = dict(zip(SMALL_NAMES + MATMUL_NAMES, small + mm_bf16))
    n_seq, n_t, _ = x_ref.shape

    b_t = jnp.transpose(w["mix_b"][...])
    bias_tile = jnp.concatenate(
        [jnp.broadcast_to(b_t[:, hd:hd + 1], (CHUNK, HEAD_W)) for hd in range(N_HEADS)],
        axis=1)
    bias_ref[...] = bias_tile

    def mix_row(t, j):
        return jnp.concatenate(
            [jnp.broadcast_to(w["mix"][hd, t:t + 1, j:j + 1], (1, HEAD_W))
             for hd in range(N_HEADS)], axis=1)

    def slab(a, t):
        return a[t * n_seq:(t + 1) * n_seq, :]

    def gather(ref):
        return jnp.concatenate([ref[:, t, :] for t in range(ref.shape[1])], axis=0)

    def scatter(ref, a):
        for t in range(ref.shape[1]):
            ref[:, t, :] = slab(a, t)

    def spatial_mix(v):
        out = []
        for t in range(n_t):
            acc = bias_tile[t:t + 1, :] + mix_row(t, 0) * slab(v, 0)
            for j in range(1, t + 1):
                acc = acc + mix_row(t, j) * slab(v, j)
            out.append(acc)
        return jnp.concatenate(out, axis=0)

    def conv_taps(u):
        full = [st_ref[:, j, :] for j in range(CONV_W - 1)] + [
            slab(u, t) for t in range(n_t)]
        return (jnp.concatenate(full[1:1 + n_t], axis=0),
                jnp.concatenate(full[0:n_t], axis=0))

    weights_ready, drain = _weight_stream(mm_f32_hbm, mm_bf16, mm_bf16_hbm, stage_ref,
                                          in_sem, out_sem)
    y, v, u = _layer_tile(gather(x_ref), gather(p_ref), w, spatial_mix, conv_taps,
                          weights_ready)
    scatter(y_ref, y)
    scatter(vrow_ref, v)
    scatter(conv_ref, u[(n_t - (CONV_W - 1)) * n_seq:, :])
    drain()


def _resident(shape):
    return pl.BlockSpec(shape, lambda *_: (0,) * len(shape),
                        pipeline_mode=pl.Buffered(1))


def kernel(x_prompt, x_sample, state_conv, p_prompt, p_sample, norm_g, w_in, ln_v_g,
           ln_v_b, w_s, b_s, conv_w, w_a_out, w_b_out, w_o, pe_norm_g, w_pe_gate,
           w_pe_proj, final_norm_g):
    assert w_in.shape[0] == 1, "single-layer step only"
    batch, seq, d = x_prompt.shape
    n_seq, n_t, _ = x_sample.shape
    p_dim = p_prompt.shape[-1]
    tm = PROMPT_TILE
    assert seq % tm == 0 and tm % CHUNK == 0 and d == N_HEADS * HEAD_W
    assert CONV_W - 1 <= n_t <= CHUNK
    f32, bf = jnp.float32, jnp.bfloat16

    row2 = lambda a: a.reshape(1, -1)
    small = dict(norm_g=row2(norm_g[0]), ln_g=row2(ln_v_g[0]), ln_b=row2(ln_v_b[0]),
                 mix=w_s[0], mix_b=b_s[0], conv_w=jnp.transpose(conv_w, (1, 0, 2)),
                 pe_g=row2(pe_norm_g[0]), fin_g=row2(final_norm_g))
    mm_f32 = dict(w_in=w_in[0], w_a=w_a_out[0], w_b=w_b_out[0], w_o=w_o[0],
                  w_pg=w_pe_gate[0], w_pp=w_pe_proj[0])
    mm_in = tuple(mm_f32[k] for k in MATMUL_NAMES)

    def specs(arrs):
        return [_resident(a.shape) for a in arrs]

    hbm = pl.BlockSpec(memory_space=pltpu.HBM)

    sample_in = (x_sample, p_sample[0], state_conv[0]) + tuple(small[k] for k in SMALL_NAMES)
    out_shapes = ((n_seq, n_t, d), (n_seq, CONV_W - 1, d), (n_seq, n_t, d), (CHUNK, d))
    outs = pl.pallas_call(
        _sample_kernel,
        grid=(1,),
        in_specs=specs(sample_in) + [hbm] * N_MM,
        out_specs=[_resident(s) for s in out_shapes] + [hbm] * N_MM,
        out_shape=[jax.ShapeDtypeStruct(s, f32) for s in out_shapes]
        + [jax.ShapeDtypeStruct(a.shape, bf) for a in mm_in],
        scratch_shapes=[pltpu.VMEM(a.shape, bf) for a in mm_in]
        + [pltpu.VMEM((STAGE_SLOTS, STAGE_ROWS, STAGE_COLS), f32),
           pltpu.SemaphoreType.DMA((STAGE_SLOTS,)),
           pltpu.SemaphoreType.DMA((len(WEIGHT_USE_ORDER),))],
        compiler_params=pltpu.CompilerParams(
            dimension_semantics=("arbitrary",),
            vmem_limit_bytes=V7X_VMEM_LIMIT_BYTES),
        name="sample_layer",
    )(*sample_in, *mm_in)
    y_s, conv_s, v_s, bias_tile = outs[:4]
    mm_bf16 = tuple(outs[4:])

    prompt_w = tuple(dict(small, mix_b=bias_tile)[k] for k in SMALL_NAMES) + mm_bf16
    y_p, conv_p, v_p = pl.pallas_call(
        _prompt_kernel,
        grid=(batch, seq // tm),
        in_specs=[pl.BlockSpec((1, tm, d), lambda b, i: (b, i, 0)),
                  pl.BlockSpec((1, tm, p_dim), lambda b, i: (b, i, 0))] + specs(prompt_w),
        out_specs=[pl.BlockSpec((1, tm, d), lambda b, i: (b, i, 0)),
                   pl.BlockSpec((1, 1, CONV_W - 1, d), lambda b, i: (0, b, 0, 0)),
                   pl.BlockSpec((1, 1, CHUNK, d), lambda b, i: (0, b, 0, 0))],
        out_shape=[jax.ShapeDtypeStruct((batch, seq, d), f32),
                   jax.ShapeDtypeStruct((1, batch, CONV_W - 1, d), f32),
                   jax.ShapeDtypeStruct((1, batch, CHUNK, d), f32)],
        scratch_shapes=[pltpu.VMEM((tm + CARRY_ROWS, d), f32)],
        compiler_params=pltpu.CompilerParams(
            dimension_semantics=("arbitrary", "arbitrary"),
            vmem_limit_bytes=V7X_VMEM_LIMIT_BYTES),
        name="prompt_layer",
    )(x_prompt, p_prompt[0], *prompt_w)

    return (y_p, y_s, conv_p, conv_s[None], v_p, v_s[None])
```

```python
import math

import jax
import jax.numpy as jnp
from jax import lax
from jax.experimental import pallas as pl
from jax.experimental.pallas import tpu as pltpu

CHUNK = 128
N_HEADS = 8
HEAD_W = 128
CONV_W = 3
EPS = 1e-6
LN_EPS = 1e-5
COL_U, COL_V, COL_GA, COL_C, COL_B, COL_H, COL_GB, COL_MA, COL_MB = range(9)

V7X_VMEM_LIMIT_BYTES = 60000 * 1024
PROMPT_TILE = 512
CARRY_ROWS = 8
STAGE_ROWS, STAGE_COLS = 256, 1024
STAGE_SLOTS = 8


def _gelu(x):
    c = math.sqrt(2.0 / math.pi)
    hx = 0.5 * x
    return hx + hx * jnp.tanh(x * (c + (c * 0.044715) * (x * x)))


def _sigmoid_of_twice(hx):
    return 0.5 * jnp.tanh(hx) + 0.5


def _silu_of_twice(hx):
    return hx + hx * jnp.tanh(hx)


def _rms_norm(x, g):
    return x * lax.rsqrt(jnp.mean(x * x, axis=-1, keepdims=True) + EPS) * g


def _layer_norm(x, g, b):
    mu = jnp.mean(x, axis=-1, keepdims=True)
    xc = x - mu
    var = jnp.mean(xc * xc, axis=-1, keepdims=True)
    return xc * lax.rsqrt(var + LN_EPS) * g + b


def _dot(a, b):
    return jnp.dot(a, b, preferred_element_type=jnp.float32)


SMALL_NAMES = ("norm_g", "ln_g", "ln_b", "mix", "mix_b", "conv_w", "pe_g", "fin_g")
MATMUL_NAMES = ("w_in", "w_a", "w_b", "w_o", "w_pg", "w_pp")
N_SMALL, N_MM = len(SMALL_NAMES), len(MATMUL_NAMES)
WEIGHT_USE_ORDER = (
    ("w_in", COL_V), ("w_in", COL_U), ("w_in", COL_GA), ("w_in", COL_C), ("w_in", COL_H),
    ("w_in", COL_B), ("w_in", COL_GB), ("w_in", COL_MA), "w_a", ("w_in", COL_MB), "w_b",
    "w_o", "w_pg", "w_pp")
HALVED_BLOCKS = (("w_in", COL_GA), ("w_in", COL_GB), ("w_in", COL_MA), ("w_in", COL_MB),
                 "w_pg")


def _layer_tile(x, p, w, spatial_mix, conv_taps, weights_ready=lambda key: None):
    d = x.shape[-1]
    bf = jnp.bfloat16
    h = _rms_norm(x, w["norm_g"][...]).astype(bf)

    def proj(col):
        weights_ready(("w_in", col))
        return _dot(h, w["w_in"][:, col * d:(col + 1) * d])

    def mm(a, name):
        weights_ready(name)
        return _dot(a.astype(bf), w[name][...])

    v = _layer_norm(_gelu(proj(COL_V)), w["ln_g"][...], w["ln_b"][...])
    y_a = _gelu(proj(COL_U)) * spatial_mix(v) * _silu_of_twice(proj(COL_GA))

    u = proj(COL_C) * proj(COL_H)
    u1, u2 = conv_taps(u)
    cw = w["conv_w"]
    conv = cw[0] * u2 + cw[1] * u1 + cw[2] * u
    y_b = proj(COL_B) * conv * _silu_of_twice(proj(COL_GB))

    merged = (_sigmoid_of_twice(proj(COL_MA)) * mm(y_a, "w_a")
              + _sigmoid_of_twice(proj(COL_MB)) * mm(y_b, "w_b"))
    x = x + mm(merged, "w_o")
    inv_rms = lax.rsqrt(jnp.mean(x * x, axis=-1, keepdims=True) + EPS)
    gate = _sigmoid_of_twice(inv_rms * mm(x * w["pe_g"][...], "w_pg"))
    x = x + gate * mm(p, "w_pp")
    return _rms_norm(x, w["fin_g"][...]), v, u


def _weight_stream(f32_hbm, bf16_vmem, bf16_hbm, stage_ref, in_sem, out_sem):
    index = dict(zip(MATMUL_NAMES, range(N_MM)))
    chunks, block_end, block_cols = [], [], []
    for key in WEIGHT_USE_ORDER:
        name, col = key if isinstance(key, tuple) else (key, None)
        rows, cols = f32_hbm[index[name]].shape
        rb = min(rows, STAGE_ROWS)
        n_blocks = len([k for k in WEIGHT_USE_ORDER if isinstance(k, tuple) and k[0] == name])
        width = cols if col is None else cols // n_blocks
        assert rows % rb == 0 and width % STAGE_COLS == 0
        c_lo = 0 if col is None else col * width
        for c0 in range(c_lo, c_lo + width, STAGE_COLS):
            for r0 in range(0, rows, rb):
                chunks.append((index[name], r0, rb, c0))
        block_end.append(len(chunks))
        block_cols.append((index[name], c_lo, width))
    n_slots = stage_ref.shape[0]
    visited = []

    def fetch(k):
        wi, r0, rb, c0 = chunks[k]
        return pltpu.make_async_copy(
            f32_hbm[wi].at[pl.ds(r0, rb), pl.ds(c0, STAGE_COLS)],
            stage_ref.at[k % n_slots, pl.ds(0, rb), :], in_sem.at[k % n_slots])

    def export(i):
        wi, c_lo, width = block_cols[i]
        return pltpu.make_async_copy(bf16_vmem[wi].at[:, pl.ds(c_lo, width)],
                                     bf16_hbm[wi].at[:, pl.ds(c_lo, width)], out_sem.at[i])

    for k in range(min(n_slots, len(chunks))):
        fetch(k).start()

    def weights_ready(key):
        i = len(visited)
        assert key == WEIGHT_USE_ORDER[i], (key, WEIGHT_USE_ORDER[i])
        visited.append(key)
        for k in range(block_end[i - 1] if i else 0, block_end[i]):
            wi, r0, rb, c0 = chunks[k]
            fetch(k).wait()
            chunk = stage_ref[k % n_slots, 0:rb, :]
            if key in HALVED_BLOCKS:
                chunk = 0.5 * chunk
            bf16_vmem[wi][pl.ds(r0, rb), pl.ds(c0, STAGE_COLS)] = chunk.astype(jnp.bfloat16)
            if k + n_slots < len(chunks):
                fetch(k + n_slots).start()
        export(i).start()

    def drain():
        assert len(visited) == len(WEIGHT_USE_ORDER)
        for i in range(len(WEIGHT_USE_ORDER)):
            export(i).wait()

    return weights_ready, drain


def _prompt_kernel(x_ref, p_ref, *rest):
    w = dict(zip(SMALL_NAMES + MATMUL_NAMES, rest[:N_SMALL + N_MM]))
    y_ref, conv_ref, vrow_ref, pbuf_ref = rest[N_SMALL + N_MM:]
    tm, d = x_ref.shape[1], x_ref.shape[2]
    bf = jnp.bfloat16

    @pl.when(pl.program_id(1) == 0)
    def _():
        pbuf_ref[0:CARRY_ROWS, :] = jnp.zeros((CARRY_ROWS, d), jnp.float32)

    def spatial_mix(v):
        v_b = v.astype(bf)
        row = lax.broadcasted_iota(jnp.int32, (CHUNK, CHUNK), 0)
        col = lax.broadcasted_iota(jnp.int32, (CHUNK, CHUNK), 1)
        mix = [jnp.where(col <= row, w["mix"][hd], 0.0).astype(bf)
               for hd in range(N_HEADS)]
        s_rows = []
        for c in range(tm // CHUNK):
            s_heads = [_dot(mix[hd], v_b[c * CHUNK:(c + 1) * CHUNK,
                                         hd * HEAD_W:(hd + 1) * HEAD_W])
                       for hd in range(N_HEADS)]
            s_rows.append(jnp.concatenate(s_heads, axis=1) + w["mix_b"][...])
        return jnp.concatenate(s_rows, axis=0)

    def conv_taps(u):
        pbuf_ref[CARRY_ROWS:CARRY_ROWS + tm, :] = u
        return (pbuf_ref[CARRY_ROWS - 1:CARRY_ROWS - 1 + tm, :],
                pbuf_ref[CARRY_ROWS - 2:CARRY_ROWS - 2 + tm, :])

    y, v, _ = _layer_tile(x_ref[0], p_ref[0], w, spatial_mix, conv_taps)
    y_ref[0] = y
    vrow_ref[0, 0] = v[tm - CHUNK:, :]
    tail = pbuf_ref[tm:tm + CARRY_ROWS, :]
    pbuf_ref[0:CARRY_ROWS, :] = tail
    conv_ref[0, 0] = tail[CARRY_ROWS - (CONV_W - 1):, :]


def _sample_kernel(x_ref, p_ref, st_ref, *rest):
    small = rest[:N_SMALL]
    mm_f32_hbm = rest[N_SMALL:N_SMALL + N_MM]
    outs = rest[N_SMALL + N_MM:]
    y_ref, conv_ref, vrow_ref, bias_ref = outs[:4]
    mm_bf16_hbm = outs[4:4 + N_MM]
    mm_bf16 = outs[4 + N_MM:4 + 2 * N_MM]
    stage_ref, in_sem, out_sem = outs[4 + 2 * N_MM:]
    w = dict(zip(SMALL_NAMES + MATMUL_NAMES, small + mm_bf16))
    n_seq, n_t, _ = x_ref.shape

    b_t = jnp.transpose(w["mix_b"][...])
    bias_tile = jnp.concatenate(
        [jnp.broadcast_to(b_t[:, hd:hd + 1], (CHUNK, HEAD_W)) for hd in range(N_HEADS)],
        axis=1)
    bias_ref[...] = bias_tile

    def mix_row(t, j):
        return jnp.concatenate(
            [jnp.broadcast_to(w["mix"][hd, t:t + 1, j:j + 1], (1, HEAD_W))
             for hd in range(N_HEADS)], axis=1)

    def slab(a, t):
        return a[t * n_seq:(t + 1) * n_seq, :]

    def gather(ref):
        return jnp.concatenate([ref[:, t, :] for t in range(ref.shape[1])], axis=0)

    def scatter(ref, a):
        for t in range(ref.shape[1]):
            ref[:, t, :] = slab(a, t)

    def spatial_mix(v):
        out = []
        for t in range(n_t):
            acc = bias_tile[t:t + 1, :] + mix_row(t, 0) * slab(v, 0)
            for j in range(1, t + 1):
                acc = acc + mix_row(t, j) * slab(v, j)
            out.append(acc)
        return jnp.concatenate(out, axis=0)

    def conv_taps(u):
        full = [st_ref[:, j, :] for j in range(CONV_W - 1)] + [
            slab(u, t) for t in range(n_t)]
        return (jnp.concatenate(full[1:1 + n_t], axis=0),
                jnp.concatenate(full[0:n_t], axis=0))

    weights_ready, drain = _weight_stream(mm_f32_hbm, mm_bf16, mm_bf16_hbm, stage_ref,
                                          in_sem, out_sem)
    y, v, u = _layer_tile(gather(x_ref), gather(p_ref), w, spatial_mix, conv_taps,
                          weights_ready)
    scatter(y_ref, y)
    scatter(vrow_ref, v)
    scatter(conv_ref, u[(n_t - (CONV_W - 1)) * n_seq:, :])
    drain()


def _resident(shape):
    return pl.BlockSpec(shape, lambda *_: (0,) * len(shape),
                        pipeline_mode=pl.Buffered(1))


def kernel(x_prompt, x_sample, state_conv, p_prompt, p_sample, norm_g, w_in, ln_v_g,
           ln_v_b, w_s, b_s, conv_w, w_a_out, w_b_out, w_o, pe_norm_g, w_pe_gate,
           w_pe_proj, final_norm_g):
    assert w_in.shape[0] == 1, "single-layer step only"
    batch, seq, d = x_prompt.shape
    n_seq, n_t, _ = x_sample.shape
    p_dim = p_prompt.shape[-1]
    tm = PROMPT_TILE
    assert seq % tm == 0 and tm % CHUNK == 0 and d == N_HEADS * HEAD_W
    assert CONV_W - 1 <= n_t <= CHUNK
    f32, bf = jnp.float32, jnp.bfloat16

    row2 = lambda a: a.reshape(1, -1)
    small = dict(norm_g=row2(norm_g[0]), ln_g=row2(ln_v_g[0]), ln_b=row2(ln_v_b[0]),
                 mix=w_s[0], mix_b=b_s[0], conv_w=jnp.transpose(conv_w, (1, 0, 2)),
                 pe_g=row2(pe_norm_g[0]), fin_g=row2(final_norm_g))
    mm_f32 = dict(w_in=w_in[0], w_a=w_a_out[0], w_b=w_b_out[0], w_o=w_o[0],
                  w_pg=w_pe_gate[0], w_pp=w_pe_proj[0])
    mm_in = tuple(mm_f32[k] for k in MATMUL_NAMES)

    def specs(arrs):
        return [_resident(a.shape) for a in arrs]

    hbm = pl.BlockSpec(memory_space=pltpu.HBM)

    sample_in = (x_sample, p_sample[0], state_conv[0]) + tuple(small[k] for k in SMALL_NAMES)
    out_shapes = ((n_seq, n_t, d), (n_seq, CONV_W - 1, d), (n_seq, n_t, d), (CHUNK, d))
    outs = pl.pallas_call(
        _sample_kernel,
        grid=(1,),
        in_specs=specs(sample_in) + [hbm] * N_MM,
        out_specs=[_resident(s) for s in out_shapes] + [hbm] * N_MM,
        out_shape=[jax.ShapeDtypeStruct(s, f32) for s in out_shapes]
        + [jax.ShapeDtypeStruct(a.shape, bf) for a in mm_in],
        scratch_shapes=[pltpu.VMEM(a.shape, bf) for a in mm_in]
        + [pltpu.VMEM((STAGE_SLOTS, STAGE_ROWS, STAGE_COLS), f32),
           pltpu.SemaphoreType.DMA((STAGE_SLOTS,)),
           pltpu.SemaphoreType.DMA((len(WEIGHT_USE_ORDER),))],
        compiler_params=pltpu.CompilerParams(
            dimension_semantics=("arbitrary",),
            vmem_limit_bytes=V7X_VMEM_LIMIT_BYTES),
        name="sample_layer",
    )(*sample_in, *mm_in)
    y_s, conv_s, v_s, bias_tile = outs[:4]
    mm_bf16 = tuple(outs[4:])

    prompt_w = tuple(dict(small, mix_b=bias_tile)[k] for k in SMALL_NAMES) + mm_bf16
    y_p, conv_p, v_p = pl.pallas_call(
        _prompt_kernel,
        grid=(batch, seq // tm),
        in_specs=[pl.BlockSpec((1, tm, d), lambda b, i: (b, i, 0)),
                  pl.BlockSpec((1, tm, p_dim), lambda b, i: (b, i, 0))] + specs(prompt_w),
        out_specs=[pl.BlockSpec((1, tm, d), lambda b, i: (b, i, 0)),
                   pl.BlockSpec((1, 1, CONV_W - 1, d), lambda b, i: (0, b, 0, 0)),
                   pl.BlockSpec((1, 1, CHUNK, d), lambda b, i: (0, b, 0, 0))],
        out_shape=[jax.ShapeDtypeStruct((batch, seq, d), f32),
                   jax.ShapeDtypeStruct((1, batch, CONV_W - 1, d), f32),
                   jax.ShapeDtypeStruct((1, batch, CHUNK, d), f32)],
        scratch_shapes=[pltpu.VMEM((tm + CARRY_ROWS, d), f32)],
        compiler_params=pltpu.CompilerParams(
            dimension_semantics=("arbitrary", "arbitrary"),
            vmem_limit_bytes=V7X_VMEM_LIMIT_BYTES),
        name="prompt_layer",
    )(x_prompt, p_prompt[0], *prompt_w)

    return (y_p, y_s, conv_p, conv_s[None], v_p, v_s[None])
```

```python
import math

import jax
import jax.numpy as jnp
from jax import lax
from jax.experimental import pallas as pl
from jax.experimental.pallas import tpu as pltpu

CHUNK = 128
N_HEADS = 8
HEAD_W = 128
CONV_W = 3
EPS = 1e-6
LN_EPS = 1e-5
COL_U, COL_V, COL_GA, COL_C, COL_B, COL_H, COL_GB, COL_MA, COL_MB = range(9)

V7X_VMEM_LIMIT_BYTES = 60000 * 1024
PROMPT_TILE = 512
CARRY_ROWS = 8
STAGE_ROWS, STAGE_COLS = 256, 1024
STAGE_SLOTS = 8


def _gelu(x):
    c = math.sqrt(2.0 / math.pi)
    hx = 0.5 * x
    return hx + hx * jnp.tanh(x * (c + (c * 0.044715) * (x * x)))


def _sigmoid_of_twice(hx):
    return 0.5 * jnp.tanh(hx) + 0.5


def _silu_of_twice(hx):
    return hx + hx * jnp.tanh(hx)


def _rms_norm(x, g):
    return x * lax.rsqrt(jnp.mean(x * x, axis=-1, keepdims=True) + EPS) * g


def _layer_norm(x, g, b):
    mu = jnp.mean(x, axis=-1, keepdims=True)
    xc = x - mu
    var = jnp.mean(xc * xc, axis=-1, keepdims=True)
    return xc * lax.rsqrt(var + LN_EPS) * g + b


def _dot(a, b):
    return jnp.dot(a, b, preferred_element_type=jnp.float32)


SMALL_NAMES = ("norm_g", "ln_g", "ln_b", "mix", "mix_b", "conv_w", "pe_g", "fin_g")
MATMUL_NAMES = ("w_in", "w_a", "w_b", "w_o", "w_pg", "w_pp")
N_SMALL, N_MM = len(SMALL_NAMES), len(MATMUL_NAMES)
WEIGHT_USE_ORDER = (
    ("w_in", COL_V), ("w_in", COL_U), ("w_in", COL_GA), ("w_in", COL_C), ("w_in", COL_H),
    ("w_in", COL_B), ("w_in", COL_GB), ("w_in", COL_MA), "w_a", ("w_in", COL_MB), "w_b",
    "w_o", "w_pg", "w_pp")
HALVED_BLOCKS = (("w_in", COL_GA), ("w_in", COL_GB), ("w_in", COL_MA), ("w_in", COL_MB),
                 "w_pg")


def _layer_tile(x, p, w, spatial_mix, conv_taps, weights_ready=lambda key: None):
    d = x.shape[-1]
    bf = jnp.bfloat16
    h = _rms_norm(x, w["norm_g"][...]).astype(bf)

    def proj(col):
        weights_ready(("w_in", col))
        return _dot(h, w["w_in"][:, col * d:(col + 1) * d])

    def mm(a, name):
        weights_ready(name)
        return _dot(a.astype(bf), w[name][...])

    v = _layer_norm(_gelu(proj(COL_V)), w["ln_g"][...], w["ln_b"][...])
    y_a = _gelu(proj(COL_U)) * spatial_mix(v) * _silu_of_twice(proj(COL_GA))

    u = proj(COL_C) * proj(COL_H)
    u1, u2 = conv_taps(u)
    cw = w["conv_w"]
    conv = cw[0] * u2 + cw[1] * u1 + cw[2] * u
    y_b = proj(COL_B) * conv * _silu_of_twice(proj(COL_GB))

    merged = (_sigmoid_of_twice(proj(COL_MA)) * mm(y_a, "w_a")
              + _sigmoid_of_twice(proj(COL_MB)) * mm(y_b, "w_b"))
    x = x + mm(merged, "w_o")
    inv_rms = lax.rsqrt(jnp.mean(x * x, axis=-1, keepdims=True) + EPS)
    gate = _sigmoid_of_twice(inv_rms * mm(x * w["pe_g"][...], "w_pg"))
    x = x + gate * mm(p, "w_pp")
    return _rms_norm(x, w["fin_g"][...]), v, u


def _weight_stream(f32_hbm, bf16_vmem, bf16_hbm, stage_ref, in_sem, out_sem):
    index = dict(zip(MATMUL_NAMES, range(N_MM)))
    chunks, block_end, block_cols = [], [], []
    for key in WEIGHT_USE_ORDER:
        name, col = key if isinstance(key, tuple) else (key, None)
        rows, cols = f32_hbm[index[name]].shape
        rb = min(rows, STAGE_ROWS)
        n_blocks = len([k for k in WEIGHT_USE_ORDER if isinstance(k, tuple) and k[0] == name])
        width = cols if col is None else cols // n_blocks
        assert rows % rb == 0 and width % STAGE_COLS == 0
        c_lo = 0 if col is None else col * width
        for c0 in range(c_lo, c_lo + width, STAGE_COLS):
            for r0 in range(0, rows, rb):
                chunks.append((index[name], r0, rb, c0))
        block_end.append(len(chunks))
        block_cols.append((index[name], c_lo, width))
    n_slots = stage_ref.shape[0]
    visited = []

    def fetch(k):
        wi, r0, rb, c0 = chunks[k]
        return pltpu.make_async_copy(
            f32_hbm[wi].at[pl.ds(r0, rb), pl.ds(c0, STAGE_COLS)],
            stage_ref.at[k % n_slots, pl.ds(0, rb), :], in_sem.at[k % n_slots])

    def export(i):
        wi, c_lo, width = block_cols[i]
        return pltpu.make_async_copy(bf16_vmem[wi].at[:, pl.ds(c_lo, width)],
                                     bf16_hbm[wi].at[:, pl.ds(c_lo, width)], out_sem.at[i])

    for k in range(min(n_slots, len(chunks))):
        fetch(k).start()

    def weights_ready(key):
        i = len(visited)
        assert key == WEIGHT_USE_ORDER[i], (key, WEIGHT_USE_ORDER[i])
        visited.append(key)
        for k in range(block_end[i - 1] if i else 0, block_end[i]):
            wi, r0, rb, c0 = chunks[k]
            fetch(k).wait()
            chunk = stage_ref[k % n_slots, 0:rb, :]
            if key in HALVED_BLOCKS:
                chunk = 0.5 * chunk
            bf16_vmem[wi][pl.ds(r0, rb), pl.ds(c0, STAGE_COLS)] = chunk.astype(jnp.bfloat16)
            if k + n_slots < len(chunks):
                fetch(k + n_slots).start()
        export(i).start()

    def drain():
        assert len(visited) == len(WEIGHT_USE_ORDER)
        for i in range(len(WEIGHT_USE_ORDER)):
            export(i).wait()

    return weights_ready, drain


def _prompt_kernel(x_ref, p_ref, *rest):
    w = dict(zip(SMALL_NAMES + MATMUL_NAMES, rest[:N_SMALL + N_MM]))
    y_ref, conv_ref, vrow_ref, pbuf_ref = rest[N_SMALL + N_MM:]
    tm, d = x_ref.shape[1], x_ref.shape[2]
    bf = jnp.bfloat16

    @pl.when(pl.program_id(1) == 0)
    def _():
        pbuf_ref[0:CARRY_ROWS, :] = jnp.zeros((CARRY_ROWS, d), jnp.float32)

    def spatial_mix(v):
        v_b = v.astype(bf)
        row = lax.broadcasted_iota(jnp.int32, (CHUNK, CHUNK), 0)
        col = lax.broadcasted_iota(jnp.int32, (CHUNK, CHUNK), 1)
        mix = [jnp.where(col <= row, w["mix"][hd], 0.0).astype(bf)
               for hd in range(N_HEADS)]
        s_rows = []
        for c in range(tm // CHUNK):
            s_heads = [_dot(mix[hd], v_b[c * CHUNK:(c + 1) * CHUNK,
                                         hd * HEAD_W:(hd + 1) * HEAD_W])
                       for hd in range(N_HEADS)]
            s_rows.append(jnp.concatenate(s_heads, axis=1) + w["mix_b"][...])
        return jnp.concatenate(s_rows, axis=0)

    def conv_taps(u):
        carry = pbuf_ref[...]
        pbuf_ref[...] = u[tm - CARRY_ROWS:, :]
        row = lax.broadcasted_iota(jnp.int32, (CARRY_ROWS, d), 0)

        def shifted(k):
            r = pltpu.roll(u, k, axis=0)
            head = jnp.where(row < k, pltpu.roll(carry, k, axis=0), r[:CARRY_ROWS, :])
            return jnp.concatenate([head, r[CARRY_ROWS:, :]], axis=0)

        return shifted(1), shifted(2)

    y, v, u = _layer_tile(x_ref[0], p_ref[0], w, spatial_mix, conv_taps)
    y_ref[0] = y
    vrow_ref[0, 0] = v[tm - CHUNK:, :]
    conv_ref[0, 0] = u[tm - (CONV_W - 1):, :]


def _sample_kernel(x_ref, p_ref, st_ref, *rest):
    small = rest[:N_SMALL]
    mm_f32_hbm = rest[N_SMALL:N_SMALL + N_MM]
    outs = rest[N_SMALL + N_MM:]
    y_ref, conv_ref, vrow_ref, bias_ref = outs[:4]
    mm_bf16_hbm = outs[4:4 + N_MM]
    mm_bf16 = outs[4 + N_MM:4 + 2 * N_MM]
    stage_ref, in_sem, out_sem = outs[4 + 2 * N_MM:]
    w = dict(zip(SMALL_NAMES + MATMUL_NAMES, small + mm_bf16))
    n_seq, n_t, _ = x_ref.shape

    b_t = jnp.transpose(w["mix_b"][...])
    bias_tile = jnp.concatenate(
        [jnp.broadcast_to(b_t[:, hd:hd + 1], (CHUNK, HEAD_W)) for hd in range(N_HEADS)],
        axis=1)
    bias_ref[...] = bias_tile

    def mix_row(t, j):
        return jnp.concatenate(
            [jnp.broadcast_to(w["mix"][hd, t:t + 1, j:j + 1], (1, HEAD_W))
             for hd in range(N_HEADS)], axis=1)

    def slab(a, t):
        return a[t * n_seq:(t + 1) * n_seq, :]

    def gather(ref):
        return jnp.concatenate([ref[:, t, :] for t in range(ref.shape[1])], axis=0)

    def scatter(ref, a):
        for t in range(ref.shape[1]):
            ref[:, t, :] = slab(a, t)

    def spatial_mix(v):
        out = []
        for t in range(n_t):
            acc = bias_tile[t:t + 1, :] + mix_row(t, 0) * slab(v, 0)
            for j in range(1, t + 1):
                acc = acc + mix_row(t, j) * slab(v, j)
            out.append(acc)
        return jnp.concatenate(out, axis=0)

    def conv_taps(u):
        full = [st_ref[:, j, :] for j in range(CONV_W - 1)] + [
            slab(u, t) for t in range(n_t)]
        return (jnp.concatenate(full[1:1 + n_t], axis=0),
                jnp.concatenate(full[0:n_t], axis=0))

    weights_ready, drain = _weight_stream(mm_f32_hbm, mm_bf16, mm_bf16_hbm, stage_ref,
                                          in_sem, out_sem)
    y, v, u = _layer_tile(gather(x_ref), gather(p_ref), w, spatial_mix, conv_taps,
                          weights_ready)
    scatter(y_ref, y)
    scatter(vrow_ref, v)
    scatter(conv_ref, u[(n_t - (CONV_W - 1)) * n_seq:, :])
    drain()


def _resident(shape):
    return pl.BlockSpec(shape, lambda *_: (0,) * len(shape),
                        pipeline_mode=pl.Buffered(1))


def kernel(x_prompt, x_sample, state_conv, p_prompt, p_sample, norm_g, w_in, ln_v_g,
           ln_v_b, w_s, b_s, conv_w, w_a_out, w_b_out, w_o, pe_norm_g, w_pe_gate,
           w_pe_proj, final_norm_g):
    assert w_in.shape[0] == 1, "single-layer step only"
    batch, seq, d = x_prompt.shape
    n_seq, n_t, _ = x_sample.shape
    p_dim = p_prompt.shape[-1]
    tm = PROMPT_TILE
    assert seq % tm == 0 and tm % CHUNK == 0 and d == N_HEADS * HEAD_W
    assert CONV_W - 1 <= n_t <= CHUNK
    f32, bf = jnp.float32, jnp.bfloat16

    row2 = lambda a: a.reshape(1, -1)
    small = dict(norm_g=row2(norm_g[0]), ln_g=row2(ln_v_g[0]), ln_b=row2(ln_v_b[0]),
                 mix=w_s[0], mix_b=b_s[0], conv_w=jnp.transpose(conv_w, (1, 0, 2)),
                 pe_g=row2(pe_norm_g[0]), fin_g=row2(final_norm_g))
    mm_f32 = dict(w_in=w_in[0], w_a=w_a_out[0], w_b=w_b_out[0], w_o=w_o[0],
                  w_pg=w_pe_gate[0], w_pp=w_pe_proj[0])
    mm_in = tuple(mm_f32[k] for k in MATMUL_NAMES)

    def specs(arrs):
        return [_resident(a.shape) for a in arrs]

    hbm = pl.BlockSpec(memory_space=pltpu.HBM)

    sample_in = (x_sample, p_sample[0], state_conv[0]) + tuple(small[k] for k in SMALL_NAMES)
    out_shapes = ((n_seq, n_t, d), (n_seq, CONV_W - 1, d), (n_seq, n_t, d), (CHUNK, d))
    outs = pl.pallas_call(
        _sample_kernel,
        grid=(1,),
        in_specs=specs(sample_in) + [hbm] * N_MM,
        out_specs=[_resident(s) for s in out_shapes] + [hbm] * N_MM,
        out_shape=[jax.ShapeDtypeStruct(s, f32) for s in out_shapes]
        + [jax.ShapeDtypeStruct(a.shape, bf) for a in mm_in],
        scratch_shapes=[pltpu.VMEM(a.shape, bf) for a in mm_in]
        + [pltpu.VMEM((STAGE_SLOTS, STAGE_ROWS, STAGE_COLS), f32),
           pltpu.SemaphoreType.DMA((STAGE_SLOTS,)),
           pltpu.SemaphoreType.DMA((len(WEIGHT_USE_ORDER),))],
        compiler_params=pltpu.CompilerParams(
            dimension_semantics=("arbitrary",),
            vmem_limit_bytes=V7X_VMEM_LIMIT_BYTES),
        name="sample_layer",
    )(*sample_in, *mm_in)
    y_s, conv_s, v_s, bias_tile = outs[:4]
    mm_bf16 = tuple(outs[4:])

    prompt_w = tuple(dict(small, mix_b=bias_tile)[k] for k in SMALL_NAMES) + mm_bf16
    y_p, conv_p, v_p = pl.pallas_call(
        _prompt_kernel,
        grid=(batch, seq // tm),
        in_specs=[pl.BlockSpec((1, tm, d), lambda b, i: (b, i, 0)),
                  pl.BlockSpec((1, tm, p_dim), lambda b, i: (b, i, 0))] + specs(prompt_w),
        out_specs=[pl.BlockSpec((1, tm, d), lambda b, i: (b, i, 0)),
                   pl.BlockSpec((1, 1, CONV_W - 1, d), lambda b, i: (0, b, 0, 0)),
                   pl.BlockSpec((1, 1, CHUNK, d), lambda b, i: (0, b, 0, 0))],
        out_shape=[jax.ShapeDtypeStruct((batch, seq, d), f32),
                   jax.ShapeDtypeStruct((1, batch, CONV_W - 1, d), f32),
                   jax.ShapeDtypeStruct((1, batch, CHUNK, d), f32)],
        scratch_shapes=[pltpu.VMEM((CARRY_ROWS, d), f32)],
        compiler_params=pltpu.CompilerParams(
            dimension_semantics=("arbitrary", "arbitrary"),
            vmem_limit_bytes=V7X_VMEM_LIMIT_BYTES),
        name="prompt_layer",
    )(x_prompt, p_prompt[0], *prompt_w)

    return (y_p, y_s, conv_p, conv_s[None], v_p, v_s[None])
```

```python
import math

import jax
import jax.numpy as jnp
from jax import lax
from jax.experimental import pallas as pl
from jax.experimental.pallas import tpu as pltpu

CHUNK = 128
N_HEADS = 8
HEAD_W = 128
CONV_W = 3
EPS = 1e-6
LN_EPS = 1e-5
COL_U, COL_V, COL_GA, COL_C, COL_B, COL_H, COL_GB, COL_MA, COL_MB = range(9)

V7X_VMEM_LIMIT_BYTES = 60000 * 1024
PROMPT_TILE = 512
CARRY_ROWS = 8
STAGE_ROWS, STAGE_COLS = 256, 1024
STAGE_SLOTS = 8


def _gelu_of_twice(hx):
    c = math.sqrt(2.0 / math.pi)
    return hx + hx * jnp.tanh(hx * (2.0 * c + (8.0 * c * 0.044715) * (hx * hx)))


def _twice_sigmoid_of_twice(hx):
    return jnp.tanh(hx) + 1.0


def _silu_of_twice(hx):
    return hx + hx * jnp.tanh(hx)


def _rms_norm(x, g):
    return x * lax.rsqrt(jnp.mean(x * x, axis=-1, keepdims=True) + EPS) * g


def _layer_norm(x, g, b):
    mu = jnp.mean(x, axis=-1, keepdims=True)
    xc = x - mu
    var = jnp.mean(xc * xc, axis=-1, keepdims=True)
    return xc * lax.rsqrt(var + LN_EPS) * g + b


def _dot(a, b):
    return jnp.dot(a, b, preferred_element_type=jnp.float32)


SMALL_NAMES = ("norm_g", "ln_g", "ln_b", "mix", "mix_b", "conv_w", "pe_g", "fin_g")
MATMUL_NAMES = ("w_in", "w_a", "w_b", "w_o", "w_pg", "w_pp")
N_SMALL, N_MM = len(SMALL_NAMES), len(MATMUL_NAMES)
WEIGHT_USE_ORDER = (
    ("w_in", COL_V), ("w_in", COL_U), ("w_in", COL_GA), ("w_in", COL_C), ("w_in", COL_H),
    ("w_in", COL_B), ("w_in", COL_GB), ("w_in", COL_MA), "w_a", ("w_in", COL_MB), "w_b",
    "w_o", "w_pg", "w_pp")
HALVED_BLOCKS = (("w_in", COL_U), ("w_in", COL_V), ("w_in", COL_GA), ("w_in", COL_GB),
                 ("w_in", COL_MA), ("w_in", COL_MB), "w_pg", "w_a", "w_b", "w_pp")


def _layer_tile(x, p, w, spatial_mix, conv_taps, weights_ready=lambda key: None):
    d = x.shape[-1]
    bf = jnp.bfloat16
    h = _rms_norm(x, w["norm_g"][...]).astype(bf)

    def proj(col):
        weights_ready(("w_in", col))
        return _dot(h, w["w_in"][:, col * d:(col + 1) * d])

    def mm(a, name):
        weights_ready(name)
        return _dot(a.astype(bf), w[name][...])

    v = _layer_norm(_gelu_of_twice(proj(COL_V)), w["ln_g"][...], w["ln_b"][...])
    y_a = _gelu_of_twice(proj(COL_U)) * spatial_mix(v) * _silu_of_twice(proj(COL_GA))

    u = proj(COL_C) * proj(COL_H)
    u1, u2 = conv_taps(u)
    cw = w["conv_w"]
    conv = cw[0] * u2 + cw[1] * u1 + cw[2] * u
    y_b = proj(COL_B) * conv * _silu_of_twice(proj(COL_GB))

    merged = (_twice_sigmoid_of_twice(proj(COL_MA)) * mm(y_a, "w_a")
              + _twice_sigmoid_of_twice(proj(COL_MB)) * mm(y_b, "w_b"))
    x = x + mm(merged, "w_o")
    inv_rms = lax.rsqrt(jnp.mean(x * x, axis=-1, keepdims=True) + EPS)
    gate = _twice_sigmoid_of_twice(inv_rms * mm(x * w["pe_g"][...], "w_pg"))
    x = x + gate * mm(p, "w_pp")
    return _rms_norm(x, w["fin_g"][...]), v, u


def _weight_stream(f32_hbm, bf16_vmem, bf16_hbm, stage_ref, in_sem, out_sem):
    index = dict(zip(MATMUL_NAMES, range(N_MM)))
    chunks, block_end, block_cols = [], [], []
    for key in WEIGHT_USE_ORDER:
        name, col = key if isinstance(key, tuple) else (key, None)
        rows, cols = f32_hbm[index[name]].shape
        rb = min(rows, STAGE_ROWS)
        n_blocks = len([k for k in WEIGHT_USE_ORDER if isinstance(k, tuple) and k[0] == name])
        width = cols if col is None else cols // n_blocks
        assert rows % rb == 0 and width % STAGE_COLS == 0
        c_lo = 0 if col is None else col * width
        for c0 in range(c_lo, c_lo + width, STAGE_COLS):
            for r0 in range(0, rows, rb):
                chunks.append((index[name], r0, rb, c0))
        block_end.append(len(chunks))
        block_cols.append((index[name], c_lo, width))
    n_slots = stage_ref.shape[0]
    visited = []

    def fetch(k):
        wi, r0, rb, c0 = chunks[k]
        return pltpu.make_async_copy(
            f32_hbm[wi].at[pl.ds(r0, rb), pl.ds(c0, STAGE_COLS)],
            stage_ref.at[k % n_slots, pl.ds(0, rb), :], in_sem.at[k % n_slots])

    def export(i):
        wi, c_lo, width = block_cols[i]
        return pltpu.make_async_copy(bf16_vmem[wi].at[:, pl.ds(c_lo, width)],
                                     bf16_hbm[wi].at[:, pl.ds(c_lo, width)], out_sem.at[i])

    for k in range(min(n_slots, len(chunks))):
        fetch(k).start()

    def weights_ready(key):
        i = len(visited)
        assert key == WEIGHT_USE_ORDER[i], (key, WEIGHT_USE_ORDER[i])
        visited.append(key)
        for k in range(block_end[i - 1] if i else 0, block_end[i]):
            wi, r0, rb, c0 = chunks[k]
            fetch(k).wait()
            chunk = stage_ref[k % n_slots, 0:rb, :]
            if key in HALVED_BLOCKS:
                chunk = 0.5 * chunk
            bf16_vmem[wi][pl.ds(r0, rb), pl.ds(c0, STAGE_COLS)] = chunk.astype(jnp.bfloat16)
            if k + n_slots < len(chunks):
                fetch(k + n_slots).start()
        export(i).start()

    def drain():
        assert len(visited) == len(WEIGHT_USE_ORDER)
        for i in range(len(WEIGHT_USE_ORDER)):
            export(i).wait()

    return weights_ready, drain


def _prompt_kernel(x_ref, p_ref, *rest):
    w = dict(zip(SMALL_NAMES + MATMUL_NAMES, rest[:N_SMALL + N_MM]))
    y_ref, conv_ref, vrow_ref, pbuf_ref = rest[N_SMALL + N_MM:]
    tm, d = x_ref.shape[1], x_ref.shape[2]
    bf = jnp.bfloat16

    @pl.when(pl.program_id(1) == 0)
    def _():
        pbuf_ref[0:CARRY_ROWS, :] = jnp.zeros((CARRY_ROWS, d), jnp.float32)

    def spatial_mix(v):
        v_b = v.astype(bf)
        row = lax.broadcasted_iota(jnp.int32, (CHUNK, CHUNK), 0)
        col = lax.broadcasted_iota(jnp.int32, (CHUNK, CHUNK), 1)
        mix = [jnp.where(col <= row, w["mix"][hd], 0.0).astype(bf)
               for hd in range(N_HEADS)]
        s_rows = []
        for c in range(tm // CHUNK):
            s_heads = [_dot(mix[hd], v_b[c * CHUNK:(c + 1) * CHUNK,
                                         hd * HEAD_W:(hd + 1) * HEAD_W])
                       for hd in range(N_HEADS)]
            s_rows.append(jnp.concatenate(s_heads, axis=1) + w["mix_b"][...])
        return jnp.concatenate(s_rows, axis=0)

    def conv_taps(u):
        pbuf_ref[CARRY_ROWS:CARRY_ROWS + tm, :] = u
        return (pbuf_ref[CARRY_ROWS - 1:CARRY_ROWS - 1 + tm, :],
                pbuf_ref[CARRY_ROWS - 2:CARRY_ROWS - 2 + tm, :])

    y, v, _ = _layer_tile(x_ref[0], p_ref[0], w, spatial_mix, conv_taps)
    y_ref[0] = y
    vrow_ref[0, 0] = v[tm - CHUNK:, :]
    tail = pbuf_ref[tm:tm + CARRY_ROWS, :]
    pbuf_ref[0:CARRY_ROWS, :] = tail
    conv_ref[0, 0] = tail[CARRY_ROWS - (CONV_W - 1):, :]


def _sample_kernel(x_ref, p_ref, st_ref, *rest):
    small = rest[:N_SMALL]
    mm_f32_hbm = rest[N_SMALL:N_SMALL + N_MM]
    outs = rest[N_SMALL + N_MM:]
    y_ref, conv_ref, vrow_ref, bias_ref = outs[:4]
    mm_bf16_hbm = outs[4:4 + N_MM]
    mm_bf16 = outs[4 + N_MM:4 + 2 * N_MM]
    stage_ref, in_sem, out_sem = outs[4 + 2 * N_MM:]
    w = dict(zip(SMALL_NAMES + MATMUL_NAMES, small + mm_bf16))
    n_seq, n_t, _ = x_ref.shape

    b_t = jnp.transpose(w["mix_b"][...])
    bias_tile = jnp.concatenate(
        [jnp.broadcast_to(b_t[:, hd:hd + 1], (CHUNK, HEAD_W)) for hd in range(N_HEADS)],
        axis=1)
    bias_ref[...] = bias_tile

    def mix_row(t, j):
        return jnp.concatenate(
            [jnp.broadcast_to(w["mix"][hd, t:t + 1, j:j + 1], (1, HEAD_W))
             for hd in range(N_HEADS)], axis=1)

    def slab(a, t):
        return a[t * n_seq:(t + 1) * n_seq, :]

    def gather(ref):
        return jnp.concatenate([ref[:, t, :] for t in range(ref.shape[1])], axis=0)

    def scatter(ref, a):
        for t in range(ref.shape[1]):
            ref[:, t, :] = slab(a, t)

    def spatial_mix(v):
        out = []
        for t in range(n_t):
            acc = bias_tile[t:t + 1, :] + mix_row(t, 0) * slab(v, 0)
            for j in range(1, t + 1):
                acc = acc + mix_row(t, j) * slab(v, j)
            out.append(acc)
        return jnp.concatenate(out, axis=0)

    def conv_taps(u):
        full = [st_ref[:, j, :] for j in range(CONV_W - 1)] + [
            slab(u, t) for t in range(n_t)]
        return (jnp.concatenate(full[1:1 + n_t], axis=0),
                jnp.concatenate(full[0:n_t], axis=0))

    weights_ready, drain = _weight_stream(mm_f32_hbm, mm_bf16, mm_bf16_hbm, stage_ref,
                                          in_sem, out_sem)
    y, v, u = _layer_tile(gather(x_ref), gather(p_ref), w, spatial_mix, conv_taps,
                          weights_ready)
    scatter(y_ref, y)
    scatter(vrow_ref, v)
    scatter(conv_ref, u[(n_t - (CONV_W - 1)) * n_seq:, :])
    drain()


def _resident(shape):
    return pl.BlockSpec(shape, lambda *_: (0,) * len(shape),
                        pipeline_mode=pl.Buffered(1))


def kernel(x_prompt, x_sample, state_conv, p_prompt, p_sample, norm_g, w_in, ln_v_g,
           ln_v_b, w_s, b_s, conv_w, w_a_out, w_b_out, w_o, pe_norm_g, w_pe_gate,
           w_pe_proj, final_norm_g):
    assert w_in.shape[0] == 1, "single-layer step only"
    batch, seq, d = x_prompt.shape
    n_seq, n_t, _ = x_sample.shape
    p_dim = p_prompt.shape[-1]
    tm = PROMPT_TILE
    assert seq % tm == 0 and tm % CHUNK == 0 and d == N_HEADS * HEAD_W
    assert CONV_W - 1 <= n_t <= CHUNK
    f32, bf = jnp.float32, jnp.bfloat16

    row2 = lambda a: a.reshape(1, -1)
    small = dict(norm_g=row2(norm_g[0]), ln_g=row2(ln_v_g[0]), ln_b=row2(ln_v_b[0]),
                 mix=w_s[0], mix_b=b_s[0], conv_w=jnp.transpose(conv_w, (1, 0, 2)),
                 pe_g=row2(pe_norm_g[0]), fin_g=row2(final_norm_g))
    mm_f32 = dict(w_in=w_in[0], w_a=w_a_out[0], w_b=w_b_out[0], w_o=w_o[0],
                  w_pg=w_pe_gate[0], w_pp=w_pe_proj[0])
    mm_in = tuple(mm_f32[k] for k in MATMUL_NAMES)

    def specs(arrs):
        return [_resident(a.shape) for a in arrs]

    hbm = pl.BlockSpec(memory_space=pltpu.HBM)

    sample_in = (x_sample, p_sample[0], state_conv[0]) + tuple(small[k] for k in SMALL_NAMES)
    out_shapes = ((n_seq, n_t, d), (n_seq, CONV_W - 1, d), (n_seq, n_t, d), (CHUNK, d))
    outs = pl.pallas_call(
        _sample_kernel,
        grid=(1,),
        in_specs=specs(sample_in) + [hbm] * N_MM,
        out_specs=[_resident(s) for s in out_shapes] + [hbm] * N_MM,
        out_shape=[jax.ShapeDtypeStruct(s, f32) for s in out_shapes]
        + [jax.ShapeDtypeStruct(a.shape, bf) for a in mm_in],
        scratch_shapes=[pltpu.VMEM(a.shape, bf) for a in mm_in]
        + [pltpu.VMEM((STAGE_SLOTS, STAGE_ROWS, STAGE_COLS), f32),
           pltpu.SemaphoreType.DMA((STAGE_SLOTS,)),
           pltpu.SemaphoreType.DMA((len(WEIGHT_USE_ORDER),))],
        compiler_params=pltpu.CompilerParams(
            dimension_semantics=("arbitrary",),
            vmem_limit_bytes=V7X_VMEM_LIMIT_BYTES),
        name="sample_layer",
    )(*sample_in, *mm_in)
    y_s, conv_s, v_s, bias_tile = outs[:4]
    mm_bf16 = tuple(outs[4:])

    prompt_w = tuple(dict(small, mix_b=bias_tile)[k] for k in SMALL_NAMES) + mm_bf16
    y_p, conv_p, v_p = pl.pallas_call(
        _prompt_kernel,
        grid=(batch, seq // tm),
        in_specs=[pl.BlockSpec((1, tm, d), lambda b, i: (b, i, 0)),
                  pl.BlockSpec((1, tm, p_dim), lambda b, i: (b, i, 0))] + specs(prompt_w),
        out_specs=[pl.BlockSpec((1, tm, d), lambda b, i: (b, i, 0)),
                   pl.BlockSpec((1, 1, CONV_W - 1, d), lambda b, i: (0, b, 0, 0)),
                   pl.BlockSpec((1, 1, CHUNK, d), lambda b, i: (0, b, 0, 0))],
        out_shape=[jax.ShapeDtypeStruct((batch, seq, d), f32),
                   jax.ShapeDtypeStruct((1, batch, CONV_W - 1, d), f32),
                   jax.ShapeDtypeStruct((1, batch, CHUNK, d), f32)],
        scratch_shapes=[pltpu.VMEM((tm + CARRY_ROWS, d), f32)],
        compiler_params=pltpu.CompilerParams(
            dimension_semantics=("arbitrary", "arbitrary"),
            vmem_limit_bytes=V7X_VMEM_LIMIT_BYTES),
        name="prompt_layer",
    )(x_prompt, p_prompt[0], *prompt_w)

    return (y_p, y_s, conv_p, conv_s[None], v_p, v_s[None])
```

```python
import math

import jax
import jax.numpy as jnp
from jax import lax
from jax.experimental import pallas as pl
from jax.experimental.pallas import tpu as pltpu

CHUNK = 128
N_HEADS = 8
HEAD_W = 128
CONV_W = 3
EPS = 1e-6
LN_EPS = 1e-5
COL_U, COL_V, COL_GA, COL_C, COL_B, COL_H, COL_GB, COL_MA, COL_MB = range(9)

V7X_VMEM_LIMIT_BYTES = 60000 * 1024
PROMPT_TILE = 512
CARRY_ROWS = 8
STAGE_ROWS, STAGE_COLS = 256, 1024
STAGE_SLOTS = 8


def _gelu_of_twice(hx):
    c = math.sqrt(2.0 / math.pi)
    return hx + hx * jnp.tanh(hx * (2.0 * c + (8.0 * c * 0.044715) * (hx * hx)))


def _twice_sigmoid_of_twice(hx):
    return jnp.tanh(hx) + 1.0


def _silu_of_twice(hx):
    return hx + hx * jnp.tanh(hx)


def _rms_norm(x, g):
    return x * lax.rsqrt(jnp.mean(x * x, axis=-1, keepdims=True) + EPS) * g


def _layer_norm(x, g, b):
    mu = jnp.mean(x, axis=-1, keepdims=True)
    xc = x - mu
    var = jnp.mean(xc * xc, axis=-1, keepdims=True)
    return xc * lax.rsqrt(var + LN_EPS) * g + b


def _dot(a, b):
    return jnp.dot(a, b, preferred_element_type=jnp.float32)


SMALL_NAMES = ("norm_g", "ln_g", "ln_b", "mix", "mix_b", "conv_w", "pe_g", "fin_g")
MATMUL_NAMES = ("w_in", "w_a", "w_b", "w_o", "w_pg", "w_pp")
N_SMALL, N_MM = len(SMALL_NAMES), len(MATMUL_NAMES)
WEIGHT_USE_ORDER = (
    ("w_in", COL_V), ("w_in", COL_U), ("w_in", COL_GA), ("w_in", COL_C), ("w_in", COL_H),
    ("w_in", COL_B), ("w_in", COL_GB), ("w_in", COL_MA), "w_a", ("w_in", COL_MB), "w_b",
    "w_o", "w_pg", "w_pp")
HALVED_BLOCKS = (("w_in", COL_U), ("w_in", COL_V), ("w_in", COL_GA), ("w_in", COL_GB),
                 ("w_in", COL_MA), ("w_in", COL_MB), "w_pg", "w_a", "w_b", "w_pp")


def _layer_tile(x, p, w, spatial_mix, conv_taps, weights_ready=lambda key: None):
    d = x.shape[-1]
    bf = jnp.bfloat16
    h = _rms_norm(x, w["norm_g"][...]).astype(bf)

    def proj(col):
        weights_ready(("w_in", col))
        return _dot(h, w["w_in"][:, col * d:(col + 1) * d])

    def mm(a, name):
        weights_ready(name)
        return _dot(a.astype(bf), w[name][...])

    v = _layer_norm(_gelu_of_twice(proj(COL_V)), w["ln_g"][...], w["ln_b"][...])
    y_a = _gelu_of_twice(proj(COL_U)) * spatial_mix(v) * _silu_of_twice(proj(COL_GA))

    u = proj(COL_C) * proj(COL_H)
    u1, u2 = conv_taps(u)
    cw = w["conv_w"]
    conv = cw[0] * u2 + cw[1] * u1 + cw[2] * u
    y_b = proj(COL_B) * conv * _silu_of_twice(proj(COL_GB))

    merged = (_twice_sigmoid_of_twice(proj(COL_MA)) * mm(y_a, "w_a")
              + _twice_sigmoid_of_twice(proj(COL_MB)) * mm(y_b, "w_b"))
    x = x + mm(merged, "w_o")
    inv_rms = lax.rsqrt(jnp.mean(x * x, axis=-1, keepdims=True) + EPS)
    gate = _twice_sigmoid_of_twice(inv_rms * mm(x * w["pe_g"][...], "w_pg"))
    x = x + gate * mm(p, "w_pp")
    return _rms_norm(x, w["fin_g"][...]), v, u


def _weight_stream(f32_hbm, bf16_vmem, bf16_hbm, stage_ref, in_sem, out_sem):
    index = dict(zip(MATMUL_NAMES, range(N_MM)))
    chunks, block_end, block_cols = [], [], []
    for key in WEIGHT_USE_ORDER:
        name, col = key if isinstance(key, tuple) else (key, None)
        rows, cols = f32_hbm[index[name]].shape
        rb = min(rows, STAGE_ROWS)
        n_blocks = len([k for k in WEIGHT_USE_ORDER if isinstance(k, tuple) and k[0] == name])
        width = cols if col is None else cols // n_blocks
        assert rows % rb == 0 and width % STAGE_COLS == 0
        c_lo = 0 if col is None else col * width
        for c0 in range(c_lo, c_lo + width, STAGE_COLS):
            for r0 in range(0, rows, rb):
                chunks.append((index[name], r0, rb, c0))
        block_end.append(len(chunks))
        block_cols.append((index[name], c_lo, width))
    n_slots = stage_ref.shape[0]
    visited = []

    def fetch(k):
        wi, r0, rb, c0 = chunks[k]
        return pltpu.make_async_copy(
            f32_hbm[wi].at[pl.ds(r0, rb), pl.ds(c0, STAGE_COLS)],
            stage_ref.at[k % n_slots, pl.ds(0, rb), :], in_sem.at[k % n_slots])

    def export(i):
        wi, c_lo, width = block_cols[i]
        return pltpu.make_async_copy(bf16_vmem[wi].at[:, pl.ds(c_lo, width)],
                                     bf16_hbm[wi].at[:, pl.ds(c_lo, width)], out_sem.at[i])

    for k in range(min(n_slots, len(chunks))):
        fetch(k).start()

    def weights_ready(key):
        i = len(visited)
        assert key == WEIGHT_USE_ORDER[i], (key, WEIGHT_USE_ORDER[i])
        visited.append(key)
        for k in range(block_end[i - 1] if i else 0, block_end[i]):
            wi, r0, rb, c0 = chunks[k]
            fetch(k).wait()
            chunk = stage_ref[k % n_slots, 0:rb, :]
            if key in HALVED_BLOCKS:
                chunk = 0.5 * chunk
            bf16_vmem[wi][pl.ds(r0, rb), pl.ds(c0, STAGE_COLS)] = chunk.astype(jnp.bfloat16)
            if k + n_slots < len(chunks):
                fetch(k + n_slots).start()
        export(i).start()

    def drain():
        assert len(visited) == len(WEIGHT_USE_ORDER)
        for i in range(len(WEIGHT_USE_ORDER)):
            export(i).wait()

    return weights_ready, drain


def _prompt_kernel(x_ref, p_ref, *rest):
    w = dict(zip(SMALL_NAMES + MATMUL_NAMES, rest[:N_SMALL + N_MM]))
    y_ref, conv_ref, vrow_ref, pbuf_ref = rest[N_SMALL + N_MM:]
    tm, d = x_ref.shape[1], x_ref.shape[2]
    bf = jnp.bfloat16

    @pl.when(pl.program_id(1) == 0)
    def _():
        pbuf_ref[0:CARRY_ROWS, :] = jnp.zeros((CARRY_ROWS, d), jnp.float32)

    def spatial_mix(v):
        v_b = v.astype(bf)
        s_rows = []
        for c in range(tm // CHUNK):
            s_heads = [_dot(w["mix"][hd], v_b[c * CHUNK:(c + 1) * CHUNK,
                                              hd * HEAD_W:(hd + 1) * HEAD_W])
                       for hd in range(N_HEADS)]
            s_rows.append(jnp.concatenate(s_heads, axis=1) + w["mix_b"][...])
        return jnp.concatenate(s_rows, axis=0)

    def conv_taps(u):
        pbuf_ref[CARRY_ROWS:CARRY_ROWS + tm, :] = u
        return (pbuf_ref[CARRY_ROWS - 1:CARRY_ROWS - 1 + tm, :],
                pbuf_ref[CARRY_ROWS - 2:CARRY_ROWS - 2 + tm, :])

    y, v, _ = _layer_tile(x_ref[0], p_ref[0], w, spatial_mix, conv_taps)
    y_ref[0] = y
    vrow_ref[0, 0] = v[tm - CHUNK:, :]
    tail = pbuf_ref[tm:tm + CARRY_ROWS, :]
    pbuf_ref[0:CARRY_ROWS, :] = tail
    conv_ref[0, 0] = tail[CARRY_ROWS - (CONV_W - 1):, :]


def _sample_kernel(x_ref, p_ref, st_ref, *rest):
    small = rest[:N_SMALL]
    mm_f32_hbm = rest[N_SMALL:N_SMALL + N_MM]
    outs = rest[N_SMALL + N_MM:]
    y_ref, conv_ref, vrow_ref, bias_ref, tril_ref = outs[:5]
    mm_bf16_hbm = outs[5:5 + N_MM]
    mm_bf16 = outs[5 + N_MM:5 + 2 * N_MM]
    stage_ref, in_sem, out_sem = outs[5 + 2 * N_MM:]
    w = dict(zip(SMALL_NAMES + MATMUL_NAMES, small + mm_bf16))
    n_seq, n_t, _ = x_ref.shape

    row = lax.broadcasted_iota(jnp.int32, (CHUNK, CHUNK), 0)
    col = lax.broadcasted_iota(jnp.int32, (CHUNK, CHUNK), 1)
    for hd in range(N_HEADS):
        tril_ref[hd] = jnp.where(col <= row, w["mix"][hd], 0.0).astype(jnp.bfloat16)
    b_t = jnp.transpose(w["mix_b"][...])
    bias_tile = jnp.concatenate(
        [jnp.broadcast_to(b_t[:, hd:hd + 1], (CHUNK, HEAD_W)) for hd in range(N_HEADS)],
        axis=1)
    bias_ref[...] = bias_tile

    def mix_row(t, j):
        return jnp.concatenate(
            [jnp.broadcast_to(w["mix"][hd, t:t + 1, j:j + 1], (1, HEAD_W))
             for hd in range(N_HEADS)], axis=1)

    def slab(a, t):
        return a[t * n_seq:(t + 1) * n_seq, :]

    def gather(ref):
        return jnp.concatenate([ref[:, t, :] for t in range(ref.shape[1])], axis=0)

    def scatter(ref, a):
        for t in range(ref.shape[1]):
            ref[:, t, :] = slab(a, t)

    def spatial_mix(v):
        out = []
        for t in range(n_t):
            acc = bias_tile[t:t + 1, :] + mix_row(t, 0) * slab(v, 0)
            for j in range(1, t + 1):
                acc = acc + mix_row(t, j) * slab(v, j)
            out.append(acc)
        return jnp.concatenate(out, axis=0)

    def conv_taps(u):
        full = [st_ref[:, j, :] for j in range(CONV_W - 1)] + [
            slab(u, t) for t in range(n_t)]
        return (jnp.concatenate(full[1:1 + n_t], axis=0),
                jnp.concatenate(full[0:n_t], axis=0))

    weights_ready, drain = _weight_stream(mm_f32_hbm, mm_bf16, mm_bf16_hbm, stage_ref,
                                          in_sem, out_sem)
    y, v, u = _layer_tile(gather(x_ref), gather(p_ref), w, spatial_mix, conv_taps,
                          weights_ready)
    scatter(y_ref, y)
    scatter(vrow_ref, v)
    scatter(conv_ref, u[(n_t - (CONV_W - 1)) * n_seq:, :])
    drain()


def _resident(shape):
    return pl.BlockSpec(shape, lambda *_: (0,) * len(shape),
                        pipeline_mode=pl.Buffered(1))


def kernel(x_prompt, x_sample, state_conv, p_prompt, p_sample, norm_g, w_in, ln_v_g,
           ln_v_b, w_s, b_s, conv_w, w_a_out, w_b_out, w_o, pe_norm_g, w_pe_gate,
           w_pe_proj, final_norm_g):
    assert w_in.shape[0] == 1, "single-layer step only"
    batch, seq, d = x_prompt.shape
    n_seq, n_t, _ = x_sample.shape
    p_dim = p_prompt.shape[-1]
    tm = PROMPT_TILE
    assert seq % tm == 0 and tm % CHUNK == 0 and d == N_HEADS * HEAD_W
    assert CONV_W - 1 <= n_t <= CHUNK
    f32, bf = jnp.float32, jnp.bfloat16

    row2 = lambda a: a.reshape(1, -1)
    small = dict(norm_g=row2(norm_g[0]), ln_g=row2(ln_v_g[0]), ln_b=row2(ln_v_b[0]),
                 mix=w_s[0], mix_b=b_s[0], conv_w=jnp.transpose(conv_w, (1, 0, 2)),
                 pe_g=row2(pe_norm_g[0]), fin_g=row2(final_norm_g))
    mm_f32 = dict(w_in=w_in[0], w_a=w_a_out[0], w_b=w_b_out[0], w_o=w_o[0],
                  w_pg=w_pe_gate[0], w_pp=w_pe_proj[0])
    mm_in = tuple(mm_f32[k] for k in MATMUL_NAMES)

    def specs(arrs):
        return [_resident(a.shape) for a in arrs]

    hbm = pl.BlockSpec(memory_space=pltpu.HBM)

    sample_in = (x_sample, p_sample[0], state_conv[0]) + tuple(small[k] for k in SMALL_NAMES)
    out_shapes = ((n_seq, n_t, d), (n_seq, CONV_W - 1, d), (n_seq, n_t, d), (CHUNK, d))
    tril_shape = (N_HEADS, CHUNK, CHUNK)
    outs = pl.pallas_call(
        _sample_kernel,
        grid=(1,),
        in_specs=specs(sample_in) + [hbm] * N_MM,
        out_specs=[_resident(s) for s in out_shapes + (tril_shape,)] + [hbm] * N_MM,
        out_shape=[jax.ShapeDtypeStruct(s, f32) for s in out_shapes]
        + [jax.ShapeDtypeStruct(tril_shape, bf)]
        + [jax.ShapeDtypeStruct(a.shape, bf) for a in mm_in],
        scratch_shapes=[pltpu.VMEM(a.shape, bf) for a in mm_in]
        + [pltpu.VMEM((STAGE_SLOTS, STAGE_ROWS, STAGE_COLS), f32),
           pltpu.SemaphoreType.DMA((STAGE_SLOTS,)),
           pltpu.SemaphoreType.DMA((len(WEIGHT_USE_ORDER),))],
        compiler_params=pltpu.CompilerParams(
            dimension_semantics=("arbitrary",),
            vmem_limit_bytes=V7X_VMEM_LIMIT_BYTES),
        name="sample_layer",
    )(*sample_in, *mm_in)
    y_s, conv_s, v_s, bias_tile, mix_tril = outs[:5]
    mm_bf16 = tuple(outs[5:])

    prompt_w = tuple(dict(small, mix=mix_tril, mix_b=bias_tile)[k]
                     for k in SMALL_NAMES) + mm_bf16
    y_p, conv_p, v_p = pl.pallas_call(
        _prompt_kernel,
        grid=(batch, seq // tm),
        in_specs=[pl.BlockSpec((1, tm, d), lambda b, i: (b, i, 0)),
                  pl.BlockSpec((1, tm, p_dim), lambda b, i: (b, i, 0))] + specs(prompt_w),
        out_specs=[pl.BlockSpec((1, tm, d), lambda b, i: (b, i, 0)),
                   pl.BlockSpec((1, 1, CONV_W - 1, d), lambda b, i: (0, b, 0, 0)),
                   pl.BlockSpec((1, 1, CHUNK, d), lambda b, i: (0, b, 0, 0))],
        out_shape=[jax.ShapeDtypeStruct((batch, seq, d), f32),
                   jax.ShapeDtypeStruct((1, batch, CONV_W - 1, d), f32),
                   jax.ShapeDtypeStruct((1, batch, CHUNK, d), f32)],
        scratch_shapes=[pltpu.VMEM((tm + CARRY_ROWS, d), f32)],
        compiler_params=pltpu.CompilerParams(
            dimension_semantics=("arbitrary", "arbitrary"),
            vmem_limit_bytes=V7X_VMEM_LIMIT_BYTES),
        name="prompt_layer",
    )(x_prompt, p_prompt[0], *prompt_w)

    return (y_p, y_s, conv_p, conv_s[None], v_p, v_s[None])
```

```python
import math

import jax
import jax.numpy as jnp
from jax import lax
from jax.experimental import pallas as pl
from jax.experimental.pallas import tpu as pltpu

CHUNK = 128
N_HEADS = 8
HEAD_W = 128
CONV_W = 3
EPS = 1e-6
LN_EPS = 1e-5
COL_U, COL_V, COL_GA, COL_C, COL_B, COL_H, COL_GB, COL_MA, COL_MB = range(9)

V7X_VMEM_LIMIT_BYTES = 60000 * 1024
PROMPT_TILE = 512
CARRY_ROWS = 8
STAGE_ROWS, STAGE_COLS = 256, 1024
STAGE_SLOTS = 8


def _gelu_of_twice(hx):
    c = math.sqrt(2.0 / math.pi)
    return hx + hx * jnp.tanh(hx * (2.0 * c + (8.0 * c * 0.044715) * (hx * hx)))


def _twice_sigmoid_of_twice(hx):
    return jnp.tanh(hx) + 1.0


def _silu_of_twice(hx):
    return hx + hx * jnp.tanh(hx)


def _rms_norm(x, g):
    return x * lax.rsqrt(jnp.mean(x * x, axis=-1, keepdims=True) + EPS) * g


def _layer_norm(x, g, b):
    mu = jnp.mean(x, axis=-1, keepdims=True)
    xc = x - mu
    var = jnp.mean(xc * xc, axis=-1, keepdims=True)
    return xc * lax.rsqrt(var + LN_EPS) * g + b


def _dot(a, b):
    return jnp.dot(a, b, preferred_element_type=jnp.float32)


SMALL_NAMES = ("norm_g", "ln_g", "ln_b", "mix", "mix_b", "conv_w", "pe_g", "fin_g")
MATMUL_NAMES = ("w_in", "w_a", "w_b", "w_o", "w_pg", "w_pp")
N_SMALL, N_MM = len(SMALL_NAMES), len(MATMUL_NAMES)
WEIGHT_USE_ORDER = (
    ("w_in", COL_V), ("w_in", COL_U), ("w_in", COL_GA), ("w_in", COL_C), ("w_in", COL_H),
    ("w_in", COL_B), ("w_in", COL_GB), ("w_in", COL_MA), "w_a", ("w_in", COL_MB), "w_b",
    "w_o", "w_pg", "w_pp")
HALVED_BLOCKS = (("w_in", COL_U), ("w_in", COL_V), ("w_in", COL_GA), ("w_in", COL_GB),
                 ("w_in", COL_MA), ("w_in", COL_MB), "w_pg", "w_a", "w_b", "w_pp")
MXU_TILE_N = 256
PAIRED_BLOCKS = {COL_C: COL_H}
DST_BLOCK = {COL_U: 0, COL_V: 1, COL_GA: 2, COL_C: 3, COL_B: 5, COL_GB: 6, COL_MA: 7,
             COL_MB: 8}


def _w_in_pieces(col, d):
    second_of = {b: a for a, b in PAIRED_BLOCKS.items()}
    if col in PAIRED_BLOCKS or col in second_of:
        first = col if col in PAIRED_BLOCKS else second_of[col]
        odd = 0 if col in PAIRED_BLOCKS else 1
        return [(i * MXU_TILE_N, DST_BLOCK[first] * d + (2 * i + odd) * MXU_TILE_N, MXU_TILE_N)
                for i in range(d // MXU_TILE_N)]
    return [(0, DST_BLOCK[col] * d, d)]


def _layer_tile(x, p, w, spatial_mix, conv_taps, weights_ready=lambda key: None):
    d = x.shape[-1]
    bf = jnp.bfloat16
    h = _rms_norm(x, w["norm_g"][...]).astype(bf)

    def proj(col):
        weights_ready(("w_in", col))
        return _dot(h, w["w_in"][:, DST_BLOCK[col] * d:(DST_BLOCK[col] + 1) * d])

    def proj_pair(col):
        weights_ready(("w_in", col))
        weights_ready(("w_in", PAIRED_BLOCKS[col]))
        z = _dot(h, w["w_in"][:, DST_BLOCK[col] * d:(DST_BLOCK[col] + 2) * d])
        t = MXU_TILE_N
        n = d // t
        return ([z[:, 2 * i * t:(2 * i + 1) * t] for i in range(n)],
                [z[:, (2 * i + 1) * t:(2 * i + 2) * t] for i in range(n)])

    def mm(a, name):
        weights_ready(name)
        return _dot(a.astype(bf), w[name][...])

    v = _layer_norm(_gelu_of_twice(proj(COL_V)), w["ln_g"][...], w["ln_b"][...])
    y_a = _gelu_of_twice(proj(COL_U)) * spatial_mix(v) * _silu_of_twice(proj(COL_GA))

    z_c, z_h = proj_pair(COL_C)
    u = jnp.concatenate([a * b for a, b in zip(z_c, z_h)], axis=1)
    u1, u2 = conv_taps(u)
    cw = w["conv_w"]
    conv = cw[0] * u2 + cw[1] * u1 + cw[2] * u
    y_b = proj(COL_B) * conv * _silu_of_twice(proj(COL_GB))

    merged = (_twice_sigmoid_of_twice(proj(COL_MA)) * mm(y_a, "w_a")
              + _twice_sigmoid_of_twice(proj(COL_MB)) * mm(y_b, "w_b"))
    x = x + mm(merged, "w_o")
    inv_rms = lax.rsqrt(jnp.mean(x * x, axis=-1, keepdims=True) + EPS)
    gate = _twice_sigmoid_of_twice(inv_rms * mm(x * w["pe_g"][...], "w_pg"))
    x = x + gate * mm(p, "w_pp")
    return _rms_norm(x, w["fin_g"][...]), v, u


def _weight_stream(f32_hbm, bf16_vmem, bf16_hbm, stage_ref, in_sem, out_sem):
    index = dict(zip(MATMUL_NAMES, range(N_MM)))
    chunks, block_end, block_cols = [], [], []
    pieces = []
    for key in WEIGHT_USE_ORDER:
        name, col = key if isinstance(key, tuple) else (key, None)
        rows, cols = f32_hbm[index[name]].shape
        rb = min(rows, STAGE_ROWS)
        n_blocks = len([k for k in WEIGHT_USE_ORDER if isinstance(k, tuple) and k[0] == name])
        width = cols if col is None else cols // n_blocks
        assert rows % rb == 0 and width % STAGE_COLS == 0
        c_lo = 0 if col is None else col * width
        for c0 in range(c_lo, c_lo + width, STAGE_COLS):
            for r0 in range(0, rows, rb):
                chunks.append((index[name], r0, rb, c0))
        block_end.append(len(chunks))
        if col is None:
            pieces.append([(0, 0, width)])
            block_cols.append((index[name], 0, width))
        else:
            assert width == STAGE_COLS
            pieces.append(_w_in_pieces(col, width))
            second_of = {b: a for a, b in PAIRED_BLOCKS.items()}
            block_cols.append(
                None if col in PAIRED_BLOCKS else
                (index[name], DST_BLOCK[second_of[col]] * width, 2 * width) if col in second_of
                else (index[name], DST_BLOCK[col] * width, width))
    n_slots = stage_ref.shape[0]
    visited = []

    def fetch(k):
        wi, r0, rb, c0 = chunks[k]
        return pltpu.make_async_copy(
            f32_hbm[wi].at[pl.ds(r0, rb), pl.ds(c0, STAGE_COLS)],
            stage_ref.at[k % n_slots, pl.ds(0, rb), :], in_sem.at[k % n_slots])

    def export(i):
        wi, c_lo, width = block_cols[i]
        return pltpu.make_async_copy(bf16_vmem[wi].at[:, pl.ds(c_lo, width)],
                                     bf16_hbm[wi].at[:, pl.ds(c_lo, width)], out_sem.at[i])

    for k in range(min(n_slots, len(chunks))):
        fetch(k).start()

    def weights_ready(key):
        i = len(visited)
        assert key == WEIGHT_USE_ORDER[i], (key, WEIGHT_USE_ORDER[i])
        visited.append(key)
        for k in range(block_end[i - 1] if i else 0, block_end[i]):
            wi, r0, rb, c0 = chunks[k]
            fetch(k).wait()
            chunk = stage_ref[k % n_slots, 0:rb, :]
            if key in HALVED_BLOCKS:
                chunk = 0.5 * chunk
            chunk = chunk.astype(jnp.bfloat16)
            c_src = c0 if isinstance(key, str) else 0
            for src, dst, wd in pieces[i]:
                bf16_vmem[wi][pl.ds(r0, rb), pl.ds(c_src + dst, wd)] = chunk[:, src:src + wd]
            if k + n_slots < len(chunks):
                fetch(k + n_slots).start()
        if block_cols[i] is not None:
            export(i).start()

    def drain():
        assert len(visited) == len(WEIGHT_USE_ORDER)
        for i in range(len(WEIGHT_USE_ORDER)):
            if block_cols[i] is not None:
                export(i).wait()

    return weights_ready, drain


def _prompt_kernel(x_ref, p_ref, *rest):
    w = dict(zip(SMALL_NAMES + MATMUL_NAMES, rest[:N_SMALL + N_MM]))
    y_ref, conv_ref, vrow_ref, pbuf_ref = rest[N_SMALL + N_MM:]
    tm, d = x_ref.shape[1], x_ref.shape[2]
    bf = jnp.bfloat16

    @pl.when(pl.program_id(1) == 0)
    def _():
        pbuf_ref[0:CARRY_ROWS, :] = jnp.zeros((CARRY_ROWS, d), jnp.float32)

    def spatial_mix(v):
        v_b = v.astype(bf)
        s_rows = []
        for c in range(tm // CHUNK):
            s_heads = [_dot(w["mix"][hd], v_b[c * CHUNK:(c + 1) * CHUNK,
                                              hd * HEAD_W:(hd + 1) * HEAD_W])
                       for hd in range(N_HEADS)]
            s_rows.append(jnp.concatenate(s_heads, axis=1) + w["mix_b"][...])
        return jnp.concatenate(s_rows, axis=0)

    def conv_taps(u):
        pbuf_ref[CARRY_ROWS:CARRY_ROWS + tm, :] = u
        return (pbuf_ref[CARRY_ROWS - 1:CARRY_ROWS - 1 + tm, :],
                pbuf_ref[CARRY_ROWS - 2:CARRY_ROWS - 2 + tm, :])

    y, v, _ = _layer_tile(x_ref[0], p_ref[0], w, spatial_mix, conv_taps)
    y_ref[0] = y
    vrow_ref[0, 0] = v[tm - CHUNK:, :]
    tail = pbuf_ref[tm:tm + CARRY_ROWS, :]
    pbuf_ref[0:CARRY_ROWS, :] = tail
    conv_ref[0, 0] = tail[CARRY_ROWS - (CONV_W - 1):, :]


def _sample_kernel(x_ref, p_ref, st_ref, *rest):
    small = rest[:N_SMALL]
    mm_f32_hbm = rest[N_SMALL:N_SMALL + N_MM]
    outs = rest[N_SMALL + N_MM:]
    y_ref, conv_ref, vrow_ref, bias_ref, tril_ref = outs[:5]
    mm_bf16_hbm = outs[5:5 + N_MM]
    mm_bf16 = outs[5 + N_MM:5 + 2 * N_MM]
    stage_ref, in_sem, out_sem = outs[5 + 2 * N_MM:]
    w = dict(zip(SMALL_NAMES + MATMUL_NAMES, small + mm_bf16))
    n_seq, n_t, _ = x_ref.shape

    row = lax.broadcasted_iota(jnp.int32, (CHUNK, CHUNK), 0)
    col = lax.broadcasted_iota(jnp.int32, (CHUNK, CHUNK), 1)
    for hd in range(N_HEADS):
        tril_ref[hd] = jnp.where(col <= row, w["mix"][hd], 0.0).astype(jnp.bfloat16)
    b_t = jnp.transpose(w["mix_b"][...])
    bias_tile = jnp.concatenate(
        [jnp.broadcast_to(b_t[:, hd:hd + 1], (CHUNK, HEAD_W)) for hd in range(N_HEADS)],
        axis=1)
    bias_ref[...] = bias_tile

    def mix_row(t, j):
        return jnp.concatenate(
            [jnp.broadcast_to(w["mix"][hd, t:t + 1, j:j + 1], (1, HEAD_W))
             for hd in range(N_HEADS)], axis=1)

    def slab(a, t):
        return a[t * n_seq:(t + 1) * n_seq, :]

    def gather(ref):
        return jnp.concatenate([ref[:, t, :] for t in range(ref.shape[1])], axis=0)

    def scatter(ref, a):
        for t in range(ref.shape[1]):
            ref[:, t, :] = slab(a, t)

    def spatial_mix(v):
        out = []
        for t in range(n_t):
            acc = bias_tile[t:t + 1, :] + mix_row(t, 0) * slab(v, 0)
            for j in range(1, t + 1):
                acc = acc + mix_row(t, j) * slab(v, j)
            out.append(acc)
        return jnp.concatenate(out, axis=0)

    def conv_taps(u):
        full = [st_ref[:, j, :] for j in range(CONV_W - 1)] + [
            slab(u, t) for t in range(n_t)]
        return (jnp.concatenate(full[1:1 + n_t], axis=0),
                jnp.concatenate(full[0:n_t], axis=0))

    weights_ready, drain = _weight_stream(mm_f32_hbm, mm_bf16, mm_bf16_hbm, stage_ref,
                                          in_sem, out_sem)
    y, v, u = _layer_tile(gather(x_ref), gather(p_ref), w, spatial_mix, conv_taps,
                          weights_ready)
    scatter(y_ref, y)
    scatter(vrow_ref, v)
    scatter(conv_ref, u[(n_t - (CONV_W - 1)) * n_seq:, :])
    drain()


def _resident(shape):
    return pl.BlockSpec(shape, lambda *_: (0,) * len(shape),
                        pipeline_mode=pl.Buffered(1))


def kernel(x_prompt, x_sample, state_conv, p_prompt, p_sample, norm_g, w_in, ln_v_g,
           ln_v_b, w_s, b_s, conv_w, w_a_out, w_b_out, w_o, pe_norm_g, w_pe_gate,
           w_pe_proj, final_norm_g):
    assert w_in.shape[0] == 1, "single-layer step only"
    batch, seq, d = x_prompt.shape
    n_seq, n_t, _ = x_sample.shape
    p_dim = p_prompt.shape[-1]
    tm = PROMPT_TILE
    assert seq % tm == 0 and tm % CHUNK == 0 and d == N_HEADS * HEAD_W
    assert CONV_W - 1 <= n_t <= CHUNK
    f32, bf = jnp.float32, jnp.bfloat16

    row2 = lambda a: a.reshape(1, -1)
    small = dict(norm_g=row2(norm_g[0]), ln_g=row2(ln_v_g[0]), ln_b=row2(ln_v_b[0]),
                 mix=w_s[0], mix_b=b_s[0], conv_w=jnp.transpose(conv_w, (1, 0, 2)),
                 pe_g=row2(pe_norm_g[0]), fin_g=row2(final_norm_g))
    mm_f32 = dict(w_in=w_in[0], w_a=w_a_out[0], w_b=w_b_out[0], w_o=w_o[0],
                  w_pg=w_pe_gate[0], w_pp=w_pe_proj[0])
    mm_in = tuple(mm_f32[k] for k in MATMUL_NAMES)

    def specs(arrs):
        return [_resident(a.shape) for a in arrs]

    hbm = pl.BlockSpec(memory_space=pltpu.HBM)

    sample_in = (x_sample, p_sample[0], state_conv[0]) + tuple(small[k] for k in SMALL_NAMES)
    out_shapes = ((n_seq, n_t, d), (n_seq, CONV_W - 1, d), (n_seq, n_t, d), (CHUNK, d))
    tril_shape = (N_HEADS, CHUNK, CHUNK)
    outs = pl.pallas_call(
        _sample_kernel,
        grid=(1,),
        in_specs=specs(sample_in) + [hbm] * N_MM,
        out_specs=[_resident(s) for s in out_shapes + (tril_shape,)] + [hbm] * N_MM,
        out_shape=[jax.ShapeDtypeStruct(s, f32) for s in out_shapes]
        + [jax.ShapeDtypeStruct(tril_shape, bf)]
        + [jax.ShapeDtypeStruct(a.shape, bf) for a in mm_in],
        scratch_shapes=[pltpu.VMEM(a.shape, bf) for a in mm_in]
        + [pltpu.VMEM((STAGE_SLOTS, STAGE_ROWS, STAGE_COLS), f32),
           pltpu.SemaphoreType.DMA((STAGE_SLOTS,)),
           pltpu.SemaphoreType.DMA((len(WEIGHT_USE_ORDER),))],
        compiler_params=pltpu.CompilerParams(
            dimension_semantics=("arbitrary",),
            vmem_limit_bytes=V7X_VMEM_LIMIT_BYTES),
        name="sample_layer",
    )(*sample_in, *mm_in)
    y_s, conv_s, v_s, bias_tile, mix_tril = outs[:5]
    mm_bf16 = tuple(outs[5:])

    prompt_w = tuple(dict(small, mix=mix_tril, mix_b=bias_tile)[k]
                     for k in SMALL_NAMES) + mm_bf16
    y_p, conv_p, v_p = pl.pallas_call(
        _prompt_kernel,
        grid=(batch, seq // tm),
        in_specs=[pl.BlockSpec((1, tm, d), lambda b, i: (b, i, 0)),
                  pl.BlockSpec((1, tm, p_dim), lambda b, i: (b, i, 0))] + specs(prompt_w),
        out_specs=[pl.BlockSpec((1, tm, d), lambda b, i: (b, i, 0)),
                   pl.BlockSpec((1, 1, CONV_W - 1, d), lambda b, i: (0, b, 0, 0)),
                   pl.BlockSpec((1, 1, CHUNK, d), lambda b, i: (0, b, 0, 0))],
        out_shape=[jax.ShapeDtypeStruct((batch, seq, d), f32),
                   jax.ShapeDtypeStruct((1, batch, CONV_W - 1, d), f32),
                   jax.ShapeDtypeStruct((1, batch, CHUNK, d), f32)],
        scratch_shapes=[pltpu.VMEM((tm + CARRY_ROWS, d), f32)],
        compiler_params=pltpu.CompilerParams(
            dimension_semantics=("arbitrary", "arbitrary"),
            vmem_limit_bytes=V7X_VMEM_LIMIT_BYTES),
        name="prompt_layer",
    )(x_prompt, p_prompt[0], *prompt_w)

    return (y_p, y_s, conv_p, conv_s[None], v_p, v_s[None])
```

```python
import functools
import math

import jax
import jax.numpy as jnp
from jax import lax
from jax.experimental import pallas as pl
from jax.experimental.pallas import tpu as pltpu

CHUNK = 128
N_HEADS = 8
HEAD_W = 128
CONV_W = 3
EPS = 1e-6
LN_EPS = 1e-5
COL_U, COL_V, COL_GA, COL_C, COL_B, COL_H, COL_GB, COL_MA, COL_MB = range(9)

V7X_VMEM_LIMIT_BYTES = 60000 * 1024
PROMPT_TILE = 512
CARRY_ROWS = 8
STAGE_ROWS, STAGE_COLS = 256, 1024
STAGE_SLOTS = 8


def _gelu_of_twice(hx):
    c = math.sqrt(2.0 / math.pi)
    return hx + hx * jnp.tanh(hx * (2.0 * c + (8.0 * c * 0.044715) * (hx * hx)))


def _twice_sigmoid_of_twice(hx):
    return jnp.tanh(hx) + 1.0


def _silu_of_twice(hx):
    return hx + hx * jnp.tanh(hx)


def _rms_norm(x, g):
    return x * lax.rsqrt(jnp.mean(x * x, axis=-1, keepdims=True) + EPS) * g


def _layer_norm(x, g, b):
    mu = jnp.mean(x, axis=-1, keepdims=True)
    xc = x - mu
    var = jnp.mean(xc * xc, axis=-1, keepdims=True)
    return xc * lax.rsqrt(var + LN_EPS) * g + b


def _dot(a, b):
    return jnp.dot(a, b, preferred_element_type=jnp.float32)


SMALL_NAMES = ("norm_g", "ln_g", "ln_b", "mix", "mix_b", "conv_w", "pe_g", "fin_g")
MATMUL_NAMES = ("w_in", "w_a", "w_b", "w_o", "w_pg", "w_pp")
N_SMALL, N_MM = len(SMALL_NAMES), len(MATMUL_NAMES)
WEIGHT_USE_ORDER = (
    ("w_in", COL_V), ("w_in", COL_U), ("w_in", COL_GA), ("w_in", COL_C), ("w_in", COL_H),
    ("w_in", COL_B), ("w_in", COL_GB), ("w_in", COL_MA), "w_a", ("w_in", COL_MB), "w_b",
    "w_o", "w_pg", "w_pp")
HALVED_BLOCKS = (("w_in", COL_U), ("w_in", COL_V), ("w_in", COL_GA), ("w_in", COL_GB),
                 ("w_in", COL_MA), ("w_in", COL_MB), "w_pg", "w_a", "w_b", "w_pp")


def _layer_tile(x, p, w, spatial_mix, conv_taps, weights_ready=lambda key: None):
    d = x.shape[-1]
    bf = jnp.bfloat16
    h = _rms_norm(x, w["norm_g"][...]).astype(bf)

    def proj(col):
        weights_ready(("w_in", col))
        return _dot(h, w["w_in"][:, col * d:(col + 1) * d])

    def mm(a, name):
        weights_ready(name)
        return _dot(a.astype(bf), w[name][...])

    v = _layer_norm(_gelu_of_twice(proj(COL_V)), w["ln_g"][...], w["ln_b"][...])
    y_a = _gelu_of_twice(proj(COL_U)) * spatial_mix(v) * _silu_of_twice(proj(COL_GA))

    u = proj(COL_C) * proj(COL_H)
    u1, u2 = conv_taps(u)
    cw = w["conv_w"]
    conv = cw[0] * u2 + cw[1] * u1 + cw[2] * u
    y_b = proj(COL_B) * conv * _silu_of_twice(proj(COL_GB))

    merged = (_twice_sigmoid_of_twice(proj(COL_MA)) * mm(y_a, "w_a")
              + _twice_sigmoid_of_twice(proj(COL_MB)) * mm(y_b, "w_b"))
    x = x + mm(merged, "w_o")
    inv_rms = lax.rsqrt(jnp.mean(x * x, axis=-1, keepdims=True) + EPS)
    gate = _twice_sigmoid_of_twice(inv_rms * mm(x * w["pe_g"][...], "w_pg"))
    x = x + gate * mm(p, "w_pp")
    return _rms_norm(x, w["fin_g"][...]), v, u


def _weight_stream(f32_hbm, bf16_vmem, bf16_hbm, stage_ref, in_sem, out_sem):
    index = dict(zip(MATMUL_NAMES, range(N_MM)))
    chunks, block_end, block_cols = [], [], []
    for key in WEIGHT_USE_ORDER:
        name, col = key if isinstance(key, tuple) else (key, None)
        rows, cols = f32_hbm[index[name]].shape
        rb = min(rows, STAGE_ROWS)
        n_blocks = len([k for k in WEIGHT_USE_ORDER if isinstance(k, tuple) and k[0] == name])
        width = cols if col is None else cols // n_blocks
        assert rows % rb == 0 and width % STAGE_COLS == 0
        c_lo = 0 if col is None else col * width
        for c0 in range(c_lo, c_lo + width, STAGE_COLS):
            for r0 in range(0, rows, rb):
                chunks.append((index[name], r0, rb, c0))
        block_end.append(len(chunks))
        block_cols.append((index[name], c_lo, width))
    n_slots = stage_ref.shape[0]
    visited = []

    def fetch(k):
        wi, r0, rb, c0 = chunks[k]
        return pltpu.make_async_copy(
            f32_hbm[wi].at[pl.ds(r0, rb), pl.ds(c0, STAGE_COLS)],
            stage_ref.at[k % n_slots, pl.ds(0, rb), :], in_sem.at[k % n_slots])

    def export(i):
        wi, c_lo, width = block_cols[i]
        return pltpu.make_async_copy(bf16_vmem[wi].at[:, pl.ds(c_lo, width)],
                                     bf16_hbm[wi].at[:, pl.ds(c_lo, width)], out_sem.at[i])

    for k in range(min(n_slots, len(chunks))):
        fetch(k).start()

    def weights_ready(key):
        i = len(visited)
        assert key == WEIGHT_USE_ORDER[i], (key, WEIGHT_USE_ORDER[i])
        visited.append(key)
        for k in range(block_end[i - 1] if i else 0, block_end[i]):
            wi, r0, rb, c0 = chunks[k]
            fetch(k).wait()
            chunk = stage_ref[k % n_slots, 0:rb, :]
            if key in HALVED_BLOCKS:
                chunk = 0.5 * chunk
            bf16_vmem[wi][pl.ds(r0, rb), pl.ds(c0, STAGE_COLS)] = chunk.astype(jnp.bfloat16)
            if k + n_slots < len(chunks):
                fetch(k + n_slots).start()
        export(i).start()

    def drain():
        assert len(visited) == len(WEIGHT_USE_ORDER)
        for i in range(len(WEIGHT_USE_ORDER)):
            export(i).wait()

    return weights_ready, drain


def _prompt_kernel(x_hbm, p_hbm, *rest, tm):
    weights = rest[:N_SMALL + N_MM]
    y_hbm, conv_hbm, vrow_hbm, pbuf_ref = rest[N_SMALL + N_MM:]
    batch, seq, d = x_hbm.shape
    p_dim = p_hbm.shape[-1]

    def tile(x_ref, p_ref, y_ref, conv_ref, vrow_ref):
        _prompt_tile(x_ref, p_ref, *weights, y_ref, conv_ref, vrow_ref, pbuf_ref)

    pltpu.emit_pipeline(
        tile,
        grid=(batch, seq // tm),
        in_specs=[pl.BlockSpec((1, tm, d), lambda b, i: (b, i, 0)),
                  pl.BlockSpec((1, tm, p_dim), lambda b, i: (b, i, 0))],
        out_specs=[pl.BlockSpec((1, tm, d), lambda b, i: (b, i, 0)),
                   pl.BlockSpec((1, 1, CONV_W - 1, d), lambda b, i: (0, b, 0, 0)),
                   pl.BlockSpec((1, 1, CHUNK, d), lambda b, i: (0, b, 0, 0))],
    )(x_hbm, p_hbm, y_hbm, conv_hbm, vrow_hbm)


def _prompt_tile(x_ref, p_ref, *rest):
    w = dict(zip(SMALL_NAMES + MATMUL_NAMES, rest[:N_SMALL + N_MM]))
    y_ref, conv_ref, vrow_ref, pbuf_ref = rest[N_SMALL + N_MM:]
    tm, d = x_ref.shape[1], x_ref.shape[2]
    bf = jnp.bfloat16

    @pl.when(pl.program_id(1) == 0)
    def _():
        pbuf_ref[0:CARRY_ROWS, :] = jnp.zeros((CARRY_ROWS, d), jnp.float32)

    def spatial_mix(v):
        v_b = v.astype(bf)
        s_rows = []
        for c in range(tm // CHUNK):
            s_heads = [_dot(w["mix"][hd], v_b[c * CHUNK:(c + 1) * CHUNK,
                                              hd * HEAD_W:(hd + 1) * HEAD_W])
                       for hd in range(N_HEADS)]
            s_rows.append(jnp.concatenate(s_heads, axis=1) + w["mix_b"][...])
        return jnp.concatenate(s_rows, axis=0)

    def conv_taps(u):
        pbuf_ref[CARRY_ROWS:CARRY_ROWS + tm, :] = u
        return (pbuf_ref[CARRY_ROWS - 1:CARRY_ROWS - 1 + tm, :],
                pbuf_ref[CARRY_ROWS - 2:CARRY_ROWS - 2 + tm, :])

    y, v, _ = _layer_tile(x_ref[0], p_ref[0], w, spatial_mix, conv_taps)
    y_ref[0] = y
    vrow_ref[0, 0] = v[tm - CHUNK:, :]
    tail = pbuf_ref[tm:tm + CARRY_ROWS, :]
    pbuf_ref[0:CARRY_ROWS, :] = tail
    conv_ref[0, 0] = tail[CARRY_ROWS - (CONV_W - 1):, :]


def _sample_kernel(x_ref, p_ref, st_ref, *rest):
    small = rest[:N_SMALL]
    mm_f32_hbm = rest[N_SMALL:N_SMALL + N_MM]
    outs = rest[N_SMALL + N_MM:]
    y_ref, conv_ref, vrow_ref, bias_ref, tril_ref = outs[:5]
    mm_bf16_hbm = outs[5:5 + N_MM]
    mm_bf16 = outs[5 + N_MM:5 + 2 * N_MM]
    stage_ref, in_sem, out_sem = outs[5 + 2 * N_MM:]
    w = dict(zip(SMALL_NAMES + MATMUL_NAMES, small + mm_bf16))
    n_seq, n_t, _ = x_ref.shape

    row = lax.broadcasted_iota(jnp.int32, (CHUNK, CHUNK), 0)
    col = lax.broadcasted_iota(jnp.int32, (CHUNK, CHUNK), 1)
    for hd in range(N_HEADS):
        tril_ref[hd] = jnp.where(col <= row, w["mix"][hd], 0.0).astype(jnp.bfloat16)
    b_t = jnp.transpose(w["mix_b"][...])
    bias_tile = jnp.concatenate(
        [jnp.broadcast_to(b_t[:, hd:hd + 1], (CHUNK, HEAD_W)) for hd in range(N_HEADS)],
        axis=1)
    bias_ref[...] = bias_tile

    def mix_row(t, j):
        return jnp.concatenate(
            [jnp.broadcast_to(w["mix"][hd, t:t + 1, j:j + 1], (1, HEAD_W))
             for hd in range(N_HEADS)], axis=1)

    def slab(a, t):
        return a[t * n_seq:(t + 1) * n_seq, :]

    def gather(ref):
        return jnp.concatenate([ref[:, t, :] for t in range(ref.shape[1])], axis=0)

    def scatter(ref, a):
        for t in range(ref.shape[1]):
            ref[:, t, :] = slab(a, t)

    def spatial_mix(v):
        out = []
        for t in range(n_t):
            acc = bias_tile[t:t + 1, :] + mix_row(t, 0) * slab(v, 0)
            for j in range(1, t + 1):
                acc = acc + mix_row(t, j) * slab(v, j)
            out.append(acc)
        return jnp.concatenate(out, axis=0)

    def conv_taps(u):
        full = [st_ref[:, j, :] for j in range(CONV_W - 1)] + [
            slab(u, t) for t in range(n_t)]
        return (jnp.concatenate(full[1:1 + n_t], axis=0),
                jnp.concatenate(full[0:n_t], axis=0))

    weights_ready, drain = _weight_stream(mm_f32_hbm, mm_bf16, mm_bf16_hbm, stage_ref,
                                          in_sem, out_sem)
    y, v, u = _layer_tile(gather(x_ref), gather(p_ref), w, spatial_mix, conv_taps,
                          weights_ready)
    scatter(y_ref, y)
    scatter(vrow_ref, v)
    scatter(conv_ref, u[(n_t - (CONV_W - 1)) * n_seq:, :])
    drain()


def _resident(shape):
    return pl.BlockSpec(shape, lambda *_: (0,) * len(shape),
                        pipeline_mode=pl.Buffered(1))


def kernel(x_prompt, x_sample, state_conv, p_prompt, p_sample, norm_g, w_in, ln_v_g,
           ln_v_b, w_s, b_s, conv_w, w_a_out, w_b_out, w_o, pe_norm_g, w_pe_gate,
           w_pe_proj, final_norm_g):
    assert w_in.shape[0] == 1, "single-layer step only"
    batch, seq, d = x_prompt.shape
    n_seq, n_t, _ = x_sample.shape
    p_dim = p_prompt.shape[-1]
    tm = PROMPT_TILE
    assert seq % tm == 0 and tm % CHUNK == 0 and d == N_HEADS * HEAD_W
    assert CONV_W - 1 <= n_t <= CHUNK
    f32, bf = jnp.float32, jnp.bfloat16

    row2 = lambda a: a.reshape(1, -1)
    small = dict(norm_g=row2(norm_g[0]), ln_g=row2(ln_v_g[0]), ln_b=row2(ln_v_b[0]),
                 mix=w_s[0], mix_b=b_s[0], conv_w=jnp.transpose(conv_w, (1, 0, 2)),
                 pe_g=row2(pe_norm_g[0]), fin_g=row2(final_norm_g))
    mm_f32 = dict(w_in=w_in[0], w_a=w_a_out[0], w_b=w_b_out[0], w_o=w_o[0],
                  w_pg=w_pe_gate[0], w_pp=w_pe_proj[0])
    mm_in = tuple(mm_f32[k] for k in MATMUL_NAMES)

    def specs(arrs):
        return [_resident(a.shape) for a in arrs]

    hbm = pl.BlockSpec(memory_space=pltpu.HBM)

    sample_in = (x_sample, p_sample[0], state_conv[0]) + tuple(small[k] for k in SMALL_NAMES)
    out_shapes = ((n_seq, n_t, d), (n_seq, CONV_W - 1, d), (n_seq, n_t, d), (CHUNK, d))
    tril_shape = (N_HEADS, CHUNK, CHUNK)
    outs = pl.pallas_call(
        _sample_kernel,
        grid=(1,),
        in_specs=specs(sample_in) + [hbm] * N_MM,
        out_specs=[_resident(s) for s in out_shapes + (tril_shape,)] + [hbm] * N_MM,
        out_shape=[jax.ShapeDtypeStruct(s, f32) for s in out_shapes]
        + [jax.ShapeDtypeStruct(tril_shape, bf)]
        + [jax.ShapeDtypeStruct(a.shape, bf) for a in mm_in],
        scratch_shapes=[pltpu.VMEM(a.shape, bf) for a in mm_in]
        + [pltpu.VMEM((STAGE_SLOTS, STAGE_ROWS, STAGE_COLS), f32),
           pltpu.SemaphoreType.DMA((STAGE_SLOTS,)),
           pltpu.SemaphoreType.DMA((len(WEIGHT_USE_ORDER),))],
        compiler_params=pltpu.CompilerParams(
            dimension_semantics=("arbitrary",),
            vmem_limit_bytes=V7X_VMEM_LIMIT_BYTES),
        name="sample_layer",
    )(*sample_in, *mm_in)
    y_s, conv_s, v_s, bias_tile, mix_tril = outs[:5]
    mm_bf16 = tuple(outs[5:])

    prompt_w = tuple(dict(small, mix=mix_tril, mix_b=bias_tile)[k]
                     for k in SMALL_NAMES) + mm_bf16
    vmem = pl.BlockSpec(memory_space=pltpu.VMEM)
    y_p, conv_p, v_p = pl.pallas_call(
        functools.partial(_prompt_kernel, tm=tm),
        in_specs=[hbm, hbm] + [vmem] * len(prompt_w),
        out_specs=[hbm, hbm, hbm],
        out_shape=[jax.ShapeDtypeStruct((batch, seq, d), f32),
                   jax.ShapeDtypeStruct((1, batch, CONV_W - 1, d), f32),
                   jax.ShapeDtypeStruct((1, batch, CHUNK, d), f32)],
        scratch_shapes=[pltpu.VMEM((tm + CARRY_ROWS, d), f32)],
        compiler_params=pltpu.CompilerParams(vmem_limit_bytes=V7X_VMEM_LIMIT_BYTES),
        name="prompt_layer",
    )(x_prompt, p_prompt[0], *prompt_w)

    return (y_p, y_s, conv_p, conv_s[None], v_p, v_s[None])
```

```python
import functools
import math

import jax
import jax.numpy as jnp
from jax import lax
from jax.experimental import pallas as pl
from jax.experimental.pallas import tpu as pltpu

CHUNK = 128
N_HEADS = 8
HEAD_W = 128
CONV_W = 3
EPS = 1e-6
LN_EPS = 1e-5
COL_U, COL_V, COL_GA, COL_C, COL_B, COL_H, COL_GB, COL_MA, COL_MB = range(9)

V7X_VMEM_LIMIT_BYTES = 61440 * 1024
PROMPT_TILE = 512
CARRY_ROWS = 8
STAGE_ROWS, STAGE_COLS = 256, 1024
STAGE_SLOTS = 8


def _gelu_of_twice(hx):
    c = math.sqrt(2.0 / math.pi)
    return hx + hx * jnp.tanh(hx * (2.0 * c + (8.0 * c * 0.044715) * (hx * hx)))


def _twice_sigmoid_of_twice(hx):
    return jnp.tanh(hx) + 1.0


def _silu_of_twice(hx):
    return hx + hx * jnp.tanh(hx)


def _rms_norm(x, g):
    return x * lax.rsqrt(jnp.mean(x * x, axis=-1, keepdims=True) + EPS) * g


def _layer_norm(x, g, b):
    mu = jnp.mean(x, axis=-1, keepdims=True)
    xc = x - mu
    var = jnp.mean(xc * xc, axis=-1, keepdims=True)
    return xc * lax.rsqrt(var + LN_EPS) * g + b


def _dot(a, b):
    return jnp.dot(a, b, preferred_element_type=jnp.float32)


SMALL_NAMES = ("norm_g", "ln_g", "ln_b", "mix", "mix_b", "conv_w", "pe_g", "fin_g")
MATMUL_NAMES = ("w_in", "w_a", "w_b", "w_o", "w_pg", "w_pp")
N_SMALL, N_MM = len(SMALL_NAMES), len(MATMUL_NAMES)
WEIGHT_USE_ORDER = (
    ("w_in", COL_V), ("w_in", COL_U), ("w_in", COL_GA), ("w_in", COL_C), ("w_in", COL_H),
    ("w_in", COL_B), ("w_in", COL_GB), ("w_in", COL_MA), "w_a", ("w_in", COL_MB), "w_b",
    "w_o", "w_pg", "w_pp")
HALVED_BLOCKS = (("w_in", COL_U), ("w_in", COL_V), ("w_in", COL_GA), ("w_in", COL_GB),
                 ("w_in", COL_MA), ("w_in", COL_MB), "w_pg", "w_a", "w_b", "w_pp")


def _layer_tile(x, p, w, spatial_mix, conv_taps, weights_ready=lambda key: None):
    d = x.shape[-1]
    bf = jnp.bfloat16
    h = _rms_norm(x, w["norm_g"][...]).astype(bf)

    def proj(col):
        weights_ready(("w_in", col))
        return _dot(h, w["w_in"][:, col * d:(col + 1) * d])

    def mm(a, name):
        weights_ready(name)
        return _dot(a.astype(bf), w[name][...])

    v = _layer_norm(_gelu_of_twice(proj(COL_V)), w["ln_g"][...], w["ln_b"][...])
    y_a = _gelu_of_twice(proj(COL_U)) * spatial_mix(v) * _silu_of_twice(proj(COL_GA))

    u = proj(COL_C) * proj(COL_H)
    u1, u2 = conv_taps(u)
    cw = w["conv_w"]
    conv = cw[0] * u2 + cw[1] * u1 + cw[2] * u
    y_b = proj(COL_B) * conv * _silu_of_twice(proj(COL_GB))

    merged = (_twice_sigmoid_of_twice(proj(COL_MA)) * mm(y_a, "w_a")
              + _twice_sigmoid_of_twice(proj(COL_MB)) * mm(y_b, "w_b"))
    x = x + mm(merged, "w_o")
    inv_rms = lax.rsqrt(jnp.mean(x * x, axis=-1, keepdims=True) + EPS)
    gate = _twice_sigmoid_of_twice(inv_rms * mm(x * w["pe_g"][...], "w_pg"))
    x = x + gate * mm(p, "w_pp")
    return _rms_norm(x, w["fin_g"][...]), v, u


def _weight_stream(f32_hbm, bf16_vmem, bf16_hbm, stage_ref, in_sem, out_sem):
    index = dict(zip(MATMUL_NAMES, range(N_MM)))
    chunks, block_end, block_cols = [], [], []
    for key in WEIGHT_USE_ORDER:
        name, col = key if isinstance(key, tuple) else (key, None)
        rows, cols = f32_hbm[index[name]].shape
        rb = min(rows, STAGE_ROWS)
        n_blocks = len([k for k in WEIGHT_USE_ORDER if isinstance(k, tuple) and k[0] == name])
        width = cols if col is None else cols // n_blocks
        assert rows % rb == 0 and width % STAGE_COLS == 0
        c_lo = 0 if col is None else col * width
        for c0 in range(c_lo, c_lo + width, STAGE_COLS):
            for r0 in range(0, rows, rb):
                chunks.append((index[name], r0, rb, c0))
        block_end.append(len(chunks))
        block_cols.append((index[name], c_lo, width))
    n_slots = stage_ref.shape[0]
    visited = []

    def fetch(k):
        wi, r0, rb, c0 = chunks[k]
        return pltpu.make_async_copy(
            f32_hbm[wi].at[pl.ds(r0, rb), pl.ds(c0, STAGE_COLS)],
            stage_ref.at[k % n_slots, pl.ds(0, rb), :], in_sem.at[k % n_slots])

    def export(i):
        wi, c_lo, width = block_cols[i]
        return pltpu.make_async_copy(bf16_vmem[wi].at[:, pl.ds(c_lo, width)],
                                     bf16_hbm[wi].at[:, pl.ds(c_lo, width)], out_sem.at[i])

    for k in range(min(n_slots, len(chunks))):
        fetch(k).start()

    def weights_ready(key):
        i = len(visited)
        assert key == WEIGHT_USE_ORDER[i], (key, WEIGHT_USE_ORDER[i])
        visited.append(key)
        for k in range(block_end[i - 1] if i else 0, block_end[i]):
            wi, r0, rb, c0 = chunks[k]
            fetch(k).wait()
            chunk = stage_ref[k % n_slots, 0:rb, :]
            if key in HALVED_BLOCKS:
                chunk = 0.5 * chunk
            bf16_vmem[wi][pl.ds(r0, rb), pl.ds(c0, STAGE_COLS)] = chunk.astype(jnp.bfloat16)
            if k + n_slots < len(chunks):
                fetch(k + n_slots).start()
        if bf16_hbm is not None:
            export(i).start()

    def drain():
        assert len(visited) == len(WEIGHT_USE_ORDER)
        for i in range(len(WEIGHT_USE_ORDER) if bf16_hbm is not None else 0):
            export(i).wait()

    return weights_ready, drain


def _fused_kernel(x_hbm, p_hbm, xs_ref, ps_ref, st_ref, *rest, tm):
    small = rest[:N_SMALL]
    mm_f32_hbm = rest[N_SMALL:N_SMALL + N_MM]
    outs = rest[N_SMALL + N_MM:]
    y_hbm, conv_hbm, vrow_hbm, ys_ref, cs_ref, vs_ref = outs[:6]
    pbuf_ref, bias_ref, tril_ref = outs[6:9]
    mm_bf16 = outs[9:9 + N_MM]
    batch, seq, d = x_hbm.shape
    p_dim = p_hbm.shape[-1]

    def decode_phase(stage_ref, in_sem):
        _sample_kernel(xs_ref, ps_ref, st_ref, *small, *mm_f32_hbm, ys_ref, cs_ref, vs_ref,
                       bias_ref, tril_ref, *([None] * N_MM), *mm_bf16, stage_ref, in_sem,
                       None)

    pl.run_scoped(decode_phase,
                  pltpu.VMEM((STAGE_SLOTS, STAGE_ROWS, STAGE_COLS), jnp.float32),
                  pltpu.SemaphoreType.DMA((STAGE_SLOTS,)))

    weights = dict(zip(SMALL_NAMES, small), mix=tril_ref, mix_b=bias_ref)
    weights = tuple(weights[k] for k in SMALL_NAMES) + tuple(mm_bf16)

    def tile(x_ref, p_ref, y_ref, conv_ref, vrow_ref):
        _prompt_tile(x_ref, p_ref, *weights, y_ref, conv_ref, vrow_ref, pbuf_ref)

    pltpu.emit_pipeline(
        tile,
        grid=(batch, seq // tm),
        in_specs=[pl.BlockSpec((1, tm, d), lambda b, i: (b, i, 0)),
                  pl.BlockSpec((1, tm, p_dim), lambda b, i: (b, i, 0))],
        out_specs=[pl.BlockSpec((1, tm, d), lambda b, i: (b, i, 0)),
                   pl.BlockSpec((1, 1, CONV_W - 1, d), lambda b, i: (0, b, 0, 0)),
                   pl.BlockSpec((1, 1, CHUNK, d), lambda b, i: (0, b, 0, 0))],
    )(x_hbm, p_hbm, y_hbm, conv_hbm, vrow_hbm)


def _prompt_tile(x_ref, p_ref, *rest):
    w = dict(zip(SMALL_NAMES + MATMUL_NAMES, rest[:N_SMALL + N_MM]))
    y_ref, conv_ref, vrow_ref, pbuf_ref = rest[N_SMALL + N_MM:]
    tm, d = x_ref.shape[1], x_ref.shape[2]
    bf = jnp.bfloat16

    @pl.when(pl.program_id(1) == 0)
    def _():
        pbuf_ref[0:CARRY_ROWS, :] = jnp.zeros((CARRY_ROWS, d), jnp.float32)

    def spatial_mix(v):
        v_b = v.astype(bf)
        s_rows = []
        for c in range(tm // CHUNK):
            s_heads = [_dot(w["mix"][hd], v_b[c * CHUNK:(c + 1) * CHUNK,
                                              hd * HEAD_W:(hd + 1) * HEAD_W])
                       for hd in range(N_HEADS)]
            s_rows.append(jnp.concatenate(s_heads, axis=1) + w["mix_b"][...])
        return jnp.concatenate(s_rows, axis=0)

    def conv_taps(u):
        pbuf_ref[CARRY_ROWS:CARRY_ROWS + tm, :] = u
        return (pbuf_ref[CARRY_ROWS - 1:CARRY_ROWS - 1 + tm, :],
                pbuf_ref[CARRY_ROWS - 2:CARRY_ROWS - 2 + tm, :])

    y, v, _ = _layer_tile(x_ref[0], p_ref[0], w, spatial_mix, conv_taps)
    y_ref[0] = y
    vrow_ref[0, 0] = v[tm - CHUNK:, :]
    tail = pbuf_ref[tm:tm + CARRY_ROWS, :]
    pbuf_ref[0:CARRY_ROWS, :] = tail
    conv_ref[0, 0] = tail[CARRY_ROWS - (CONV_W - 1):, :]


def _sample_kernel(x_ref, p_ref, st_ref, *rest):
    small = rest[:N_SMALL]
    mm_f32_hbm = rest[N_SMALL:N_SMALL + N_MM]
    outs = rest[N_SMALL + N_MM:]
    y_ref, conv_ref, vrow_ref, bias_ref, tril_ref = outs[:5]
    mm_bf16_hbm = outs[5:5 + N_MM]
    mm_bf16 = outs[5 + N_MM:5 + 2 * N_MM]
    stage_ref, in_sem, out_sem = outs[5 + 2 * N_MM:]
    w = dict(zip(SMALL_NAMES + MATMUL_NAMES, small + mm_bf16))
    n_seq, n_t, _ = x_ref.shape

    row = lax.broadcasted_iota(jnp.int32, (CHUNK, CHUNK), 0)
    col = lax.broadcasted_iota(jnp.int32, (CHUNK, CHUNK), 1)
    for hd in range(N_HEADS):
        tril_ref[hd] = jnp.where(col <= row, w["mix"][hd], 0.0).astype(jnp.bfloat16)
    b_t = jnp.transpose(w["mix_b"][...])
    bias_tile = jnp.concatenate(
        [jnp.broadcast_to(b_t[:, hd:hd + 1], (CHUNK, HEAD_W)) for hd in range(N_HEADS)],
        axis=1)
    bias_ref[...] = bias_tile

    def mix_row(t, j):
        return jnp.concatenate(
            [jnp.broadcast_to(w["mix"][hd, t:t + 1, j:j + 1], (1, HEAD_W))
             for hd in range(N_HEADS)], axis=1)

    def slab(a, t):
        return a[t * n_seq:(t + 1) * n_seq, :]

    def gather(ref):
        return jnp.concatenate([ref[:, t, :] for t in range(ref.shape[1])], axis=0)

    def scatter(ref, a):
        for t in range(ref.shape[1]):
            ref[:, t, :] = slab(a, t)

    def spatial_mix(v):
        out = []
        for t in range(n_t):
            acc = bias_tile[t:t + 1, :] + mix_row(t, 0) * slab(v, 0)
            for j in range(1, t + 1):
                acc = acc + mix_row(t, j) * slab(v, j)
            out.append(acc)
        return jnp.concatenate(out, axis=0)

    def conv_taps(u):
        full = [st_ref[:, j, :] for j in range(CONV_W - 1)] + [
            slab(u, t) for t in range(n_t)]
        return (jnp.concatenate(full[1:1 + n_t], axis=0),
                jnp.concatenate(full[0:n_t], axis=0))

    weights_ready, drain = _weight_stream(
        mm_f32_hbm, mm_bf16, None if mm_bf16_hbm[0] is None else mm_bf16_hbm, stage_ref,
        in_sem, out_sem)
    y, v, u = _layer_tile(gather(x_ref), gather(p_ref), w, spatial_mix, conv_taps,
                          weights_ready)
    scatter(y_ref, y)
    scatter(vrow_ref, v)
    scatter(conv_ref, u[(n_t - (CONV_W - 1)) * n_seq:, :])
    drain()


def kernel(x_prompt, x_sample, state_conv, p_prompt, p_sample, norm_g, w_in, ln_v_g,
           ln_v_b, w_s, b_s, conv_w, w_a_out, w_b_out, w_o, pe_norm_g, w_pe_gate,
           w_pe_proj, final_norm_g):
    assert w_in.shape[0] == 1, "single-layer step only"
    batch, seq, d = x_prompt.shape
    n_seq, n_t, _ = x_sample.shape
    p_dim = p_prompt.shape[-1]
    tm = PROMPT_TILE
    assert seq % tm == 0 and tm % CHUNK == 0 and d == N_HEADS * HEAD_W
    assert CONV_W - 1 <= n_t <= CHUNK
    f32, bf = jnp.float32, jnp.bfloat16

    row2 = lambda a: a.reshape(1, -1)
    small = dict(norm_g=row2(norm_g[0]), ln_g=row2(ln_v_g[0]), ln_b=row2(ln_v_b[0]),
                 mix=w_s[0], mix_b=b_s[0], conv_w=jnp.transpose(conv_w, (1, 0, 2)),
                 pe_g=row2(pe_norm_g[0]), fin_g=row2(final_norm_g))
    mm_f32 = dict(w_in=w_in[0], w_a=w_a_out[0], w_b=w_b_out[0], w_o=w_o[0],
                  w_pg=w_pe_gate[0], w_pp=w_pe_proj[0])
    mm_in = tuple(mm_f32[k] for k in MATMUL_NAMES)

    hbm = pl.BlockSpec(memory_space=pltpu.HBM)
    vmem = pl.BlockSpec(memory_space=pltpu.VMEM)
    small_in = tuple(small[k] for k in SMALL_NAMES)
    y_p, conv_p, v_p, y_s, conv_s, v_s = pl.pallas_call(
        functools.partial(_fused_kernel, tm=tm),
        in_specs=[hbm, hbm] + [vmem] * (3 + N_SMALL) + [hbm] * N_MM,
        out_specs=[hbm, hbm, hbm, vmem, vmem, vmem],
        out_shape=[jax.ShapeDtypeStruct((batch, seq, d), f32),
                   jax.ShapeDtypeStruct((1, batch, CONV_W - 1, d), f32),
                   jax.ShapeDtypeStruct((1, batch, CHUNK, d), f32),
                   jax.ShapeDtypeStruct((n_seq, n_t, d), f32),
                   jax.ShapeDtypeStruct((n_seq, CONV_W - 1, d), f32),
                   jax.ShapeDtypeStruct((n_seq, n_t, d), f32)],
        scratch_shapes=[pltpu.VMEM((tm + CARRY_ROWS, d), f32),
                        pltpu.VMEM((CHUNK, d), f32),
                        pltpu.VMEM((N_HEADS, CHUNK, CHUNK), bf)]
        + [pltpu.VMEM(a.shape, bf) for a in mm_in],
        compiler_params=pltpu.CompilerParams(vmem_limit_bytes=V7X_VMEM_LIMIT_BYTES),
        name="mixer_layer",
    )(x_prompt, p_prompt[0], x_sample, p_sample[0], state_conv[0], *small_in, *mm_in)

    return (y_p, y_s, conv_p, conv_s[None], v_p, v_s[None])
```

```python
import functools
import math

import jax
import jax.numpy as jnp
from jax import lax
from jax.experimental import pallas as pl
from jax.experimental.pallas import tpu as pltpu

CHUNK = 128
N_HEADS = 8
HEAD_W = 128
CONV_W = 3
EPS = 1e-6
LN_EPS = 1e-5
COL_U, COL_V, COL_GA, COL_C, COL_B, COL_H, COL_GB, COL_MA, COL_MB = range(9)

V7X_VMEM_LIMIT_BYTES = 61440 * 1024
PROMPT_TILE = 512
CARRY_ROWS = 8
STAGE_ROWS, STAGE_COLS = 1024, 1024
STAGE_SLOTS = 2


def _gelu_of_twice(hx):
    c = math.sqrt(2.0 / math.pi)
    return hx + hx * jnp.tanh(hx * (2.0 * c + (8.0 * c * 0.044715) * (hx * hx)))


def _twice_sigmoid_of_twice(hx):
    return jnp.tanh(hx) + 1.0


def _silu_of_twice(hx):
    return hx + hx * jnp.tanh(hx)


def _rms_norm(x, g):
    return x * lax.rsqrt(jnp.mean(x * x, axis=-1, keepdims=True) + EPS) * g


def _layer_norm(x, g, b):
    mu = jnp.mean(x, axis=-1, keepdims=True)
    xc = x - mu
    var = jnp.mean(xc * xc, axis=-1, keepdims=True)
    return xc * lax.rsqrt(var + LN_EPS) * g + b


def _dot(a, b):
    return jnp.dot(a, b, preferred_element_type=jnp.float32)


SMALL_NAMES = ("norm_g", "ln_g", "ln_b", "mix", "mix_b", "conv_w", "pe_g", "fin_g")
MATMUL_NAMES = ("w_in", "w_a", "w_b", "w_o", "w_pg", "w_pp")
N_SMALL, N_MM = len(SMALL_NAMES), len(MATMUL_NAMES)
WEIGHT_USE_ORDER = (
    ("w_in", COL_V), ("w_in", COL_U), ("w_in", COL_GA), ("w_in", COL_C), ("w_in", COL_H),
    ("w_in", COL_B), ("w_in", COL_GB), ("w_in", COL_MA), "w_a", ("w_in", COL_MB), "w_b",
    "w_o", "w_pg", "w_pp")
HALVED_BLOCKS = (("w_in", COL_U), ("w_in", COL_V), ("w_in", COL_GA), ("w_in", COL_GB),
                 ("w_in", COL_MA), ("w_in", COL_MB), "w_pg", "w_a", "w_b", "w_pp")


def _layer_tile(x, p, w, spatial_mix, conv_taps, weights_ready=lambda key: None):
    d = x.shape[-1]
    bf = jnp.bfloat16
    h = _rms_norm(x, w["norm_g"][...]).astype(bf)

    def proj(col):
        weights_ready(("w_in", col))
        return _dot(h, w["w_in"][:, col * d:(col + 1) * d])

    def mm(a, name):
        weights_ready(name)
        return _dot(a.astype(bf), w[name][...])

    v = _layer_norm(_gelu_of_twice(proj(COL_V)), w["ln_g"][...], w["ln_b"][...])
    y_a = _gelu_of_twice(proj(COL_U)) * spatial_mix(v) * _silu_of_twice(proj(COL_GA))

    u = proj(COL_C) * proj(COL_H)
    u1, u2 = conv_taps(u)
    cw = w["conv_w"]
    conv = cw[0] * u2 + cw[1] * u1 + cw[2] * u
    y_b = proj(COL_B) * conv * _silu_of_twice(proj(COL_GB))

    merged = (_twice_sigmoid_of_twice(proj(COL_MA)) * mm(y_a, "w_a")
              + _twice_sigmoid_of_twice(proj(COL_MB)) * mm(y_b, "w_b"))
    x = x + mm(merged, "w_o")
    inv_rms = lax.rsqrt(jnp.mean(x * x, axis=-1, keepdims=True) + EPS)
    gate = _twice_sigmoid_of_twice(inv_rms * mm(x * w["pe_g"][...], "w_pg"))
    x = x + gate * mm(p, "w_pp")
    return _rms_norm(x, w["fin_g"][...]), v, u


def _weight_stream(f32_hbm, bf16_vmem, bf16_hbm, stage_ref, in_sem, out_sem):
    index = dict(zip(MATMUL_NAMES, range(N_MM)))
    chunks, block_end, block_cols = [], [], []
    for key in WEIGHT_USE_ORDER:
        name, col = key if isinstance(key, tuple) else (key, None)
        rows, cols = f32_hbm[index[name]].shape
        rb = min(rows, STAGE_ROWS)
        n_blocks = len([k for k in WEIGHT_USE_ORDER if isinstance(k, tuple) and k[0] == name])
        width = cols if col is None else cols // n_blocks
        assert rows % rb == 0 and width % STAGE_COLS == 0
        c_lo = 0 if col is None else col * width
        for c0 in range(c_lo, c_lo + width, STAGE_COLS):
            for r0 in range(0, rows, rb):
                chunks.append((index[name], r0, rb, c0))
        block_end.append(len(chunks))
        block_cols.append((index[name], c_lo, width))
    n_slots = stage_ref.shape[0]
    visited = []

    def fetch(k):
        wi, r0, rb, c0 = chunks[k]
        return pltpu.make_async_copy(
            f32_hbm[wi].at[pl.ds(r0, rb), pl.ds(c0, STAGE_COLS)],
            stage_ref.at[k % n_slots, pl.ds(0, rb), :], in_sem.at[k % n_slots])

    def export(i):
        wi, c_lo, width = block_cols[i]
        return pltpu.make_async_copy(bf16_vmem[wi].at[:, pl.ds(c_lo, width)],
                                     bf16_hbm[wi].at[:, pl.ds(c_lo, width)], out_sem.at[i])

    for k in range(min(n_slots, len(chunks))):
        fetch(k).start()

    def weights_ready(key):
        i = len(visited)
        assert key == WEIGHT_USE_ORDER[i], (key, WEIGHT_USE_ORDER[i])
        visited.append(key)
        for k in range(block_end[i - 1] if i else 0, block_end[i]):
            wi, r0, rb, c0 = chunks[k]
            fetch(k).wait()
            chunk = stage_ref[k % n_slots, 0:rb, :]
            if key in HALVED_BLOCKS:
                chunk = 0.5 * chunk
            bf16_vmem[wi][pl.ds(r0, rb), pl.ds(c0, STAGE_COLS)] = chunk.astype(jnp.bfloat16)
            if k + n_slots < len(chunks):
                fetch(k + n_slots).start()
        if bf16_hbm is not None:
            export(i).start()

    def drain():
        assert len(visited) == len(WEIGHT_USE_ORDER)
        for i in range(len(WEIGHT_USE_ORDER) if bf16_hbm is not None else 0):
            export(i).wait()

    return weights_ready, drain


def _fused_kernel(x_hbm, p_hbm, xs_ref, ps_ref, st_ref, *rest, tm):
    small = rest[:N_SMALL]
    mm_f32_hbm = rest[N_SMALL:N_SMALL + N_MM]
    outs = rest[N_SMALL + N_MM:]
    y_hbm, conv_hbm, vrow_hbm, ys_ref, cs_ref, vs_ref = outs[:6]
    pbuf_ref, bias_ref, tril_ref = outs[6:9]
    mm_bf16 = outs[9:9 + N_MM]
    batch, seq, d = x_hbm.shape
    p_dim = p_hbm.shape[-1]

    def decode_phase(stage_ref, in_sem):
        _sample_kernel(xs_ref, ps_ref, st_ref, *small, *mm_f32_hbm, ys_ref, cs_ref, vs_ref,
                       bias_ref, tril_ref, *([None] * N_MM), *mm_bf16, stage_ref, in_sem,
                       None)

    pl.run_scoped(decode_phase,
                  pltpu.VMEM((STAGE_SLOTS, STAGE_ROWS, STAGE_COLS), jnp.float32),
                  pltpu.SemaphoreType.DMA((STAGE_SLOTS,)))

    weights = dict(zip(SMALL_NAMES, small), mix=tril_ref, mix_b=bias_ref)
    weights = tuple(weights[k] for k in SMALL_NAMES) + tuple(mm_bf16)

    def tile(x_ref, p_ref, y_ref, conv_ref, vrow_ref):
        _prompt_tile(x_ref, p_ref, *weights, y_ref, conv_ref, vrow_ref, pbuf_ref)

    pltpu.emit_pipeline(
        tile,
        grid=(batch, seq // tm),
        in_specs=[pl.BlockSpec((1, tm, d), lambda b, i: (b, i, 0)),
                  pl.BlockSpec((1, tm, p_dim), lambda b, i: (b, i, 0))],
        out_specs=[pl.BlockSpec((1, tm, d), lambda b, i: (b, i, 0)),
                   pl.BlockSpec((1, 1, CONV_W - 1, d), lambda b, i: (0, b, 0, 0)),
                   pl.BlockSpec((1, 1, CHUNK, d), lambda b, i: (0, b, 0, 0))],
    )(x_hbm, p_hbm, y_hbm, conv_hbm, vrow_hbm)


def _prompt_tile(x_ref, p_ref, *rest):
    w = dict(zip(SMALL_NAMES + MATMUL_NAMES, rest[:N_SMALL + N_MM]))
    y_ref, conv_ref, vrow_ref, pbuf_ref = rest[N_SMALL + N_MM:]
    tm, d = x_ref.shape[1], x_ref.shape[2]
    bf = jnp.bfloat16

    @pl.when(pl.program_id(1) == 0)
    def _():
        pbuf_ref[0:CARRY_ROWS, :] = jnp.zeros((CARRY_ROWS, d), jnp.float32)

    def spatial_mix(v):
        v_b = v.astype(bf)
        s_rows = []
        for c in range(tm // CHUNK):
            s_heads = [_dot(w["mix"][hd], v_b[c * CHUNK:(c + 1) * CHUNK,
                                              hd * HEAD_W:(hd + 1) * HEAD_W])
                       for hd in range(N_HEADS)]
            s_rows.append(jnp.concatenate(s_heads, axis=1) + w["mix_b"][...])
        return jnp.concatenate(s_rows, axis=0)

    def conv_taps(u):
        pbuf_ref[CARRY_ROWS:CARRY_ROWS + tm, :] = u
        return (pbuf_ref[CARRY_ROWS - 1:CARRY_ROWS - 1 + tm, :],
                pbuf_ref[CARRY_ROWS - 2:CARRY_ROWS - 2 + tm, :])

    y, v, _ = _layer_tile(x_ref[0], p_ref[0], w, spatial_mix, conv_taps)
    y_ref[0] = y
    vrow_ref[0, 0] = v[tm - CHUNK:, :]
    tail = pbuf_ref[tm:tm + CARRY_ROWS, :]
    pbuf_ref[0:CARRY_ROWS, :] = tail
    conv_ref[0, 0] = tail[CARRY_ROWS - (CONV_W - 1):, :]


def _sample_kernel(x_ref, p_ref, st_ref, *rest):
    small = rest[:N_SMALL]
    mm_f32_hbm = rest[N_SMALL:N_SMALL + N_MM]
    outs = rest[N_SMALL + N_MM:]
    y_ref, conv_ref, vrow_ref, bias_ref, tril_ref = outs[:5]
    mm_bf16_hbm = outs[5:5 + N_MM]
    mm_bf16 = outs[5 + N_MM:5 + 2 * N_MM]
    stage_ref, in_sem, out_sem = outs[5 + 2 * N_MM:]
    w = dict(zip(SMALL_NAMES + MATMUL_NAMES, small + mm_bf16))
    n_seq, n_t, _ = x_ref.shape

    row = lax.broadcasted_iota(jnp.int32, (CHUNK, CHUNK), 0)
    col = lax.broadcasted_iota(jnp.int32, (CHUNK, CHUNK), 1)
    for hd in range(N_HEADS):
        tril_ref[hd] = jnp.where(col <= row, w["mix"][hd], 0.0).astype(jnp.bfloat16)
    b_t = jnp.transpose(w["mix_b"][...])
    bias_tile = jnp.concatenate(
        [jnp.broadcast_to(b_t[:, hd:hd + 1], (CHUNK, HEAD_W)) for hd in range(N_HEADS)],
        axis=1)
    bias_ref[...] = bias_tile

    def mix_row(t, j):
        return jnp.concatenate(
            [jnp.broadcast_to(w["mix"][hd, t:t + 1, j:j + 1], (1, HEAD_W))
             for hd in range(N_HEADS)], axis=1)

    def slab(a, t):
        return a[t * n_seq:(t + 1) * n_seq, :]

    def gather(ref):
        return jnp.concatenate([ref[:, t, :] for t in range(ref.shape[1])], axis=0)

    def scatter(ref, a):
        for t in range(ref.shape[1]):
            ref[:, t, :] = slab(a, t)

    def spatial_mix(v):
        out = []
        for t in range(n_t):
            acc = bias_tile[t:t + 1, :] + mix_row(t, 0) * slab(v, 0)
            for j in range(1, t + 1):
                acc = acc + mix_row(t, j) * slab(v, j)
            out.append(acc)
        return jnp.concatenate(out, axis=0)

    def conv_taps(u):
        full = [st_ref[:, j, :] for j in range(CONV_W - 1)] + [
            slab(u, t) for t in range(n_t)]
        return (jnp.concatenate(full[1:1 + n_t], axis=0),
                jnp.concatenate(full[0:n_t], axis=0))

    weights_ready, drain = _weight_stream(
        mm_f32_hbm, mm_bf16, None if mm_bf16_hbm[0] is None else mm_bf16_hbm, stage_ref,
        in_sem, out_sem)
    y, v, u = _layer_tile(gather(x_ref), gather(p_ref), w, spatial_mix, conv_taps,
                          weights_ready)
    scatter(y_ref, y)
    scatter(vrow_ref, v)
    scatter(conv_ref, u[(n_t - (CONV_W - 1)) * n_seq:, :])
    drain()


def kernel(x_prompt, x_sample, state_conv, p_prompt, p_sample, norm_g, w_in, ln_v_g,
           ln_v_b, w_s, b_s, conv_w, w_a_out, w_b_out, w_o, pe_norm_g, w_pe_gate,
           w_pe_proj, final_norm_g):
    assert w_in.shape[0] == 1, "single-layer step only"
    batch, seq, d = x_prompt.shape
    n_seq, n_t, _ = x_sample.shape
    p_dim = p_prompt.shape[-1]
    tm = PROMPT_TILE
    assert seq % tm == 0 and tm % CHUNK == 0 and d == N_HEADS * HEAD_W
    assert CONV_W - 1 <= n_t <= CHUNK
    f32, bf = jnp.float32, jnp.bfloat16

    row2 = lambda a: a.reshape(1, -1)
    small = dict(norm_g=row2(norm_g[0]), ln_g=row2(ln_v_g[0]), ln_b=row2(ln_v_b[0]),
                 mix=w_s[0], mix_b=b_s[0], conv_w=jnp.transpose(conv_w, (1, 0, 2)),
                 pe_g=row2(pe_norm_g[0]), fin_g=row2(final_norm_g))
    mm_f32 = dict(w_in=w_in[0], w_a=w_a_out[0], w_b=w_b_out[0], w_o=w_o[0],
                  w_pg=w_pe_gate[0], w_pp=w_pe_proj[0])
    mm_in = tuple(mm_f32[k] for k in MATMUL_NAMES)

    hbm = pl.BlockSpec(memory_space=pltpu.HBM)
    vmem = pl.BlockSpec(memory_space=pltpu.VMEM)
    small_in = tuple(small[k] for k in SMALL_NAMES)
    y_p, conv_p, v_p, y_s, conv_s, v_s = pl.pallas_call(
        functools.partial(_fused_kernel, tm=tm),
        in_specs=[hbm, hbm] + [vmem] * (3 + N_SMALL) + [hbm] * N_MM,
        out_specs=[hbm, hbm, hbm, vmem, vmem, vmem],
        out_shape=[jax.ShapeDtypeStruct((batch, seq, d), f32),
                   jax.ShapeDtypeStruct((1, batch, CONV_W - 1, d), f32),
                   jax.ShapeDtypeStruct((1, batch, CHUNK, d), f32),
                   jax.ShapeDtypeStruct((n_seq, n_t, d), f32),
                   jax.ShapeDtypeStruct((n_seq, CONV_W - 1, d), f32),
                   jax.ShapeDtypeStruct((n_seq, n_t, d), f32)],
        scratch_shapes=[pltpu.VMEM((tm + CARRY_ROWS, d), f32),
                        pltpu.VMEM((CHUNK, d), f32),
                        pltpu.VMEM((N_HEADS, CHUNK, CHUNK), bf)]
        + [pltpu.VMEM(a.shape, bf) for a in mm_in],
        compiler_params=pltpu.CompilerParams(vmem_limit_bytes=V7X_VMEM_LIMIT_BYTES),
        name="mixer_layer",
    )(x_prompt, p_prompt[0], x_sample, p_sample[0], state_conv[0], *small_in, *mm_in)

    return (y_p, y_s, conv_p, conv_s[None], v_p, v_s[None])
```

```python
import functools
import math

import jax
import jax.numpy as jnp
from jax import lax
from jax.experimental import pallas as pl
from jax.experimental.pallas import tpu as pltpu

CHUNK = 128
N_HEADS = 8
HEAD_W = 128
CONV_W = 3
EPS = 1e-6
LN_EPS = 1e-5
COL_U, COL_V, COL_GA, COL_C, COL_B, COL_H, COL_GB, COL_MA, COL_MB = range(9)

V7X_VMEM_LIMIT_BYTES = 61440 * 1024
PROMPT_TILE = 512
CARRY_ROWS = 8
STAGE_ROWS, STAGE_COLS = 256, 1024
STAGE_SLOTS = 8


def _gelu_of_twice(hx):
    c = math.sqrt(2.0 / math.pi)
    return hx + hx * jnp.tanh(hx * (2.0 * c + (8.0 * c * 0.044715) * (hx * hx)))


def _twice_sigmoid_of_twice(hx):
    return jnp.tanh(hx) + 1.0


def _silu_of_twice(hx):
    return hx + hx * jnp.tanh(hx)


def _rms_norm(x, g):
    return x * lax.rsqrt(jnp.mean(x * x, axis=-1, keepdims=True) + EPS) * g


def _layer_norm(x, g, b):
    mu = jnp.mean(x, axis=-1, keepdims=True)
    xc = x - mu
    var = jnp.mean(xc * xc, axis=-1, keepdims=True)
    return xc * lax.rsqrt(var + LN_EPS) * g + b


def _dot(a, b):
    return jnp.dot(a, b, preferred_element_type=jnp.float32)


SMALL_NAMES = ("norm_g", "ln_g", "ln_b", "mix", "mix_b", "conv_w", "pe_g", "fin_g")
MATMUL_NAMES = ("w_in", "w_a", "w_b", "w_o", "w_pg", "w_pp")
N_SMALL, N_MM = len(SMALL_NAMES), len(MATMUL_NAMES)
WEIGHT_USE_ORDER = (
    ("w_in", COL_V), ("w_in", COL_U), ("w_in", COL_GA), ("w_in", COL_C), ("w_in", COL_H),
    ("w_in", COL_B), ("w_in", COL_GB), ("w_in", COL_MA), "w_a", ("w_in", COL_MB), "w_b",
    "w_o", "w_pg", "w_pp")
HALVED_BLOCKS = (("w_in", COL_U), ("w_in", COL_V), ("w_in", COL_GA), ("w_in", COL_GB),
                 ("w_in", COL_MA), ("w_in", COL_MB), "w_pg", "w_a", "w_b", "w_pp")


def _layer_tile(x, p, w, spatial_mix, conv_taps, weights_ready=lambda key: None):
    d = x.shape[-1]
    bf = jnp.bfloat16
    h = _rms_norm(x, w["norm_g"][...]).astype(bf)

    def proj(col):
        weights_ready(("w_in", col))
        return _dot(h, w["w_in"][:, col * d:(col + 1) * d])

    def mm(a, name):
        weights_ready(name)
        return _dot(a.astype(bf), w[name][...])

    v = _layer_norm(_gelu_of_twice(proj(COL_V)), w["ln_g"][...], w["ln_b"][...])
    y_a = _gelu_of_twice(proj(COL_U)) * spatial_mix(v) * _silu_of_twice(proj(COL_GA))

    u = proj(COL_C) * proj(COL_H)
    u1, u2 = conv_taps(u)
    cw = w["conv_w"]
    conv = cw[0] * u2 + cw[1] * u1 + cw[2] * u
    y_b = proj(COL_B) * conv * _silu_of_twice(proj(COL_GB))

    merged = (_twice_sigmoid_of_twice(proj(COL_MA)) * mm(y_a, "w_a")
              + _twice_sigmoid_of_twice(proj(COL_MB)) * mm(y_b, "w_b"))
    x = x + mm(merged, "w_o")
    inv_rms = lax.rsqrt(jnp.mean(x * x, axis=-1, keepdims=True) + EPS)
    gate = _twice_sigmoid_of_twice(inv_rms * mm(x * w["pe_g"][...], "w_pg"))
    x = x + gate * mm(p, "w_pp")
    return _rms_norm(x, w["fin_g"][...]), v, u


def _weight_stream(f32_hbm, bf16_vmem, bf16_hbm, stage_ref, in_sem, out_sem):
    index = dict(zip(MATMUL_NAMES, range(N_MM)))
    chunks, block_end, block_cols = [], [], []
    for key in WEIGHT_USE_ORDER:
        name, col = key if isinstance(key, tuple) else (key, None)
        rows, cols = f32_hbm[index[name]].shape
        rb = min(rows, STAGE_ROWS)
        n_blocks = len([k for k in WEIGHT_USE_ORDER if isinstance(k, tuple) and k[0] == name])
        width = cols if col is None else cols // n_blocks
        assert rows % rb == 0 and width % STAGE_COLS == 0
        c_lo = 0 if col is None else col * width
        for c0 in range(c_lo, c_lo + width, STAGE_COLS):
            for r0 in range(0, rows, rb):
                chunks.append((index[name], r0, rb, c0))
        block_end.append(len(chunks))
        block_cols.append((index[name], c_lo, width))
    n_slots = stage_ref.shape[0]
    visited = []

    def fetch(k):
        wi, r0, rb, c0 = chunks[k]
        return pltpu.make_async_copy(
            f32_hbm[wi].at[pl.ds(r0, rb), pl.ds(c0, STAGE_COLS)],
            stage_ref.at[k % n_slots, pl.ds(0, rb), :], in_sem.at[k % n_slots])

    def export(i):
        wi, c_lo, width = block_cols[i]
        return pltpu.make_async_copy(bf16_vmem[wi].at[:, pl.ds(c_lo, width)],
                                     bf16_hbm[wi].at[:, pl.ds(c_lo, width)], out_sem.at[i])

    for k in range(min(n_slots, len(chunks))):
        fetch(k).start()

    def weights_ready(key):
        i = len(visited)
        assert key == WEIGHT_USE_ORDER[i], (key, WEIGHT_USE_ORDER[i])
        visited.append(key)
        for k in range(block_end[i - 1] if i else 0, block_end[i]):
            wi, r0, rb, c0 = chunks[k]
            fetch(k).wait()
            chunk = stage_ref[k % n_slots, 0:rb, :]
            if key in HALVED_BLOCKS:
                chunk = 0.5 * chunk
            bf16_vmem[wi][pl.ds(r0, rb), pl.ds(c0, STAGE_COLS)] = chunk.astype(jnp.bfloat16)
            if k + n_slots < len(chunks):
                fetch(k + n_slots).start()
        if bf16_hbm is not None:
            export(i).start()

    def drain():
        assert len(visited) == len(WEIGHT_USE_ORDER)
        for i in range(len(WEIGHT_USE_ORDER) if bf16_hbm is not None else 0):
            export(i).wait()

    return weights_ready, drain


def _fused_kernel(x_hbm, p_hbm, xs_ref, ps_ref, st_ref, *rest, tm):
    small = rest[:N_SMALL]
    mm_f32_hbm = rest[N_SMALL:N_SMALL + N_MM]
    outs = rest[N_SMALL + N_MM:]
    y_hbm, conv_hbm, vrow_hbm = outs[:3]
    decode_hbm = outs[3:6]
    pbuf_ref, bias_ref, tril_ref = outs[6:9]
    mm_bf16 = outs[9:9 + N_MM]
    decode_vmem = outs[9 + N_MM:12 + N_MM]
    ys_ref, cs_ref, vs_ref = decode_vmem
    decode_sem = outs[12 + N_MM]
    batch, seq, d = x_hbm.shape
    p_dim = p_hbm.shape[-1]

    def decode_phase(stage_ref, in_sem):
        _sample_kernel(xs_ref, ps_ref, st_ref, *small, *mm_f32_hbm, ys_ref, cs_ref, vs_ref,
                       bias_ref, tril_ref, *([None] * N_MM), *mm_bf16, stage_ref, in_sem,
                       None)

    pl.run_scoped(decode_phase,
                  pltpu.VMEM((STAGE_SLOTS, STAGE_ROWS, STAGE_COLS), jnp.float32),
                  pltpu.SemaphoreType.DMA((STAGE_SLOTS,)))

    decode_copies = [pltpu.make_async_copy(src, dst, decode_sem.at[k])
                     for k, (src, dst) in enumerate(zip(decode_vmem, decode_hbm))]
    for copy in decode_copies:
        copy.start()

    weights = dict(zip(SMALL_NAMES, small), mix=tril_ref, mix_b=bias_ref)
    weights = tuple(weights[k] for k in SMALL_NAMES) + tuple(mm_bf16)

    def tile(x_ref, p_ref, y_ref, conv_ref, vrow_ref):
        _prompt_tile(x_ref, p_ref, *weights, y_ref, conv_ref, vrow_ref, pbuf_ref)

    pltpu.emit_pipeline(
        tile,
        grid=(batch, seq // tm),
        in_specs=[pl.BlockSpec((1, tm, d), lambda b, i: (b, i, 0)),
                  pl.BlockSpec((1, tm, p_dim), lambda b, i: (b, i, 0))],
        out_specs=[pl.BlockSpec((1, tm, d), lambda b, i: (b, i, 0)),
                   pl.BlockSpec((1, 1, CONV_W - 1, d), lambda b, i: (0, b, 0, 0)),
                   pl.BlockSpec((1, 1, CHUNK, d), lambda b, i: (0, b, 0, 0))],
    )(x_hbm, p_hbm, y_hbm, conv_hbm, vrow_hbm)

    for copy in decode_copies:
        copy.wait()


def _prompt_tile(x_ref, p_ref, *rest):
    w = dict(zip(SMALL_NAMES + MATMUL_NAMES, rest[:N_SMALL + N_MM]))
    y_ref, conv_ref, vrow_ref, pbuf_ref = rest[N_SMALL + N_MM:]
    tm, d = x_ref.shape[1], x_ref.shape[2]
    bf = jnp.bfloat16

    @pl.when(pl.program_id(1) == 0)
    def _():
        pbuf_ref[0:CARRY_ROWS, :] = jnp.zeros((CARRY_ROWS, d), jnp.float32)

    def spatial_mix(v):
        v_b = v.astype(bf)
        s_rows = []
        for c in range(tm // CHUNK):
            s_heads = [_dot(w["mix"][hd], v_b[c * CHUNK:(c + 1) * CHUNK,
                                              hd * HEAD_W:(hd + 1) * HEAD_W])
                       for hd in range(N_HEADS)]
            s_rows.append(jnp.concatenate(s_heads, axis=1) + w["mix_b"][...])
        return jnp.concatenate(s_rows, axis=0)

    def conv_taps(u):
        pbuf_ref[CARRY_ROWS:CARRY_ROWS + tm, :] = u
        return (pbuf_ref[CARRY_ROWS - 1:CARRY_ROWS - 1 + tm, :],
                pbuf_ref[CARRY_ROWS - 2:CARRY_ROWS - 2 + tm, :])

    y, v, _ = _layer_tile(x_ref[0], p_ref[0], w, spatial_mix, conv_taps)
    y_ref[0] = y
    vrow_ref[0, 0] = v[tm - CHUNK:, :]
    tail = pbuf_ref[tm:tm + CARRY_ROWS, :]
    pbuf_ref[0:CARRY_ROWS, :] = tail
    conv_ref[0, 0] = tail[CARRY_ROWS - (CONV_W - 1):, :]


def _sample_kernel(x_ref, p_ref, st_ref, *rest):
    small = rest[:N_SMALL]
    mm_f32_hbm = rest[N_SMALL:N_SMALL + N_MM]
    outs = rest[N_SMALL + N_MM:]
    y_ref, conv_ref, vrow_ref, bias_ref, tril_ref = outs[:5]
    mm_bf16_hbm = outs[5:5 + N_MM]
    mm_bf16 = outs[5 + N_MM:5 + 2 * N_MM]
    stage_ref, in_sem, out_sem = outs[5 + 2 * N_MM:]
    w = dict(zip(SMALL_NAMES + MATMUL_NAMES, small + mm_bf16))
    n_seq, n_t, _ = x_ref.shape

    row = lax.broadcasted_iota(jnp.int32, (CHUNK, CHUNK), 0)
    col = lax.broadcasted_iota(jnp.int32, (CHUNK, CHUNK), 1)
    for hd in range(N_HEADS):
        tril_ref[hd] = jnp.where(col <= row, w["mix"][hd], 0.0).astype(jnp.bfloat16)
    b_t = jnp.transpose(w["mix_b"][...])
    bias_tile = jnp.concatenate(
        [jnp.broadcast_to(b_t[:, hd:hd + 1], (CHUNK, HEAD_W)) for hd in range(N_HEADS)],
        axis=1)
    bias_ref[...] = bias_tile

    def mix_row(t, j):
        return jnp.concatenate(
            [jnp.broadcast_to(w["mix"][hd, t:t + 1, j:j + 1], (1, HEAD_W))
             for hd in range(N_HEADS)], axis=1)

    def slab(a, t):
        return a[t * n_seq:(t + 1) * n_seq, :]

    def gather(ref):
        return jnp.concatenate([ref[:, t, :] for t in range(ref.shape[1])], axis=0)

    def scatter(ref, a):
        for t in range(ref.shape[1]):
            ref[:, t, :] = slab(a, t)

    def spatial_mix(v):
        out = []
        for t in range(n_t):
            acc = bias_tile[t:t + 1, :] + mix_row(t, 0) * slab(v, 0)
            for j in range(1, t + 1):
                acc = acc + mix_row(t, j) * slab(v, j)
            out.append(acc)
        return jnp.concatenate(out, axis=0)

    def conv_taps(u):
        full = [st_ref[:, j, :] for j in range(CONV_W - 1)] + [
            slab(u, t) for t in range(n_t)]
        return (jnp.concatenate(full[1:1 + n_t], axis=0),
                jnp.concatenate(full[0:n_t], axis=0))

    weights_ready, drain = _weight_stream(
        mm_f32_hbm, mm_bf16, None if mm_bf16_hbm[0] is None else mm_bf16_hbm, stage_ref,
        in_sem, out_sem)
    y, v, u = _layer_tile(gather(x_ref), gather(p_ref), w, spatial_mix, conv_taps,
                          weights_ready)
    scatter(y_ref, y)
    scatter(vrow_ref, v)
    scatter(conv_ref, u[(n_t - (CONV_W - 1)) * n_seq:, :])
    drain()


def kernel(x_prompt, x_sample, state_conv, p_prompt, p_sample, norm_g, w_in, ln_v_g,
           ln_v_b, w_s, b_s, conv_w, w_a_out, w_b_out, w_o, pe_norm_g, w_pe_gate,
           w_pe_proj, final_norm_g):
    assert w_in.shape[0] == 1, "single-layer step only"
    batch, seq, d = x_prompt.shape
    n_seq, n_t, _ = x_sample.shape
    p_dim = p_prompt.shape[-1]
    tm = PROMPT_TILE
    assert seq % tm == 0 and tm % CHUNK == 0 and d == N_HEADS * HEAD_W
    assert CONV_W - 1 <= n_t <= CHUNK
    f32, bf = jnp.float32, jnp.bfloat16

    row2 = lambda a: a.reshape(1, -1)
    small = dict(norm_g=row2(norm_g[0]), ln_g=row2(ln_v_g[0]), ln_b=row2(ln_v_b[0]),
                 mix=w_s[0], mix_b=b_s[0], conv_w=jnp.transpose(conv_w, (1, 0, 2)),
                 pe_g=row2(pe_norm_g[0]), fin_g=row2(final_norm_g))
    mm_f32 = dict(w_in=w_in[0], w_a=w_a_out[0], w_b=w_b_out[0], w_o=w_o[0],
                  w_pg=w_pe_gate[0], w_pp=w_pe_proj[0])
    mm_in = tuple(mm_f32[k] for k in MATMUL_NAMES)

    hbm = pl.BlockSpec(memory_space=pltpu.HBM)
    vmem = pl.BlockSpec(memory_space=pltpu.VMEM)
    small_in = tuple(small[k] for k in SMALL_NAMES)
    y_p, conv_p, v_p, y_s, conv_s, v_s = pl.pallas_call(
        functools.partial(_fused_kernel, tm=tm),
        in_specs=[hbm, hbm] + [vmem] * (3 + N_SMALL) + [hbm] * N_MM,
        out_specs=[hbm] * 6,
        out_shape=[jax.ShapeDtypeStruct((batch, seq, d), f32),
                   jax.ShapeDtypeStruct((1, batch, CONV_W - 1, d), f32),
                   jax.ShapeDtypeStruct((1, batch, CHUNK, d), f32),
                   jax.ShapeDtypeStruct((n_seq, n_t, d), f32),
                   jax.ShapeDtypeStruct((n_seq, CONV_W - 1, d), f32),
                   jax.ShapeDtypeStruct((n_seq, n_t, d), f32)],
        scratch_shapes=[pltpu.VMEM((tm + CARRY_ROWS, d), f32),
                        pltpu.VMEM((CHUNK, d), f32),
                        pltpu.VMEM((N_HEADS, CHUNK, CHUNK), bf)]
        + [pltpu.VMEM(a.shape, bf) for a in mm_in]
        + [pltpu.VMEM((n_seq, n_t, d), f32), pltpu.VMEM((n_seq, CONV_W - 1, d), f32),
           pltpu.VMEM((n_seq, n_t, d), f32), pltpu.SemaphoreType.DMA((3,))],
        compiler_params=pltpu.CompilerParams(vmem_limit_bytes=V7X_VMEM_LIMIT_BYTES),
        name="mixer_layer",
    )(x_prompt, p_prompt[0], x_sample, p_sample[0], state_conv[0], *small_in, *mm_in)

    return (y_p, y_s, conv_p, conv_s[None], v_p, v_s[None])
```

```python
import functools
import math

import jax
import jax.numpy as jnp
from jax import lax
from jax.experimental import pallas as pl
from jax.experimental.pallas import tpu as pltpu

CHUNK = 128
N_HEADS = 8
HEAD_W = 128
CONV_W = 3
EPS = 1e-6
LN_EPS = 1e-5
COL_U, COL_V, COL_GA, COL_C, COL_B, COL_H, COL_GB, COL_MA, COL_MB = range(9)

V7X_VMEM_LIMIT_BYTES = 61440 * 1024
PROMPT_TILE = 512
CARRY_ROWS = 8
STAGE_ROWS, STAGE_COLS = 256, 1024
STAGE_SLOTS = 8


def _gelu_of_twice(hx):
    c = math.sqrt(2.0 / math.pi)
    return hx + hx * jnp.tanh(hx * (2.0 * c + (8.0 * c * 0.044715) * (hx * hx)))


def _twice_sigmoid_of_twice(hx):
    return jnp.tanh(hx) + 1.0


def _silu_of_twice(hx):
    return hx + hx * jnp.tanh(hx)


def _rms_norm(x, g):
    return x * lax.rsqrt(jnp.mean(x * x, axis=-1, keepdims=True) + EPS) * g


def _layer_norm(x, g, b):
    mu = jnp.mean(x, axis=-1, keepdims=True)
    xc = x - mu
    var = jnp.mean(xc * xc, axis=-1, keepdims=True)
    return xc * lax.rsqrt(var + LN_EPS) * g + b


def _dot(a, b):
    return jnp.dot(a, b, preferred_element_type=jnp.float32)


SMALL_NAMES = ("norm_g", "ln_g", "ln_b", "mix", "mix_b", "conv_w", "pe_g", "fin_g")
MATMUL_NAMES = ("w_in", "w_a", "w_b", "w_o", "w_pg", "w_pp")
N_SMALL, N_MM = len(SMALL_NAMES), len(MATMUL_NAMES)
WEIGHT_USE_ORDER = (
    ("w_in", COL_V), ("w_in", COL_U), ("w_in", COL_GA), ("w_in", COL_C), ("w_in", COL_H),
    ("w_in", COL_B), ("w_in", COL_GB), ("w_in", COL_MA), "w_a", ("w_in", COL_MB), "w_b",
    "w_o", "w_pg", "w_pp")
HALVED_BLOCKS = (("w_in", COL_U), ("w_in", COL_V), ("w_in", COL_GA), ("w_in", COL_GB),
                 ("w_in", COL_MA), ("w_in", COL_MB), "w_pg", "w_a", "w_b", "w_pp")


def _layer_tile(x, p, w, spatial_mix, conv_taps, weights_ready=lambda key: None):
    d = x.shape[-1]
    bf = jnp.bfloat16
    h = _rms_norm(x, w["norm_g"][...]).astype(bf)

    def proj(col):
        weights_ready(("w_in", col))
        return _dot(h, w["w_in"][:, col * d:(col + 1) * d])

    def mm(a, name):
        weights_ready(name)
        return _dot(a.astype(bf), w[name][...])

    v = _layer_norm(_gelu_of_twice(proj(COL_V)), w["ln_g"][...], w["ln_b"][...])
    y_a = _gelu_of_twice(proj(COL_U)) * spatial_mix(v) * _silu_of_twice(proj(COL_GA))

    u = proj(COL_C) * proj(COL_H)
    u1, u2 = conv_taps(u)
    cw = w["conv_w"]
    conv = cw[0] * u2 + cw[1] * u1 + cw[2] * u
    y_b = proj(COL_B) * conv * _silu_of_twice(proj(COL_GB))

    merged = (_twice_sigmoid_of_twice(proj(COL_MA)) * mm(y_a, "w_a")
              + _twice_sigmoid_of_twice(proj(COL_MB)) * mm(y_b, "w_b"))
    x = x + mm(merged, "w_o")
    inv_rms = lax.rsqrt(jnp.mean(x * x, axis=-1, keepdims=True) + EPS)
    gate = _twice_sigmoid_of_twice(inv_rms * mm(x * w["pe_g"][...], "w_pg"))
    x = x + gate * mm(p, "w_pp")
    return _rms_norm(x, w["fin_g"][...]), v, u


def _weight_stream(f32_hbm, bf16_vmem, bf16_hbm, stage_ref, in_sem, out_sem):
    index = dict(zip(MATMUL_NAMES, range(N_MM)))
    chunks, block_end, block_cols = [], [], []
    for key in WEIGHT_USE_ORDER:
        name, col = key if isinstance(key, tuple) else (key, None)
        rows, cols = f32_hbm[index[name]].shape
        rb = min(rows, STAGE_ROWS)
        n_blocks = len([k for k in WEIGHT_USE_ORDER if isinstance(k, tuple) and k[0] == name])
        width = cols if col is None else cols // n_blocks
        assert rows % rb == 0 and width % STAGE_COLS == 0
        c_lo = 0 if col is None else col * width
        for c0 in range(c_lo, c_lo + width, STAGE_COLS):
            for r0 in range(0, rows, rb):
                chunks.append((index[name], r0, rb, c0))
        block_end.append(len(chunks))
        block_cols.append((index[name], c_lo, width))
    n_slots = stage_ref.shape[0]
    visited = []

    def fetch(k):
        wi, r0, rb, c0 = chunks[k]
        return pltpu.make_async_copy(
            f32_hbm[wi].at[pl.ds(r0, rb), pl.ds(c0, STAGE_COLS)],
            stage_ref.at[k % n_slots, pl.ds(0, rb), :], in_sem.at[k % n_slots])

    def export(i):
        wi, c_lo, width = block_cols[i]
        return pltpu.make_async_copy(bf16_vmem[wi].at[:, pl.ds(c_lo, width)],
                                     bf16_hbm[wi].at[:, pl.ds(c_lo, width)], out_sem.at[i])

    for k in range(min(n_slots, len(chunks))):
        fetch(k).start()

    def weights_ready(key):
        i = len(visited)
        assert key == WEIGHT_USE_ORDER[i], (key, WEIGHT_USE_ORDER[i])
        visited.append(key)
        for k in range(block_end[i - 1] if i else 0, block_end[i]):
            wi, r0, rb, c0 = chunks[k]
            fetch(k).wait()
            chunk = stage_ref[k % n_slots, 0:rb, :]
            if key in HALVED_BLOCKS:
                chunk = 0.5 * chunk
            bf16_vmem[wi][pl.ds(r0, rb), pl.ds(c0, STAGE_COLS)] = chunk.astype(jnp.bfloat16)
            if k + n_slots < len(chunks):
                fetch(k + n_slots).start()
        if bf16_hbm is not None:
            export(i).start()

    def drain():
        assert len(visited) == len(WEIGHT_USE_ORDER)
        for i in range(len(WEIGHT_USE_ORDER) if bf16_hbm is not None else 0):
            export(i).wait()

    return weights_ready, drain


def _fused_kernel(x_hbm, p_hbm, xs_hbm, ps_hbm, st_hbm, *rest, tm):
    small = rest[:N_SMALL]
    mm_f32_hbm = rest[N_SMALL:N_SMALL + N_MM]
    outs = rest[N_SMALL + N_MM:]
    y_hbm, conv_hbm, vrow_hbm = outs[:3]
    decode_hbm = outs[3:6]
    pbuf_ref, bias_ref, tril_ref = outs[6:9]
    mm_bf16 = outs[9:9 + N_MM]
    decode_vmem = outs[9 + N_MM:12 + N_MM]
    ys_ref, cs_ref, vs_ref = decode_vmem
    decode_sem = outs[12 + N_MM]
    batch, seq, d = x_hbm.shape
    p_dim = p_hbm.shape[-1]

    def decode_phase(stage_ref, in_sem, xs_ref, ps_ref, st_ref, load_sem):
        loads = [pltpu.make_async_copy(src, dst, load_sem.at[k]) for k, (src, dst) in
                 enumerate(((xs_hbm, xs_ref), (ps_hbm, ps_ref), (st_hbm, st_ref)))]
        for load in loads:
            load.start()

        def inputs_ready():
            for load in loads:
                load.wait()

        _sample_kernel(xs_ref, ps_ref, st_ref, *small, *mm_f32_hbm, ys_ref, cs_ref, vs_ref,
                       bias_ref, tril_ref, *([None] * N_MM), *mm_bf16, stage_ref, in_sem,
                       None, inputs_ready=inputs_ready)

    pl.run_scoped(decode_phase,
                  pltpu.VMEM((STAGE_SLOTS, STAGE_ROWS, STAGE_COLS), jnp.float32),
                  pltpu.SemaphoreType.DMA((STAGE_SLOTS,)),
                  pltpu.VMEM(xs_hbm.shape, jnp.float32),
                  pltpu.VMEM(ps_hbm.shape, jnp.float32),
                  pltpu.VMEM(st_hbm.shape, jnp.float32),
                  pltpu.SemaphoreType.DMA((3,)))

    decode_copies = [pltpu.make_async_copy(src, dst, decode_sem.at[k])
                     for k, (src, dst) in enumerate(zip(decode_vmem, decode_hbm))]
    for copy in decode_copies:
        copy.start()

    weights = dict(zip(SMALL_NAMES, small), mix=tril_ref, mix_b=bias_ref)
    weights = tuple(weights[k] for k in SMALL_NAMES) + tuple(mm_bf16)

    def tile(x_ref, p_ref, y_ref, conv_ref, vrow_ref):
        _prompt_tile(x_ref, p_ref, *weights, y_ref, conv_ref, vrow_ref, pbuf_ref)

    pltpu.emit_pipeline(
        tile,
        grid=(batch, seq // tm),
        in_specs=[pl.BlockSpec((1, tm, d), lambda b, i: (b, i, 0)),
                  pl.BlockSpec((1, tm, p_dim), lambda b, i: (b, i, 0))],
        out_specs=[pl.BlockSpec((1, tm, d), lambda b, i: (b, i, 0)),
                   pl.BlockSpec((1, 1, CONV_W - 1, d), lambda b, i: (0, b, 0, 0)),
                   pl.BlockSpec((1, 1, CHUNK, d), lambda b, i: (0, b, 0, 0))],
    )(x_hbm, p_hbm, y_hbm, conv_hbm, vrow_hbm)

    for copy in decode_copies:
        copy.wait()


def _prompt_tile(x_ref, p_ref, *rest):
    w = dict(zip(SMALL_NAMES + MATMUL_NAMES, rest[:N_SMALL + N_MM]))
    y_ref, conv_ref, vrow_ref, pbuf_ref = rest[N_SMALL + N_MM:]
    tm, d = x_ref.shape[1], x_ref.shape[2]
    bf = jnp.bfloat16

    @pl.when(pl.program_id(1) == 0)
    def _():
        pbuf_ref[0:CARRY_ROWS, :] = jnp.zeros((CARRY_ROWS, d), jnp.float32)

    def spatial_mix(v):
        v_b = v.astype(bf)
        s_rows = []
        for c in range(tm // CHUNK):
            s_heads = [_dot(w["mix"][hd], v_b[c * CHUNK:(c + 1) * CHUNK,
                                              hd * HEAD_W:(hd + 1) * HEAD_W])
                       for hd in range(N_HEADS)]
            s_rows.append(jnp.concatenate(s_heads, axis=1) + w["mix_b"][...])
        return jnp.concatenate(s_rows, axis=0)

    def conv_taps(u):
        pbuf_ref[CARRY_ROWS:CARRY_ROWS + tm, :] = u
        return (pbuf_ref[CARRY_ROWS - 1:CARRY_ROWS - 1 + tm, :],
                pbuf_ref[CARRY_ROWS - 2:CARRY_ROWS - 2 + tm, :])

    y, v, _ = _layer_tile(x_ref[0], p_ref[0], w, spatial_mix, conv_taps)
    y_ref[0] = y
    vrow_ref[0, 0] = v[tm - CHUNK:, :]
    tail = pbuf_ref[tm:tm + CARRY_ROWS, :]
    pbuf_ref[0:CARRY_ROWS, :] = tail
    conv_ref[0, 0] = tail[CARRY_ROWS - (CONV_W - 1):, :]


def _sample_kernel(x_ref, p_ref, st_ref, *rest, inputs_ready=lambda: None):
    small = rest[:N_SMALL]
    mm_f32_hbm = rest[N_SMALL:N_SMALL + N_MM]
    outs = rest[N_SMALL + N_MM:]
    y_ref, conv_ref, vrow_ref, bias_ref, tril_ref = outs[:5]
    mm_bf16_hbm = outs[5:5 + N_MM]
    mm_bf16 = outs[5 + N_MM:5 + 2 * N_MM]
    stage_ref, in_sem, out_sem = outs[5 + 2 * N_MM:]
    w = dict(zip(SMALL_NAMES + MATMUL_NAMES, small + mm_bf16))
    n_seq, n_t, _ = x_ref.shape

    row = lax.broadcasted_iota(jnp.int32, (CHUNK, CHUNK), 0)
    col = lax.broadcasted_iota(jnp.int32, (CHUNK, CHUNK), 1)
    for hd in range(N_HEADS):
        tril_ref[hd] = jnp.where(col <= row, w["mix"][hd], 0.0).astype(jnp.bfloat16)
    b_t = jnp.transpose(w["mix_b"][...])
    bias_tile = jnp.concatenate(
        [jnp.broadcast_to(b_t[:, hd:hd + 1], (CHUNK, HEAD_W)) for hd in range(N_HEADS)],
        axis=1)
    bias_ref[...] = bias_tile

    def mix_row(t, j):
        return jnp.concatenate(
            [jnp.broadcast_to(w["mix"][hd, t:t + 1, j:j + 1], (1, HEAD_W))
             for hd in range(N_HEADS)], axis=1)

    def slab(a, t):
        return a[t * n_seq:(t + 1) * n_seq, :]

    def gather(ref):
        return jnp.concatenate([ref[:, t, :] for t in range(ref.shape[1])], axis=0)

    def scatter(ref, a):
        for t in range(ref.shape[1]):
            ref[:, t, :] = slab(a, t)

    def spatial_mix(v):
        out = []
        for t in range(n_t):
            acc = bias_tile[t:t + 1, :] + mix_row(t, 0) * slab(v, 0)
            for j in range(1, t + 1):
                acc = acc + mix_row(t, j) * slab(v, j)
            out.append(acc)
        return jnp.concatenate(out, axis=0)

    def conv_taps(u):
        full = [st_ref[:, j, :] for j in range(CONV_W - 1)] + [
            slab(u, t) for t in range(n_t)]
        return (jnp.concatenate(full[1:1 + n_t], axis=0),
                jnp.concatenate(full[0:n_t], axis=0))

    weights_ready, drain = _weight_stream(
        mm_f32_hbm, mm_bf16, None if mm_bf16_hbm[0] is None else mm_bf16_hbm, stage_ref,
        in_sem, out_sem)
    inputs_ready()
    y, v, u = _layer_tile(gather(x_ref), gather(p_ref), w, spatial_mix, conv_taps,
                          weights_ready)
    scatter(y_ref, y)
    scatter(vrow_ref, v)
    scatter(conv_ref, u[(n_t - (CONV_W - 1)) * n_seq:, :])
    drain()


def kernel(x_prompt, x_sample, state_conv, p_prompt, p_sample, norm_g, w_in, ln_v_g,
           ln_v_b, w_s, b_s, conv_w, w_a_out, w_b_out, w_o, pe_norm_g, w_pe_gate,
           w_pe_proj, final_norm_g):
    assert w_in.shape[0] == 1, "single-layer step only"
    batch, seq, d = x_prompt.shape
    n_seq, n_t, _ = x_sample.shape
    p_dim = p_prompt.shape[-1]
    tm = PROMPT_TILE
    assert seq % tm == 0 and tm % CHUNK == 0 and d == N_HEADS * HEAD_W
    assert CONV_W - 1 <= n_t <= CHUNK
    f32, bf = jnp.float32, jnp.bfloat16

    row2 = lambda a: a.reshape(1, -1)
    small = dict(norm_g=row2(norm_g[0]), ln_g=row2(ln_v_g[0]), ln_b=row2(ln_v_b[0]),
                 mix=w_s[0], mix_b=b_s[0], conv_w=jnp.transpose(conv_w, (1, 0, 2)),
                 pe_g=row2(pe_norm_g[0]), fin_g=row2(final_norm_g))
    mm_f32 = dict(w_in=w_in[0], w_a=w_a_out[0], w_b=w_b_out[0], w_o=w_o[0],
                  w_pg=w_pe_gate[0], w_pp=w_pe_proj[0])
    mm_in = tuple(mm_f32[k] for k in MATMUL_NAMES)

    hbm = pl.BlockSpec(memory_space=pltpu.HBM)
    vmem = pl.BlockSpec(memory_space=pltpu.VMEM)
    small_in = tuple(small[k] for k in SMALL_NAMES)
    y_p, conv_p, v_p, y_s, conv_s, v_s = pl.pallas_call(
        functools.partial(_fused_kernel, tm=tm),
        in_specs=[hbm] * 5 + [vmem] * N_SMALL + [hbm] * N_MM,
        out_specs=[hbm] * 6,
        out_shape=[jax.ShapeDtypeStruct((batch, seq, d), f32),
                   jax.ShapeDtypeStruct((1, batch, CONV_W - 1, d), f32),
                   jax.ShapeDtypeStruct((1, batch, CHUNK, d), f32),
                   jax.ShapeDtypeStruct((n_seq, n_t, d), f32),
                   jax.ShapeDtypeStruct((n_seq, CONV_W - 1, d), f32),
                   jax.ShapeDtypeStruct((n_seq, n_t, d), f32)],
        scratch_shapes=[pltpu.VMEM((tm + CARRY_ROWS, d), f32),
                        pltpu.VMEM((CHUNK, d), f32),
                        pltpu.VMEM((N_HEADS, CHUNK, CHUNK), bf)]
        + [pltpu.VMEM(a.shape, bf) for a in mm_in]
        + [pltpu.VMEM((n_seq, n_t, d), f32), pltpu.VMEM((n_seq, CONV_W - 1, d), f32),
           pltpu.VMEM((n_seq, n_t, d), f32), pltpu.SemaphoreType.DMA((3,))],
        compiler_params=pltpu.CompilerParams(vmem_limit_bytes=V7X_VMEM_LIMIT_BYTES),
        name="mixer_layer",
    )(x_prompt, p_prompt[0], x_sample, p_sample[0], state_conv[0], *small_in, *mm_in)

    return (y_p, y_s, conv_p, conv_s[None], v_p, v_s[None])
```

```python
import functools
import math

import jax
import jax.numpy as jnp
from jax import lax
from jax.experimental import pallas as pl
from jax.experimental.pallas import tpu as pltpu

CHUNK = 128
N_HEADS = 8
HEAD_W = 128
CONV_W = 3
EPS = 1e-6
LN_EPS = 1e-5
COL_U, COL_V, COL_GA, COL_C, COL_B, COL_H, COL_GB, COL_MA, COL_MB = range(9)

V7X_VMEM_LIMIT_BYTES = 61440 * 1024
PROMPT_TILE = 512
CARRY_ROWS = 8
STAGE_ROWS, STAGE_COLS = 256, 1024
STAGE_SLOTS = 8


def _gelu_of_twice(hx):
    c = math.sqrt(2.0 / math.pi)
    return hx + hx * jnp.tanh(hx * (2.0 * c + (8.0 * c * 0.044715) * (hx * hx)))


def _twice_sigmoid_of_twice(hx):
    return jnp.tanh(hx) + 1.0


def _silu_of_twice(hx):
    return hx + hx * jnp.tanh(hx)


def _rms_norm(x, g):
    return x * lax.rsqrt(jnp.mean(x * x, axis=-1, keepdims=True) + EPS) * g


def _layer_norm(x, g, b):
    mu = jnp.mean(x, axis=-1, keepdims=True)
    xc = x - mu
    var = jnp.mean(xc * xc, axis=-1, keepdims=True)
    return xc * lax.rsqrt(var + LN_EPS) * g + b


def _dot(a, b):
    return jnp.dot(a, b, preferred_element_type=jnp.float32)


SMALL_NAMES = ("norm_g", "ln_g", "ln_b", "mix", "mix_b", "conv_w", "pe_g", "fin_g")
MATMUL_NAMES = ("w_in", "w_a", "w_b", "w_o", "w_pg", "w_pp")
N_SMALL, N_MM = len(SMALL_NAMES), len(MATMUL_NAMES)
WEIGHT_USE_ORDER = (
    ("w_in", COL_V), ("w_in", COL_U), ("w_in", COL_GA), ("w_in", COL_C), ("w_in", COL_H),
    ("w_in", COL_B), ("w_in", COL_GB), ("w_in", COL_MA), "w_a", ("w_in", COL_MB), "w_b",
    "w_o", "w_pg", "w_pp")
HALVED_BLOCKS = (("w_in", COL_U), ("w_in", COL_V), ("w_in", COL_GA), ("w_in", COL_GB),
                 ("w_in", COL_MA), ("w_in", COL_MB), "w_pg", "w_a", "w_b", "w_pp")


def _layer_tile(x, p, w, spatial_mix, conv_taps, weights_ready=lambda key: None):
    d = x.shape[-1]
    bf = jnp.bfloat16
    h = _rms_norm(x, w["norm_g"][...]).astype(bf)

    def proj(col):
        weights_ready(("w_in", col))
        return _dot(h, w["w_in"][:, col * d:(col + 1) * d])

    def mm(a, name):
        weights_ready(name)
        return _dot(a.astype(bf), w[name][...])

    v = _layer_norm(_gelu_of_twice(proj(COL_V)), w["ln_g"][...], w["ln_b"][...])
    y_a = _gelu_of_twice(proj(COL_U)) * spatial_mix(v) * _silu_of_twice(proj(COL_GA))

    u = proj(COL_C) * proj(COL_H)
    u1, u2 = conv_taps(u)
    cw = w["conv_w"]
    conv = cw[0] * u2 + cw[1] * u1 + cw[2] * u
    y_b = proj(COL_B) * conv * _silu_of_twice(proj(COL_GB))

    merged = (_twice_sigmoid_of_twice(proj(COL_MA)) * mm(y_a, "w_a")
              + _twice_sigmoid_of_twice(proj(COL_MB)) * mm(y_b, "w_b"))
    x = x + mm(merged, "w_o")
    inv_rms = lax.rsqrt(jnp.mean(x * x, axis=-1, keepdims=True) + EPS)
    gate = _twice_sigmoid_of_twice(inv_rms * mm(x * w["pe_g"][...], "w_pg"))
    x = x + gate * mm(p, "w_pp")
    return _rms_norm(x, w["fin_g"][...]), v, u


def _weight_stream(f32_hbm, bf16_vmem, bf16_hbm, stage_ref, in_sem, out_sem):
    index = dict(zip(MATMUL_NAMES, range(N_MM)))
    chunks, block_end, block_cols = [], [], []
    for key in WEIGHT_USE_ORDER:
        name, col = key if isinstance(key, tuple) else (key, None)
        rows, cols = f32_hbm[index[name]].shape
        rb = min(rows, STAGE_ROWS)
        n_blocks = len([k for k in WEIGHT_USE_ORDER if isinstance(k, tuple) and k[0] == name])
        width = cols if col is None else cols // n_blocks
        assert rows % rb == 0 and width % STAGE_COLS == 0
        c_lo = 0 if col is None else col * width
        for c0 in range(c_lo, c_lo + width, STAGE_COLS):
            for r0 in range(0, rows, rb):
                chunks.append((index[name], r0, rb, c0))
        block_end.append(len(chunks))
        block_cols.append((index[name], c_lo, width))
    n_slots = stage_ref.shape[0]
    visited = []

    def fetch(k):
        wi, r0, rb, c0 = chunks[k]
        return pltpu.make_async_copy(
            f32_hbm[wi].at[pl.ds(r0, rb), pl.ds(c0, STAGE_COLS)],
            stage_ref.at[k % n_slots, pl.ds(0, rb), :], in_sem.at[k % n_slots])

    def export(i):
        wi, c_lo, width = block_cols[i]
        return pltpu.make_async_copy(bf16_vmem[wi].at[:, pl.ds(c_lo, width)],
                                     bf16_hbm[wi].at[:, pl.ds(c_lo, width)], out_sem.at[i])

    for k in range(min(n_slots, len(chunks))):
        fetch(k).start()

    def weights_ready(key):
        i = len(visited)
        assert key == WEIGHT_USE_ORDER[i], (key, WEIGHT_USE_ORDER[i])
        visited.append(key)
        for k in range(block_end[i - 1] if i else 0, block_end[i]):
            wi, r0, rb, c0 = chunks[k]
            fetch(k).wait()
            chunk = stage_ref[k % n_slots, 0:rb, :]
            if key in HALVED_BLOCKS:
                chunk = 0.5 * chunk
            bf16_vmem[wi][pl.ds(r0, rb), pl.ds(c0, STAGE_COLS)] = chunk.astype(jnp.bfloat16)
            if k + n_slots < len(chunks):
                fetch(k + n_slots).start()
        if bf16_hbm is not None:
            export(i).start()

    def drain():
        assert len(visited) == len(WEIGHT_USE_ORDER)
        for i in range(len(WEIGHT_USE_ORDER) if bf16_hbm is not None else 0):
            export(i).wait()

    return weights_ready, drain


def _fused_kernel(x_hbm, p_hbm, xs_ref, ps_ref, st_ref, *rest, tm):
    small = rest[:N_SMALL]
    mm_f32_hbm = rest[N_SMALL:N_SMALL + N_MM]
    outs = rest[N_SMALL + N_MM:]
    y_hbm, conv_hbm, vrow_hbm = outs[:3]
    decode_hbm = outs[3:6]
    pbuf_ref, bias_ref, tril_ref = outs[6:9]
    mm_bf16 = outs[9:9 + N_MM]
    decode_vmem = outs[9 + N_MM:12 + N_MM]
    ys_ref, cs_ref, vs_ref = decode_vmem
    decode_sem = outs[12 + N_MM]
    batch, seq, d = x_hbm.shape
    p_dim = p_hbm.shape[-1]

    def decode_phase(stage_ref, in_sem):
        _sample_kernel(xs_ref, ps_ref, st_ref, *small, *mm_f32_hbm, ys_ref, cs_ref, vs_ref,
                       bias_ref, tril_ref, *([None] * N_MM), *mm_bf16, stage_ref, in_sem,
                       None)

    pl.run_scoped(decode_phase,
                  pltpu.VMEM((STAGE_SLOTS, STAGE_ROWS, STAGE_COLS), jnp.float32),
                  pltpu.SemaphoreType.DMA((STAGE_SLOTS,)))

    decode_copies = [pltpu.make_async_copy(src, dst, decode_sem.at[k])
                     for k, (src, dst) in enumerate(zip(decode_vmem, decode_hbm))]
    for copy in decode_copies:
        copy.start()

    weights = dict(zip(SMALL_NAMES, small), mix=tril_ref, mix_b=bias_ref)
    weights = tuple(weights[k] for k in SMALL_NAMES) + tuple(mm_bf16)

    def tile(x_ref, p_ref, y_ref, conv_ref, vrow_ref):
        _prompt_tile(x_ref, p_ref, *weights, y_ref, conv_ref, vrow_ref, pbuf_ref)

    pltpu.emit_pipeline(
        tile,
        grid=(batch, seq // tm),
        in_specs=[pl.BlockSpec((1, tm, d), lambda b, i: (b, i, 0)),
                  pl.BlockSpec((1, tm, p_dim), lambda b, i: (b, i, 0))],
        out_specs=[pl.BlockSpec((1, tm, d), lambda b, i: (b, i, 0)),
                   pl.BlockSpec((1, 1, CONV_W - 1, d), lambda b, i: (0, b, 0, 0)),
                   pl.BlockSpec((1, 1, CHUNK, d), lambda b, i: (0, b, 0, 0))],
    )(x_hbm, p_hbm, y_hbm, conv_hbm, vrow_hbm)

    for copy in decode_copies:
        copy.wait()


def _prompt_tile(x_ref, p_ref, *rest):
    w = dict(zip(SMALL_NAMES + MATMUL_NAMES, rest[:N_SMALL + N_MM]))
    y_ref, conv_ref, vrow_ref, pbuf_ref = rest[N_SMALL + N_MM:]
    tm, d = x_ref.shape[1], x_ref.shape[2]
    bf = jnp.bfloat16

    @pl.when(pl.program_id(1) == 0)
    def _():
        pbuf_ref[0:CARRY_ROWS, :] = jnp.zeros((CARRY_ROWS, d), jnp.float32)

    def spatial_mix(v):
        v_b = v.astype(bf)
        s_rows = []
        for c in range(tm // CHUNK):
            s_heads = [_dot(w["mix"][hd], v_b[c * CHUNK:(c + 1) * CHUNK,
                                              hd * HEAD_W:(hd + 1) * HEAD_W])
                       for hd in range(N_HEADS)]
            s_rows.append(jnp.concatenate(s_heads, axis=1) + w["mix_b"][...])
        return jnp.concatenate(s_rows, axis=0)

    def conv_taps(u):
        pbuf_ref[CARRY_ROWS:CARRY_ROWS + tm, :] = u
        return (pbuf_ref[CARRY_ROWS - 1:CARRY_ROWS - 1 + tm, :],
                pbuf_ref[CARRY_ROWS - 2:CARRY_ROWS - 2 + tm, :])

    y, v, _ = _layer_tile(x_ref[0], p_ref[0], w, spatial_mix, conv_taps)
    y_ref[0] = y
    vrow_ref[0, 0] = v[tm - CHUNK:, :]
    tail = pbuf_ref[tm:tm + CARRY_ROWS, :]
    pbuf_ref[0:CARRY_ROWS, :] = tail
    conv_ref[0, 0] = tail[CARRY_ROWS - (CONV_W - 1):, :]


def _sample_kernel(x_ref, p_ref, st_ref, *rest):
    small = rest[:N_SMALL]
    mm_f32_hbm = rest[N_SMALL:N_SMALL + N_MM]
    outs = rest[N_SMALL + N_MM:]
    y_ref, conv_ref, vrow_ref, bias_ref, tril_ref = outs[:5]
    mm_bf16_hbm = outs[5:5 + N_MM]
    mm_bf16 = outs[5 + N_MM:5 + 2 * N_MM]
    stage_ref, in_sem, out_sem = outs[5 + 2 * N_MM:]
    w = dict(zip(SMALL_NAMES + MATMUL_NAMES, small + mm_bf16))
    n_seq, n_t, _ = x_ref.shape

    weights_ready, drain = _weight_stream(
        mm_f32_hbm, mm_bf16, None if mm_bf16_hbm[0] is None else mm_bf16_hbm, stage_ref,
        in_sem, out_sem)

    row = lax.broadcasted_iota(jnp.int32, (CHUNK, CHUNK), 0)
    col = lax.broadcasted_iota(jnp.int32, (CHUNK, CHUNK), 1)
    for hd in range(N_HEADS):
        tril_ref[hd] = jnp.where(col <= row, w["mix"][hd], 0.0).astype(jnp.bfloat16)
    b_t = jnp.transpose(w["mix_b"][...])
    bias_tile = jnp.concatenate(
        [jnp.broadcast_to(b_t[:, hd:hd + 1], (CHUNK, HEAD_W)) for hd in range(N_HEADS)],
        axis=1)
    bias_ref[...] = bias_tile

    def mix_row(t, j):
        return jnp.concatenate(
            [jnp.broadcast_to(w["mix"][hd, t:t + 1, j:j + 1], (1, HEAD_W))
             for hd in range(N_HEADS)], axis=1)

    def slab(a, t):
        return a[t * n_seq:(t + 1) * n_seq, :]

    def gather(ref):
        return jnp.concatenate([ref[:, t, :] for t in range(ref.shape[1])], axis=0)

    def scatter(ref, a):
        for t in range(ref.shape[1]):
            ref[:, t, :] = slab(a, t)

    def spatial_mix(v):
        out = []
        for t in range(n_t):
            acc = bias_tile[t:t + 1, :] + mix_row(t, 0) * slab(v, 0)
            for j in range(1, t + 1):
                acc = acc + mix_row(t, j) * slab(v, j)
            out.append(acc)
        return jnp.concatenate(out, axis=0)

    def conv_taps(u):
        full = [st_ref[:, j, :] for j in range(CONV_W - 1)] + [
            slab(u, t) for t in range(n_t)]
        return (jnp.concatenate(full[1:1 + n_t], axis=0),
                jnp.concatenate(full[0:n_t], axis=0))

    y, v, u = _layer_tile(gather(x_ref), gather(p_ref), w, spatial_mix, conv_taps,
                          weights_ready)
    scatter(y_ref, y)
    scatter(vrow_ref, v)
    scatter(conv_ref, u[(n_t - (CONV_W - 1)) * n_seq:, :])
    drain()


def kernel(x_prompt, x_sample, state_conv, p_prompt, p_sample, norm_g, w_in, ln_v_g,
           ln_v_b, w_s, b_s, conv_w, w_a_out, w_b_out, w_o, pe_norm_g, w_pe_gate,
           w_pe_proj, final_norm_g):
    assert w_in.shape[0] == 1, "single-layer step only"
    batch, seq, d = x_prompt.shape
    n_seq, n_t, _ = x_sample.shape
    p_dim = p_prompt.shape[-1]
    tm = PROMPT_TILE
    assert seq % tm == 0 and tm % CHUNK == 0 and d == N_HEADS * HEAD_W
    assert CONV_W - 1 <= n_t <= CHUNK
    f32, bf = jnp.float32, jnp.bfloat16

    row2 = lambda a: a.reshape(1, -1)
    small = dict(norm_g=row2(norm_g[0]), ln_g=row2(ln_v_g[0]), ln_b=row2(ln_v_b[0]),
                 mix=w_s[0], mix_b=b_s[0], conv_w=jnp.transpose(conv_w, (1, 0, 2)),
                 pe_g=row2(pe_norm_g[0]), fin_g=row2(final_norm_g))
    mm_f32 = dict(w_in=w_in[0], w_a=w_a_out[0], w_b=w_b_out[0], w_o=w_o[0],
                  w_pg=w_pe_gate[0], w_pp=w_pe_proj[0])
    mm_in = tuple(mm_f32[k] for k in MATMUL_NAMES)

    hbm = pl.BlockSpec(memory_space=pltpu.HBM)
    vmem = pl.BlockSpec(memory_space=pltpu.VMEM)
    small_in = tuple(small[k] for k in SMALL_NAMES)
    y_p, conv_p, v_p, y_s, conv_s, v_s = pl.pallas_call(
        functools.partial(_fused_kernel, tm=tm),
        in_specs=[hbm, hbm] + [vmem] * (3 + N_SMALL) + [hbm] * N_MM,
        out_specs=[hbm] * 6,
        out_shape=[jax.ShapeDtypeStruct((batch, seq, d), f32),
                   jax.ShapeDtypeStruct((1, batch, CONV_W - 1, d), f32),
                   jax.ShapeDtypeStruct((1, batch, CHUNK, d), f32),
                   jax.ShapeDtypeStruct((n_seq, n_t, d), f32),
                   jax.ShapeDtypeStruct((n_seq, CONV_W - 1, d), f32),
                   jax.ShapeDtypeStruct((n_seq, n_t, d), f32)],
        scratch_shapes=[pltpu.VMEM((tm + CARRY_ROWS, d), f32),
                        pltpu.VMEM((CHUNK, d), f32),
                        pltpu.VMEM((N_HEADS, CHUNK, CHUNK), bf)]
        + [pltpu.VMEM(a.shape, bf) for a in mm_in]
        + [pltpu.VMEM((n_seq, n_t, d), f32), pltpu.VMEM((n_seq, CONV_W - 1, d), f32),
           pltpu.VMEM((n_seq, n_t, d), f32), pltpu.SemaphoreType.DMA((3,))],
        compiler_params=pltpu.CompilerParams(vmem_limit_bytes=V7X_VMEM_LIMIT_BYTES),
        name="mixer_layer",
    )(x_prompt, p_prompt[0], x_sample, p_sample[0], state_conv[0], *small_in, *mm_in)

    return (y_p, y_s, conv_p, conv_s[None], v_p, v_s[None])
```
